```python
import jax, jax.numpy as jnp
from jax import lax
import numpy as np

D_MODEL = 1024
BATCH = 2
SEQ = 8192
DEPTH = 1
DEC_BATCH = 128
DEC_SEQ = 8
PAST_LEN = 8192
PAGE_SIZE = 128

HEAD_DIM = 64
N_GMLP_GROUPS = 6
GMLP_WIDTH = N_GMLP_GROUPS * HEAD_DIM
CHUNK = 128
DIL_PAIRS = ((128, 1), (512, 4), (2048, 16))
HEADS_PER_DIL = 2
ATTN_HEADS = len(DIL_PAIRS) * HEADS_PER_DIL
ATTN_WIDTH = ATTN_HEADS * HEAD_DIM
N_MEM = 256
MEM_HEADS = 4
MEM_WIDTH = MEM_HEADS * HEAD_DIM
MIX_WIDTH = GMLP_WIDTH + ATTN_WIDTH + MEM_WIDTH
N_IN = 3 * GMLP_WIDTH + 4 * ATTN_WIDTH + 2 * MEM_WIDTH
ROPE_THETA = 500000.0
ROT_DIM = HEAD_DIM // 4
EPS = 1e-6
NEG = -1e30
SCALE = HEAD_DIM ** -0.5

kernel_name = "hybrid_gmlp_dilated_memory_step"


def rms_norm(x, g):
    xf = x.astype(jnp.float32)
    y = xf * lax.rsqrt(jnp.mean(xf * xf, axis=-1, keepdims=True) + EPS)
    return (y * g.astype(jnp.float32)).astype(x.dtype)


def layer_norm(x, g, b):
    xf = x.astype(jnp.float32)
    mu = jnp.mean(xf, axis=-1, keepdims=True)
    var = jnp.mean(jnp.square(xf - mu), axis=-1, keepdims=True)
    y = (xf - mu) * lax.rsqrt(var + EPS) * g.astype(jnp.float32) + b.astype(jnp.float32)
    return y.astype(x.dtype)


def rotary(x, pos):
    half = ROT_DIM // 2
    inv = ROPE_THETA ** (-jnp.arange(half, dtype=jnp.float32) * 2.0 / ROT_DIM)
    ang = pos.astype(jnp.float32)[:, None] * inv[None, :]
    cos = jnp.cos(ang)[None, :, None, :]
    sin = jnp.sin(ang)[None, :, None, :]
    xf = x.astype(jnp.float32)
    x1, x2 = xf[..., :half], xf[..., half:ROT_DIM]
    out = jnp.concatenate([x1 * cos - x2 * sin, x2 * cos + x1 * sin, xf[..., ROT_DIM:]], axis=-1)
    return out.astype(x.dtype)


def mixer_inputs(x, pos, norm_gain, w_in, ln_g, ln_b, q_norm, k_norm, mem_q_norm):
    bx, t, _ = x.shape
    z = jnp.einsum('btd,dn->btn', rms_norm(x, norm_gain), w_in)
    sizes = (GMLP_WIDTH,) * 3 + (ATTN_WIDTH,) * 4 + (MEM_WIDTH,) * 2
    u, v, ga, q, k, vv, gb, qc, gc = jnp.split(z, np.cumsum(sizes)[:-1].tolist(), axis=-1)
    u = jax.nn.gelu(u, approximate=False)
    v = layer_norm(jax.nn.gelu(v, approximate=False), ln_g, ln_b).reshape(bx, t, N_GMLP_GROUPS, HEAD_DIM)
    q = rotary(rms_norm(q.reshape(bx, t, ATTN_HEADS, HEAD_DIM), q_norm), pos)
    k = rotary(rms_norm(k.reshape(bx, t, ATTN_HEADS, HEAD_DIM), k_norm), pos)
    vv = vv.reshape(bx, t, ATTN_HEADS, HEAD_DIM)
    qc = rms_norm(qc.reshape(bx, t, MEM_HEADS, HEAD_DIM), mem_q_norm)
    return u, v, ga, q, k, vv, gb, qc, gc


def gmlp_spatial(v, w_s, b_s):
    b, t, g, dg = v.shape
    nc = -(-t // CHUNK)
    tp = nc * CHUNK
    vp = jnp.pad(v, ((0, 0), (0, tp - t), (0, 0), (0, 0))).reshape(b, nc, CHUNK, g, dg)
    mask = jnp.tril(jnp.ones((CHUNK, CHUNK), dtype=bool))
    ws = jnp.where(mask[None], w_s, 0).astype(v.dtype)
    s = jnp.einsum('gts,bnsgd->bntgd', ws, vp) + b_s.T[None, None, :, :, None].astype(v.dtype)
    return s.reshape(b, tp, g, dg)[:, :t]


def dilated_attn_full(q, k, v, window, dil):
    b, s, h, dh = q.shape
    blk = window // dil
    sp = -(-s // window) * window
    nb = sp // window

    def to_blocks(x):
        x = jnp.pad(x, ((0, 0), (0, sp - s), (0, 0), (0, 0)))
        x = x.reshape(b, sp // dil, dil, h, dh).transpose(0, 2, 1, 3, 4)
        return x.reshape(b, dil, nb, blk, h, dh)

    def with_prev(x):
        prev = jnp.pad(x, ((0, 0), (0, 0), (1, 0), (0, 0), (0, 0), (0, 0)))[:, :, :-1]
        return jnp.concatenate([prev, x], axis=3)

    qb = to_blocks(q)
    kk = with_prev(to_blocks(k))
    vv = with_prev(to_blocks(v))
    sc = jnp.einsum('brnqhe,brnkhe->brnhqk', qb, kk).astype(jnp.float32) * SCALE
    qo = jnp.arange(blk)[:, None]
    ko = jnp.arange(2 * blk)[None, :]
    m = qo + blk - ko
    n = jnp.arange(nb)[:, None, None]
    valid = (m >= 0) & (m <= blk) & ((n * blk + qo - m) >= 0)
    sc = jnp.where(valid[None, None, :, None], sc, NEG)
    lse = jax.nn.logsumexp(sc, axis=-1)
    p = jnp.exp(sc - lse[..., None]).astype(v.dtype)
    o = jnp.einsum('brnhqk,brnkhe->brnqhe', p, vv)
    o = o.reshape(b, dil, sp // dil, h, dh).transpose(0, 2, 1, 3, 4).reshape(b, sp, h, dh)[:, :s]
    lse = lse.transpose(0, 1, 2, 4, 3).reshape(b, dil, sp // dil, h)
    lse = lse.transpose(0, 2, 1, 3).reshape(b, sp, h)[:, :s]
    return o, lse


def dilated_attn_step(q, kv_buf, k_new, v_new, window, dil):
    l = kv_buf.shape[1]
    t = q.shape[1]
    kc = jnp.concatenate([kv_buf[:, :, 0], k_new.astype(kv_buf.dtype)], axis=1)
    vc = jnp.concatenate([kv_buf[:, :, 1], v_new.astype(kv_buf.dtype)], axis=1)
    nk = window // dil + 1
    idx = l + jnp.arange(t)[:, None] - dil * jnp.arange(nk)[None, :]
    valid = idx >= 0
    idx = jnp.maximum(idx, 0)
    kg = kc[:, idx]
    vg = vc[:, idx]
    sc = jnp.einsum('bthe,btmhe->bthm', q, kg).astype(jnp.float32) * SCALE
    sc = jnp.where(valid[None, :, None, :], sc, NEG)
    lse = jax.nn.logsumexp(sc, axis=-1)
    p = jnp.exp(sc - lse[..., None]).astype(vg.dtype)
    o = jnp.einsum('bthm,btmhe->bthe', p, vg)
    new_buf = jnp.stack([kc, vc], axis=2)[:, kc.shape[1] - l:]
    return o, lse, new_buf


def combine_dilations(outs, lses):
    o = jnp.stack(outs, axis=2)
    a = jax.nn.softmax(jnp.stack(lses, axis=2).astype(jnp.float32), axis=2)
    b, t = o.shape[:2]
    return (o * a[..., None].astype(o.dtype)).reshape(b, t, ATTN_WIDTH)


def mem_keys_values(mem, mem_norm, w_mem_kv, mem_k_norm):
    b, n, _ = mem.shape
    kv = jnp.einsum('bmd,dn->bmn', rms_norm(mem, mem_norm), w_mem_kv).reshape(b, n, 2, MEM_HEADS, HEAD_DIM)
    return jnp.stack([rms_norm(kv[:, :, 0], mem_k_norm), kv[:, :, 1]], axis=2)


def mem_attention(q, kv):
    sc = jnp.einsum('bthe,bmhe->bhtm', q, kv[:, :, 0]).astype(jnp.float32) * SCALE
    p = jax.nn.softmax(sc, axis=-1).astype(kv.dtype)
    o = jnp.einsum('bhtm,bmhe->bthe', p, kv[:, :, 1])
    return o.reshape(q.shape[0], q.shape[1], MEM_WIDTH)


def layer_output(x, a, b, c, ga, gb, gc, w_out):
    mix = jnp.concatenate([a * jax.nn.silu(ga), b * jax.nn.silu(gb), c * jax.nn.silu(gc)], axis=-1)
    return x + jnp.einsum('btn,nd->btd', mix, w_out)


def setup_inputs(seed: int = 0) -> dict:
    key = jax.random.key(seed)
    ks = jax.random.split(key, 24)
    f32 = jnp.float32
    nrm = lambda k, shp: jax.random.normal(k, shp, dtype=f32)
    gain = lambda k, shp: 1.0 + 0.02 * nrm(k, shp)
    win_len = [min(w, PAST_LEN) for (w, _) in DIL_PAIRS]
    return {
        "x_prompt": nrm(ks[0], (BATCH, SEQ, D_MODEL)),
        "x_sample": nrm(ks[1], (DEC_BATCH, DEC_SEQ, D_MODEL)),
        "state_win0_kv": nrm(ks[2], (DEPTH, DEC_BATCH, win_len[0], 2, HEADS_PER_DIL, HEAD_DIM)),
        "state_win1_kv": nrm(ks[3], (DEPTH, DEC_BATCH, win_len[1], 2, HEADS_PER_DIL, HEAD_DIM)),
        "state_win2_kv": nrm(ks[4], (DEPTH, DEC_BATCH, win_len[2], 2, HEADS_PER_DIL, HEAD_DIM)),
        "cache_mem_kv": nrm(ks[5], (DEPTH, DEC_BATCH, N_MEM, 2, MEM_HEADS, HEAD_DIM)),
        "mem_prompt": nrm(ks[6], (BATCH, N_MEM, D_MODEL)),
        "norm_gain": gain(ks[7], (DEPTH, D_MODEL)),
        "w_in": nrm(ks[8], (DEPTH, D_MODEL, N_IN)) * D_MODEL ** -0.5,
        "gmlp_ln_gain": gain(ks[9], (DEPTH, GMLP_WIDTH)),
        "gmlp_ln_bias": 0.02 * nrm(ks[10], (DEPTH, GMLP_WIDTH)),
        "gmlp_w_s": nrm(ks[11], (DEPTH, N_GMLP_GROUPS, CHUNK, CHUNK)) * CHUNK ** -0.5,
        "gmlp_b_s": 1.0 + 0.02 * nrm(ks[12], (DEPTH, N_GMLP_GROUPS, CHUNK)),
        "attn_q_norm": gain(ks[13], (DEPTH, HEAD_DIM)),
        "attn_k_norm": gain(ks[14], (DEPTH, HEAD_DIM)),
        "mem_norm": gain(ks[15], (DEPTH, D_MODEL)),
        "w_mem_kv": nrm(ks[16], (DEPTH, D_MODEL, 2 * MEM_WIDTH)) * D_MODEL ** -0.5,
        "mem_q_norm": gain(ks[17], (DEPTH, HEAD_DIM)),
        "mem_k_norm": gain(ks[18], (DEPTH, HEAD_DIM)),
        "w_out": nrm(ks[19], (DEPTH, MIX_WIDTH, D_MODEL)) * MIX_WIDTH ** -0.5,
    }


def reference(x_prompt, x_sample, state_win0_kv, state_win1_kv, state_win2_kv, cache_mem_kv,
              mem_prompt, norm_gain, w_in, gmlp_ln_gain, gmlp_ln_bias, gmlp_w_s, gmlp_b_s,
              attn_q_norm, attn_k_norm, mem_norm, w_mem_kv, mem_q_norm, mem_k_norm, w_out):
    bp, s, _ = x_prompt.shape
    bd, t, _ = x_sample.shape
    pos_p = jnp.arange(s, dtype=jnp.int32)
    pos_s = PAST_LEN + jnp.arange(t, dtype=jnp.int32)
    win_in = (state_win0_kv, state_win1_kv, state_win2_kv)
    hp, hs = x_prompt, x_sample
    win_p = [[] for _ in DIL_PAIRS]
    win_s = [[] for _ in DIL_PAIRS]
    mem_p, gv_s = [], []
    for l in range(DEPTH):
        proj = (norm_gain[l], w_in[l], gmlp_ln_gain[l], gmlp_ln_bias[l],
                attn_q_norm[l], attn_k_norm[l], mem_q_norm[l])
        u, v, ga, q, k, vv, gb, qc, gc = mixer_inputs(hp, pos_p, *proj)
        a = u * gmlp_spatial(v, gmlp_w_s[l], gmlp_b_s[l]).reshape(bp, s, GMLP_WIDTH)
        outs, lses = [], []
        for g, (w, d) in enumerate(DIL_PAIRS):
            hsl = slice(g * HEADS_PER_DIL, (g + 1) * HEADS_PER_DIL)
            o, lse = dilated_attn_full(q[:, :, hsl], k[:, :, hsl], vv[:, :, hsl], w, d)
            outs.append(o)
            lses.append(lse)
            win_p[g].append(jnp.stack([k[:, :, hsl], vv[:, :, hsl]], axis=2)[:, s - min(w, s):])
        bo = combine_dilations(outs, lses)
        mkv = mem_keys_values(mem_prompt, mem_norm[l], w_mem_kv[l], mem_k_norm[l])
        mem_p.append(mkv)
        c = mem_attention(qc, mkv)
        hp = layer_output(hp, a, bo, c, ga, gb, gc, w_out[l])
        u, v, ga, q, k, vv, gb, qc, gc = mixer_inputs(hs, pos_s, *proj)
        a = u * gmlp_spatial(v, gmlp_w_s[l], gmlp_b_s[l]).reshape(bd, t, GMLP_WIDTH)
        gv_s.append(v)
        outs, lses = [], []
        for g, (w, d) in enumerate(DIL_PAIRS):
            hsl = slice(g * HEADS_PER_DIL, (g + 1) * HEADS_PER_DIL)
            o, lse, nbuf = dilated_attn_step(q[:, :, hsl], win_in[g][l], k[:, :, hsl], vv[:, :, hsl], w, d)
            outs.append(o)
            lses.append(lse)
            win_s[g].append(nbuf)
        bo = combine_dilations(outs, lses)
        c = mem_attention(qc, cache_mem_kv[l])
        hs = layer_output(hs, a, bo, c, ga, gb, gc, w_out[l])
    new_win0_p = jnp.stack(win_p[0])
    new_win1_p = jnp.stack(win_p[1])
    new_win2_p = jnp.stack(win_p[2])
    new_mem_kv_p = jnp.stack(mem_p)
    new_win0_s = jnp.stack(win_s[0])
    new_win1_s = jnp.stack(win_s[1])
    new_win2_s = jnp.stack(win_s[2])
    new_gmlp_v_s = jnp.stack(gv_s)
    return (hp, hs, new_win0_p, new_win1_p, new_win2_p, new_mem_kv_p,
            new_win0_s, new_win1_s, new_win2_s, new_gmlp_v_s)
```

```python
import functools

import numpy as np
import jax
import jax.numpy as jnp
from jax import lax
from jax.experimental import pallas as pl
from jax.experimental.pallas import tpu as pltpu

F32 = jnp.float32
BF16 = jnp.bfloat16

D_MODEL = 1024
HEAD_DIM = 64
LANES = 128
N_GMLP_GROUPS = 6
GMLP_WIDTH = N_GMLP_GROUPS * HEAD_DIM
CHUNK = 128
DIL_PAIRS = ((128, 1), (512, 4), (2048, 16))
ATTN_WIDTH = len(DIL_PAIRS) * LANES
N_MEM = 256
PAST_LEN = 8192
MEM_HEADS = 4
MEM_WIDTH = MEM_HEADS * HEAD_DIM
MIX_WIDTH = GMLP_WIDTH + ATTN_WIDTH + MEM_WIDTH
N_IN = 3 * GMLP_WIDTH + 4 * ATTN_WIDTH + 2 * MEM_WIDTH
ROPE_THETA = 500000.0
ROT_DIM = HEAD_DIM // 4
ROT_HALF = ROT_DIM // 2
EPS = 1e-6
NEG = -1e30
SCALE = HEAD_DIM ** -0.5

OFF_U = 0
OFF_V = OFF_U + GMLP_WIDTH
OFF_GA = OFF_V + GMLP_WIDTH
OFF_Q = OFF_GA + GMLP_WIDTH
OFF_K = OFF_Q + ATTN_WIDTH
OFF_VV = OFF_K + ATTN_WIDTH
OFF_GB = OFF_VV + ATTN_WIDTH
OFF_QC = OFF_GB + ATTN_WIDTH
OFF_GC = OFF_QC + MEM_WIDTH

ROW_TILE = 512
ATTN_TILE = 2048
VMEM_LIMIT = 56 * 1024 * 1024


def _dot(a, b):
    return jnp.dot(a, b, preferred_element_type=F32)


def _dot_nt(a, b):
    return lax.dot_general(a, b, (((1,), (1,)), ((), ())), preferred_element_type=F32)


def _silu(x):
    return x * (1.0 / (1.0 + jnp.exp(-x)))


def _gelu(x):
    return 0.5 * x * (1.0 + lax.erf(x * np.float32(np.sqrt(0.5))))


def _lane_iota(shape):
    return lax.broadcasted_iota(jnp.int32, shape, len(shape) - 1)


def _head_sum(ss):
    r = lax.broadcasted_iota(jnp.int32, (LANES, LANES), 0) >> 6
    c = lax.broadcasted_iota(jnp.int32, (LANES, LANES), 1) >> 6
    ones_blk = jnp.where(r == c, 1.0, 0.0).astype(BF16)
    hi = ss.astype(BF16)
    lo = (ss - hi.astype(F32)).astype(BF16)
    return _dot(hi, ones_blk) + _dot(lo, ones_blk)


def _head_rms(x, gain):
    ms = _head_sum(x * x) * (1.0 / HEAD_DIM)
    return x * lax.rsqrt(ms + EPS) * gain


def _rope(x, cos, sin):
    lane = _lane_iota(x.shape) & (HEAD_DIM - 1)
    partner = jnp.where(lane < ROT_HALF, pltpu.roll(x, LANES - ROT_HALF, 1), pltpu.roll(x, ROT_HALF, 1))
    return x * cos + partner * sin


def _softmax_parts(s, axis=-1):
    mx = jnp.max(s, axis=axis, keepdims=True)
    e = jnp.exp(s - mx)
    return mx, e, jnp.sum(e, axis=axis, keepdims=True)


def _mem_kv_kernel(mem_ref, norm_ref, w_ref, kn_ref, kv_ref, k_ref, v_ref):
    x = mem_ref[0]
    h = x * lax.rsqrt(jnp.mean(x * x, axis=-1, keepdims=True) + EPS) * norm_ref[...]
    kv = _dot(h.astype(BF16), w_ref[...])
    for c in range(MEM_WIDTH // LANES):
        sl = slice(c * LANES, (c + 1) * LANES)
        kc = _head_rms(kv[:, sl], kn_ref[...])
        kv_ref[0, :, sl] = kc
        k_ref[0, :, sl] = kc.astype(BF16)
    v = kv[:, MEM_WIDTH:]
    kv_ref[0, :, MEM_WIDTH:] = v
    v_ref[0] = v.astype(BF16)


def _mem_kv(mem, mem_norm, w_mem_kv, mem_k_norm):
    b = mem.shape[0]
    return pl.pallas_call(
        _mem_kv_kernel,
        grid=(b,),
        in_specs=[
            pl.BlockSpec((1, N_MEM, D_MODEL), lambda i: (i, 0, 0)),
            pl.BlockSpec((1, D_MODEL), lambda i: (0, 0)),
            pl.BlockSpec((D_MODEL, 2 * MEM_WIDTH), lambda i: (0, 0)),
            pl.BlockSpec((1, LANES), lambda i: (0, 0)),
        ],
        out_specs=[
            pl.BlockSpec((1, N_MEM, 2 * MEM_WIDTH), lambda i: (i, 0, 0)),
            pl.BlockSpec((1, N_MEM, MEM_WIDTH), lambda i: (i, 0, 0)),
            pl.BlockSpec((1, N_MEM, MEM_WIDTH), lambda i: (i, 0, 0)),
        ],
        out_shape=[
            jax.ShapeDtypeStruct((b, N_MEM, 2 * MEM_WIDTH), F32),
            jax.ShapeDtypeStruct((b, N_MEM, MEM_WIDTH), BF16),
            jax.ShapeDtypeStruct((b, N_MEM, MEM_WIDTH), BF16),
        ],
        name="mem_kv",
    )(mem, mem_norm.reshape(1, D_MODEL), w_mem_kv.astype(BF16),
      jnp.tile(mem_k_norm, LANES // HEAD_DIM).reshape(1, LANES))


def _project(x_ref, gain_ref, w_ref, h_scr, z_scr):
    x = x_ref[...]
    h = x * lax.rsqrt(jnp.mean(x * x, axis=-1, keepdims=True) + EPS) * gain_ref[...]
    h_scr[...] = h.astype(BF16)
    for lo, width in ((OFF_U, GMLP_WIDTH), (OFF_V, GMLP_WIDTH), (OFF_GA, GMLP_WIDTH),
                      (OFF_Q, ATTN_WIDTH), (OFF_K, ATTN_WIDTH), (OFF_VV, ATTN_WIDTH),
                      (OFF_GB, ATTN_WIDTH), (OFF_QC, MEM_WIDTH), (OFF_GC, MEM_WIDTH)):
        z_scr[:, lo:lo + width] = _dot(h_scr[...], w_ref[:, lo:lo + width])


def _gmlp_chunk(z_scr, rows, lng_ref, lnb_ref, ws_ref, bias_ref):
    gu = _gelu(z_scr[rows, OFF_U:OFF_U + GMLP_WIDTH])
    gv = _gelu(z_scr[rows, OFF_V:OFF_V + GMLP_WIDTH])
    mu = jnp.mean(gv, axis=-1, keepdims=True)
    dv = gv - mu
    var = jnp.mean(dv * dv, axis=-1, keepdims=True)
    vln = dv * lax.rsqrt(var + EPS) * lng_ref[...] + lnb_ref[...]
    lane = _lane_iota((CHUNK, LANES))
    pieces = []
    for p in range(GMLP_WIDTH // LANES):
        sl = slice(p * LANES, (p + 1) * LANES)
        vp = vln[:, sl]
        rhs = jnp.concatenate([jnp.where(lane < HEAD_DIM, vp, 0.0),
                               jnp.where(lane < HEAD_DIM, 0.0, vp)], axis=0).astype(BF16)
        s = _dot(ws_ref[p], rhs) + bias_ref[:, sl]
        ga = z_scr[rows, OFF_GA + p * LANES:OFF_GA + (p + 1) * LANES]
        pieces.append(gu[:, sl] * s * _silu(ga))
    return pieces, vln


def _qk_chunk(z_scr, rows, off, g, norm_ref, cos, sin):
    x = z_scr[rows, off + g * LANES:off + (g + 1) * LANES]
    return _rope(_head_rms(x, norm_ref[...]), cos, sin)


def _stack_heads(x, n_heads):
    head = _lane_iota(x.shape) >> 6
    return jnp.concatenate([jnp.where(head == h, x, 0.0) for h in range(n_heads)], axis=0)


def _unstack_heads(y, n_heads):
    r = y.shape[0] // n_heads
    head = _lane_iota((r, y.shape[1])) >> 6
    out = y[0:r]
    for h in range(1, n_heads):
        out = jnp.where(head == h, y[h * r:(h + 1) * r], out)
    return out


def _mem_attend(qc, mk, mv):
    s = _dot_nt(_stack_heads(qc, MEM_HEADS).astype(BF16), mk) * SCALE
    _, e, l = _softmax_parts(s)
    pv = _dot(e.astype(BF16), mv) * (1.0 / l)
    return _unstack_heads(pv, MEM_HEADS)


def _prompt_proj_kernel(x_ref, gain_ref, w_ref, lng_ref, lnb_ref, qn_ref, kn_ref, mqn_ref,
                        cos_ref, sin_ref, ws_ref, bias_ref, mk_ref, mv_ref,
                        mixa_ref, mixc_ref, sgb_ref,
                        q0_ref, k0_ref, v0_ref, q1_ref, k1_ref, v1_ref, q2_ref, k2_ref, v2_ref,
                        t0_ref, t1_ref, t2_ref,
                        h_scr, z_scr, perm_scr):
    _project(x_ref, gain_ref, w_ref, h_scr, z_scr)
    nat_refs = (q0_ref, k0_ref, v0_ref)

    def chunk(c, carry):
        r0 = pl.multiple_of(c * CHUNK, CHUNK)
        rows = pl.ds(r0, CHUNK)
        pieces, _ = _gmlp_chunk(z_scr, rows, lng_ref, lnb_ref, ws_ref, bias_ref)
        for p, a in enumerate(pieces):
            mixa_ref[rows, p * LANES:(p + 1) * LANES] = a.astype(BF16)
        cos = cos_ref[rows, :]
        sin = sin_ref[rows, :]
        for g in range(len(DIL_PAIRS)):
            q = _qk_chunk(z_scr, rows, OFF_Q, g, qn_ref, cos, sin)
            k = _qk_chunk(z_scr, rows, OFF_K, g, kn_ref, cos, sin)
            v = z_scr[rows, OFF_VV + g * LANES:OFF_VV + (g + 1) * LANES]
            if g == 0:
                for ref, val in zip(nat_refs, (q, k, v)):
                    ref[0, rows, :] = val.astype(BF16)
            else:
                for j, val in enumerate((q, k, v)):
                    perm_scr[3 * (g - 1) + j, rows, :] = val
            tail = (t0_ref, t1_ref, t2_ref)[g]
            if g == 0:
                @pl.when(c == ROW_TILE // CHUNK - 1)
                def _():
                    tail[0, :, 0:LANES] = k
                    tail[0, :, LANES:2 * LANES] = v
            else:
                tail[0, rows, 0:LANES] = k
                tail[0, rows, LANES:2 * LANES] = v
        sgb_ref[rows, :] = _silu(z_scr[rows, OFF_GB:OFF_GB + ATTN_WIDTH])
        qc = jnp.concatenate(
            [_head_rms(z_scr[rows, OFF_QC + j * LANES:OFF_QC + (j + 1) * LANES], mqn_ref[...])
             for j in range(MEM_WIDTH // LANES)], axis=1)
        cm = _mem_attend(qc, mk_ref[0], mv_ref[0])
        mixc_ref[rows, :] = (cm * _silu(z_scr[rows, OFF_GC:OFF_GC + MEM_WIDTH])).astype(BF16)
        return carry

    lax.fori_loop(0, ROW_TILE // CHUNK, chunk, 0)

    d1 = DIL_PAIRS[1][1]
    d2 = DIL_PAIRS[2][1]
    for j, ref in enumerate((q1_ref, k1_ref, v1_ref)):
        for r in range(d1):
            ref[0, 0, r] = perm_scr[j, pl.ds(r, ROW_TILE // d1, stride=d1), :].astype(BF16)
    for j, ref in enumerate((q2_ref, k2_ref, v2_ref)):
        for r in range(d2):
            ref[0, 0, r, 0] = perm_scr[3 + j, pl.ds(r, ROW_TILE // d2, stride=d2), :].astype(BF16)


def _sample_proj_kernel(x_ref, gain_ref, w_ref, lng_ref, lnb_ref, qn_ref, kn_ref, mqn_ref,
                        cos_ref, sin_ref, ws_ref, bias_ref,
                        mixa_ref, vln_ref, q_ref, k_ref, v_ref, qc_ref, sgb_ref, sgc_ref,
                        h_scr, z_scr):
    _project(x_ref, gain_ref, w_ref, h_scr, z_scr)

    def chunk(c, carry):
        r0 = pl.multiple_of(c * CHUNK, CHUNK)
        rows = pl.ds(r0, CHUNK)
        pieces, vln = _gmlp_chunk(z_scr, rows, lng_ref, lnb_ref, ws_ref, bias_ref)
        for p, a in enumerate(pieces):
            mixa_ref[rows, p * LANES:(p + 1) * LANES] = a.astype(BF16)
        vln_ref[rows, :] = vln
        cos = cos_ref[rows, :]
        sin = sin_ref[rows, :]
        for g in range(len(DIL_PAIRS)):
            sl = slice(g * LANES, (g + 1) * LANES)
            q_ref[rows, sl] = _qk_chunk(z_scr, rows, OFF_Q, g, qn_ref, cos, sin)
            k_ref[rows, sl] = _qk_chunk(z_scr, rows, OFF_K, g, kn_ref, cos, sin)
            v_ref[rows, sl] = z_scr[rows, OFF_VV + g * LANES:OFF_VV + (g + 1) * LANES]
        sgb_ref[rows, :] = _silu(z_scr[rows, OFF_GB:OFF_GB + ATTN_WIDTH])
        for j in range(MEM_WIDTH // LANES):
            sl = slice(j * LANES, (j + 1) * LANES)
            qc_ref[rows, sl] = _head_rms(z_scr[rows, OFF_QC + j * LANES:OFF_QC + (j + 1) * LANES],
                                         mqn_ref[...])
        sgc_ref[rows, :] = _silu(z_scr[rows, OFF_GC:OFF_GC + MEM_WIDTH])
        return carry

    lax.fori_loop(0, ROW_TILE // CHUNK, chunk, 0)


def _rope_tables(pos):
    inv = ROPE_THETA ** (-jnp.arange(ROT_HALF, dtype=F32) * 2.0 / ROT_DIM)
    ang = pos.astype(F32)[:, None] * inv[None, :]
    cos, sin = jnp.cos(ang), jnp.sin(ang)
    n = pos.shape[0]
    pad1 = jnp.ones((n, HEAD_DIM - ROT_DIM), F32)
    pad0 = jnp.zeros((n, HEAD_DIM - ROT_DIM), F32)
    cos_h = jnp.concatenate([cos, cos, pad1], axis=1)
    sin_h = jnp.concatenate([-sin, sin, pad0], axis=1)
    return jnp.tile(cos_h, (1, LANES // HEAD_DIM)), jnp.tile(sin_h, (1, LANES // HEAD_DIM))


def _common_proj_operands(norm_gain, w_in, ln_g, ln_b, q_norm, k_norm, mem_q_norm):
    tile2 = lambda g: jnp.tile(g, LANES // HEAD_DIM).reshape(1, LANES)
    return (norm_gain.reshape(1, D_MODEL), w_in.astype(BF16), ln_g.reshape(1, GMLP_WIDTH),
            ln_b.reshape(1, GMLP_WIDTH), tile2(q_norm), tile2(k_norm), tile2(mem_q_norm))


def _common_proj_specs(const):
    return [
        const((1, D_MODEL)), const((D_MODEL, N_IN)), const((1, GMLP_WIDTH)), const((1, GMLP_WIDTH)),
        const((1, LANES)), const((1, LANES)), const((1, LANES)),
    ]


def _pair_ws(ws):
    return jnp.concatenate([ws[0::2], ws[1::2]], axis=2).astype(BF16)


def _prompt_proj(x, common, cos, sin, ws_pair, bias, mem_k, mem_v):
    b, s, _ = x.shape
    n_tiles = s // ROW_TILE
    w0, w1, w2 = (w for w, _ in DIL_PAIRS)
    d1, d2 = DIL_PAIRS[1][1], DIL_PAIRS[2][1]
    per_attn = ATTN_TILE // ROW_TILE
    x2 = x.reshape(b * s, D_MODEL)
    const = lambda shape: pl.BlockSpec(shape, lambda i, j: (0,) * len(shape))
    rows = lambda width: pl.BlockSpec((ROW_TILE, width), lambda i, j: (i * n_tiles + j, 0))
    in_specs = [rows(D_MODEL)] + _common_proj_specs(const) + [
        pl.BlockSpec((ROW_TILE, LANES), lambda i, j: (j, 0)),
        pl.BlockSpec((ROW_TILE, LANES), lambda i, j: (j, 0)),
        const((GMLP_WIDTH // LANES, CHUNK, 2 * CHUNK)), const((CHUNK, GMLP_WIDTH)),
        pl.BlockSpec((1, N_MEM, MEM_WIDTH), lambda i, j: (i, 0, 0)),
        pl.BlockSpec((1, N_MEM, MEM_WIDTH), lambda i, j: (i, 0, 0)),
    ]
    nat = pl.BlockSpec((1, ROW_TILE, LANES), lambda i, j: (i, j, 0))
    g1 = pl.BlockSpec((1, 1, d1, ROW_TILE // d1, LANES), lambda i, j: (i, j, 0, 0, 0))
    g2 = pl.BlockSpec((1, 1, d2, 1, ROW_TILE // d2, LANES),
                      lambda i, j: (i, j // per_attn, 0, j % per_attn, 0, 0))
    t0 = pl.BlockSpec((1, w0, 2 * LANES), lambda i, j: (i, 0, 0))
    t1 = pl.BlockSpec((1, ROW_TILE, 2 * LANES), lambda i, j: (i, 0, 0))
    t2 = pl.BlockSpec((1, ROW_TILE, 2 * LANES),
                      lambda i, j: (i, jnp.maximum(j - (n_tiles - w2 // ROW_TILE), 0), 0))
    out_specs = [rows(GMLP_WIDTH), rows(MEM_WIDTH), rows(ATTN_WIDTH),
                 nat, nat, nat, g1, g1, g1, g2, g2, g2, t0, t1, t2]
    nat_s = jax.ShapeDtypeStruct((b, s, LANES), BF16)
    g1_s = jax.ShapeDtypeStruct((b, s // w1, d1, w1 // d1, LANES), BF16)
    g2_s = jax.ShapeDtypeStruct((b, s // w2, d2, per_attn, ROW_TILE // d2, LANES), BF16)
    out_shape = [
        jax.ShapeDtypeStruct((b * s, GMLP_WIDTH), BF16),
        jax.ShapeDtypeStruct((b * s, MEM_WIDTH), BF16),
        jax.ShapeDtypeStruct((b * s, ATTN_WIDTH), F32),
        nat_s, nat_s, nat_s, g1_s, g1_s, g1_s, g2_s, g2_s, g2_s,
        jax.ShapeDtypeStruct((b, w0, 2 * LANES), F32),
        jax.ShapeDtypeStruct((b, w1, 2 * LANES), F32),
        jax.ShapeDtypeStruct((b, w2, 2 * LANES), F32),
    ]
    return pl.pallas_call(
        _prompt_proj_kernel,
        grid=(b, n_tiles),
        in_specs=in_specs,
        out_specs=out_specs,
        out_shape=out_shape,
        scratch_shapes=[
            pltpu.VMEM((ROW_TILE, D_MODEL), BF16),
            pltpu.VMEM((ROW_TILE, N_IN), F32),
            pltpu.VMEM((6, ROW_TILE, LANES), F32),
        ],
        compiler_params=pltpu.CompilerParams(
            dimension_semantics=("arbitrary", "arbitrary"), vmem_limit_bytes=VMEM_LIMIT),
        name="prompt_proj",
    )(x2, *common, cos, sin, ws_pair, bias, mem_k, mem_v)


def _sample_proj(x, common, cos, sin, ws_pair, bias):
    n = x.shape[0]
    const = lambda shape: pl.BlockSpec(shape, lambda i: (0,) * len(shape))
    rows = lambda width: pl.BlockSpec((ROW_TILE, width), lambda i: (i, 0))
    in_specs = [rows(D_MODEL)] + _common_proj_specs(const) + [
        const((ROW_TILE, LANES)), const((ROW_TILE, LANES)),
        const((GMLP_WIDTH // LANES, CHUNK, 2 * CHUNK)), const((CHUNK, GMLP_WIDTH)),
    ]
    widths = (GMLP_WIDTH, GMLP_WIDTH, ATTN_WIDTH, ATTN_WIDTH, ATTN_WIDTH, MEM_WIDTH, ATTN_WIDTH, MEM_WIDTH)
    dtypes = (BF16,) + (F32,) * 7
    return pl.pallas_call(
        _sample_proj_kernel,
        grid=(n // ROW_TILE,),
        in_specs=in_specs,
        out_specs=[rows(w) for w in widths],
        out_shape=[jax.ShapeDtypeStruct((n, w), dt) for w, dt in zip(widths, dtypes)],
        scratch_shapes=[
            pltpu.VMEM((ROW_TILE, D_MODEL), BF16),
            pltpu.VMEM((ROW_TILE, N_IN), F32),
        ],
        compiler_params=pltpu.CompilerParams(
            dimension_semantics=("arbitrary",), vmem_limit_bytes=VMEM_LIMIT),
        name="sample_proj",
    )(x, *common, cos, sin, ws_pair, bias)


def _window_unit(q, kk, vv, valid):
    lane_lo = _lane_iota(q.shape) < HEAD_DIM
    zero = jnp.zeros_like(q)
    qs = jnp.concatenate([jnp.where(lane_lo, q, zero), jnp.where(lane_lo, zero, q)], axis=0)
    s = _dot_nt(qs, kk) * SCALE
    s = jnp.where(valid, s, NEG)
    mx, e, l = _softmax_parts(s)
    pv = _dot(e.astype(BF16), vv) * (1.0 / l)
    lse = mx + jnp.log(l)
    n = q.shape[0]
    o = jnp.where(lane_lo, pv[:n], pv[n:])
    lse_b = jnp.where(lane_lo, jnp.broadcast_to(lse[:n], q.shape), jnp.broadcast_to(lse[n:], q.shape))
    return o, lse_b


def _prompt_attn_kernel(q0_ref, k0_ref, v0_ref, k0p_ref, v0p_ref,
                        q1_ref, k1_ref, v1_ref, k1p_ref, v1p_ref,
                        q2_ref, k2_ref, v2_ref, k2p_ref, v2p_ref,
                        sgb_ref, mix_ref, o_scr, lse_scr):
    m = pl.program_id(1)
    blk = CHUNK
    qo = lax.broadcasted_iota(jnp.int32, (2 * blk, 2 * blk), 0) & (blk - 1)
    ko = lax.broadcasted_iota(jnp.int32, (2 * blk, 2 * blk), 1)
    band = (ko >= qo) & (ko <= qo + blk)

    def valid_for(first):
        return band & (ko >= blk * first.astype(jnp.int32))

    d1, d2 = DIL_PAIRS[1][1], DIL_PAIRS[2][1]
    w1 = DIL_PAIRS[1][0]
    for b0 in range(ATTN_TILE // blk):
        q = q0_ref[0, b0 * blk:(b0 + 1) * blk, :]
        if b0 == 0:
            kk = jnp.concatenate([k0p_ref[0], k0_ref[0, 0:blk, :]], axis=0)
            vv = jnp.concatenate([v0p_ref[0], v0_ref[0, 0:blk, :]], axis=0)
            valid = valid_for(m == 0)
        else:
            kk = k0_ref[0, (b0 - 1) * blk:(b0 + 1) * blk, :]
            vv = v0_ref[0, (b0 - 1) * blk:(b0 + 1) * blk, :]
            valid = band
        o, lse = _window_unit(q, kk, vv, valid)
        o_scr[0, b0 * blk:(b0 + 1) * blk, :] = o
        lse_scr[0, b0 * blk:(b0 + 1) * blk, :] = lse
    for nn in range(ATTN_TILE // w1):
        for r in range(d1):
            q = q1_ref[0, nn, r]
            if nn == 0:
                kk = jnp.concatenate([k1p_ref[0, 0, r], k1_ref[0, 0, r]], axis=0)
                vv = jnp.concatenate([v1p_ref[0, 0, r], v1_ref[0, 0, r]], axis=0)
                valid = valid_for(m == 0)
            else:
                kk = jnp.concatenate([k1_ref[0, nn - 1, r], k1_ref[0, nn, r]], axis=0)
                vv = jnp.concatenate([v1_ref[0, nn - 1, r], v1_ref[0, nn, r]], axis=0)
                valid = band
            o, lse = _window_unit(q, kk, vv, valid)
            dst = pl.ds(nn * w1 + r, blk, stride=d1)
            o_scr[1, dst, :] = o
            lse_scr[1, dst, :] = lse
    for r in range(d2):
        q = q2_ref[0, 0, r].reshape(blk, LANES)
        kk = jnp.concatenate([k2p_ref[0, 0, r].reshape(blk, LANES), k2_ref[0, 0, r].reshape(blk, LANES)], axis=0)
        vv = jnp.concatenate([v2p_ref[0, 0, r].reshape(blk, LANES), v2_ref[0, 0, r].reshape(blk, LANES)], axis=0)
        o, lse = _window_unit(q, kk, vv, valid_for(m == 0))
        dst = pl.ds(r, blk, stride=d2)
        o_scr[2, dst, :] = o
        lse_scr[2, dst, :] = lse

    def combine(c, carry):
        rows = pl.ds(pl.multiple_of(c * blk, blk), blk)
        ls = [lse_scr[g, rows, :] for g in range(3)]
        mx = jnp.maximum(jnp.maximum(ls[0], ls[1]), ls[2])
        es = [jnp.exp(x - mx) for x in ls]
        inv = 1.0 / (es[0] + es[1] + es[2])
        for g in range(3):
            sl = slice(g * LANES, (g + 1) * LANES)
            mix_ref[rows, sl] = (o_scr[g, rows, :] * (es[g] * inv) * sgb_ref[rows, sl]).astype(BF16)
        return carry

    lax.fori_loop(0, ATTN_TILE // blk, combine, 0)


def _prompt_attn(feats, sgb, b, s):
    q0, k0, v0, q1, k1, v1, q2, k2, v2 = feats
    w1, d1 = DIL_PAIRS[1]
    w2, d2 = DIL_PAIRS[2]
    n_tiles = s // ATTN_TILE
    per1 = ATTN_TILE // w1
    per_row = ATTN_TILE // ROW_TILE
    prev = lambda m, k: jnp.maximum(m * k - 1, 0)
    nat = pl.BlockSpec((1, ATTN_TILE, LANES), lambda i, m: (i, m, 0))
    nat_p = pl.BlockSpec((1, CHUNK, LANES), lambda i, m: (i, prev(m, ATTN_TILE // CHUNK), 0))
    g1 = pl.BlockSpec((1, per1, d1, w1 // d1, LANES), lambda i, m: (i, m, 0, 0, 0))
    g1_p = pl.BlockSpec((1, 1, d1, w1 // d1, LANES), lambda i, m: (i, prev(m, per1), 0, 0, 0))
    g2 = pl.BlockSpec((1, 1, d2, per_row, ROW_TILE // d2, LANES), lambda i, m: (i, m, 0, 0, 0, 0))
    g2_p = pl.BlockSpec((1, 1, d2, per_row, ROW_TILE // d2, LANES), lambda i, m: (i, prev(m, 1), 0, 0, 0, 0))
    rows = pl.BlockSpec((ATTN_TILE, ATTN_WIDTH), lambda i, m: (i * n_tiles + m, 0))
    return pl.pallas_call(
        _prompt_attn_kernel,
        grid=(b, n_tiles),
        in_specs=[nat, nat, nat, nat_p, nat_p, g1, g1, g1, g1_p, g1_p, g2, g2, g2, g2_p, g2_p, rows],
        out_specs=rows,
        out_shape=jax.ShapeDtypeStruct((b * s, ATTN_WIDTH), BF16),
        scratch_shapes=[
            pltpu.VMEM((3, ATTN_TILE, LANES), F32),
            pltpu.VMEM((3, ATTN_TILE, LANES), F32),
        ],
        compiler_params=pltpu.CompilerParams(
            dimension_semantics=("arbitrary", "arbitrary"), vmem_limit_bytes=VMEM_LIMIT),
        name="prompt_attn",
    )(q0, k0, v0, k0, v0, q1, k1, v1, k1, v1, q2, k2, v2, k2, v2, sgb)


def _sample_step_kernel(w0_ref, w1_ref, w2_ref, mem_ref, q_ref, k_ref, v_ref, qc_ref, sgb_ref, sgc_ref,
                        o0_ref, o1_ref, o2_ref, mix_ref, *, t_new, seqs):
    win_refs = (w0_ref, w1_ref, w2_ref)
    out_refs = (o0_ref, o1_ref, o2_ref)
    pad = CHUNK - t_new
    for bb in range(seqs):
        rows = slice(bb * t_new, (bb + 1) * t_new)
        outs, lses = [], []
        for g, (window, dil) in enumerate(DIL_PAIRS):
            wref, oref = win_refs[g], out_refs[g]
            length = wref.shape[1]
            sl = slice(g * LANES, (g + 1) * LANES)
            qg, kg, vg = q_ref[rows, sl], k_ref[rows, sl], v_ref[rows, sl]
            lane_lo = _lane_iota(qg.shape) < HEAD_DIM
            qs = jnp.concatenate([jnp.where(lane_lo, qg, 0.0), jnp.where(lane_lo, 0.0, qg)],
                                 axis=0).astype(BF16)
            k_old = wref[bb, :, 0:LANES].astype(BF16)
            v_old = wref[bb, :, LANES:2 * LANES].astype(BF16)
            zpad = jnp.zeros((pad, LANES), F32)
            k_new = jnp.concatenate([kg, zpad], axis=0).astype(BF16)
            v_new = jnp.concatenate([vg, zpad], axis=0).astype(BF16)
            s_old = _dot_nt(qs, k_old) * SCALE
            s_new = _dot_nt(qs, k_new) * SCALE
            tq = lax.broadcasted_iota(jnp.int32, s_old.shape, 0) & (t_new - 1)
            dist = length + tq - lax.broadcasted_iota(jnp.int32, s_old.shape, 1)
            ok_old = ((dist & (dil - 1)) == 0) & (dist <= window)
            tqn = lax.broadcasted_iota(jnp.int32, s_new.shape, 0) & (t_new - 1)
            dn = tqn - lax.broadcasted_iota(jnp.int32, s_new.shape, 1)
            ok_new = (dn >= 0) & ((dn & (dil - 1)) == 0) & (dn <= window)
            s_old = jnp.where(ok_old, s_old, NEG)
            s_new = jnp.where(ok_new, s_new, NEG)
            mx = jnp.maximum(jnp.max(s_old, axis=-1, keepdims=True), jnp.max(s_new, axis=-1, keepdims=True))
            e_old = jnp.exp(s_old - mx)
            e_new = jnp.exp(s_new - mx)
            l = jnp.sum(e_old, axis=-1, keepdims=True) + jnp.sum(e_new, axis=-1, keepdims=True)
            pv = (_dot(e_old.astype(BF16), v_old) + _dot(e_new.astype(BF16), v_new)) * (1.0 / l)
            lse = mx + jnp.log(l)
            outs.append(jnp.where(lane_lo, pv[:t_new], pv[t_new:]))
            lses.append(jnp.where(lane_lo, jnp.broadcast_to(lse[:t_new], qg.shape),
                                  jnp.broadcast_to(lse[t_new:], qg.shape)))
            oref[bb, 0:length - t_new, :] = wref[bb, t_new:length, :]
            oref[bb, length - t_new:length, 0:LANES] = kg
            oref[bb, length - t_new:length, LANES:2 * LANES] = vg
        mx = jnp.maximum(jnp.maximum(lses[0], lses[1]), lses[2])
        es = [jnp.exp(x - mx) for x in lses]
        inv = 1.0 / (es[0] + es[1] + es[2])
        for g in range(3):
            sl = slice(g * LANES, (g + 1) * LANES)
            mix_ref[rows, sl] = outs[g] * (es[g] * inv) * sgb_ref[rows, sl]
        mk = mem_ref[bb, :, 0:MEM_WIDTH].astype(BF16)
        mv = mem_ref[bb, :, MEM_WIDTH:2 * MEM_WIDTH].astype(BF16)
        cm = _mem_attend(qc_ref[rows, :], mk, mv)
        mix_ref[rows, ATTN_WIDTH:ATTN_WIDTH + MEM_WIDTH] = cm * sgc_ref[rows, :]


def _sample_step(wins, mem, q, k, v, qc, sgb, sgc, t_new, seqs=1):
    bd = mem.shape[0]
    win_spec = lambda w: pl.BlockSpec((seqs,) + w.shape[1:], lambda i: (i, 0, 0))
    rows = lambda width: pl.BlockSpec((seqs * t_new, width), lambda i: (i, 0))
    return pl.pallas_call(
        functools.partial(_sample_step_kernel, t_new=t_new, seqs=seqs),
        grid=(bd // seqs,),
        in_specs=[win_spec(w) for w in wins] + [win_spec(mem)] + [
            rows(ATTN_WIDTH), rows(ATTN_WIDTH), rows(ATTN_WIDTH), rows(MEM_WIDTH),
            rows(ATTN_WIDTH), rows(MEM_WIDTH)],
        out_specs=[win_spec(w) for w in wins] + [rows(ATTN_WIDTH + MEM_WIDTH)],
        out_shape=[jax.ShapeDtypeStruct(w.shape, F32) for w in wins]
        + [jax.ShapeDtypeStruct((bd * t_new, ATTN_WIDTH + MEM_WIDTH), F32)],
        compiler_params=pltpu.CompilerParams(
            dimension_semantics=("arbitrary",), vmem_limit_bytes=VMEM_LIMIT),
        name="sample_step",
    )(*wins, mem, q, k, v, qc, sgb, sgc)


def _out_proj_kernel(x_ref, *refs):
    *mix_refs, w_ref, y_ref = refs
    mix = jnp.concatenate([r[...].astype(BF16) for r in mix_refs], axis=1)
    y_ref[...] = x_ref[...] + _dot(mix, w_ref[...])


def _out_proj(x, mixes, w_out):
    n = x.shape[0]
    rows = lambda width: pl.BlockSpec((ROW_TILE, width), lambda i: (i, 0))
    return pl.pallas_call(
        _out_proj_kernel,
        grid=(n // ROW_TILE,),
        in_specs=[rows(D_MODEL)] + [rows(mx.shape[1]) for mx in mixes]
        + [pl.BlockSpec((MIX_WIDTH, D_MODEL), lambda i: (0, 0))],
        out_specs=rows(D_MODEL),
        out_shape=jax.ShapeDtypeStruct((n, D_MODEL), F32),
        compiler_params=pltpu.CompilerParams(
            dimension_semantics=("arbitrary",), vmem_limit_bytes=VMEM_LIMIT),
        name="out_proj",
    )(x, *mixes, w_out.astype(BF16))


def kernel(x_prompt, x_sample, state_win0_kv, state_win1_kv, state_win2_kv, cache_mem_kv, mem_prompt, norm_gain, w_in, gmlp_ln_gain, gmlp_ln_bias, gmlp_w_s, gmlp_b_s, attn_q_norm, attn_k_norm, mem_norm, w_mem_kv, mem_q_norm, mem_k_norm, w_out):
    depth = norm_gain.shape[0]
    assert depth == 1, "single-layer step only"
    bp, s, _ = x_prompt.shape
    bd, t, _ = x_sample.shape
    past_len = PAST_LEN
    assert s % ATTN_TILE == 0 and (bd * t) % ROW_TILE == 0 and CHUNK % t == 0 and t & (t - 1) == 0
    states = (state_win0_kv, state_win1_kv, state_win2_kv)
    for st, (w, d) in zip(states, DIL_PAIRS):
        assert st.shape[2] == w and w // d == CHUNK and d & (d - 1) == 0

    common = _common_proj_operands(norm_gain[0], w_in[0], gmlp_ln_gain[0], gmlp_ln_bias[0],
                                   attn_q_norm[0], attn_k_norm[0], mem_q_norm[0])
    tril = jnp.tril(jnp.ones((CHUNK, CHUNK), dtype=bool))
    ws = jnp.where(tril[None], gmlp_w_s[0], 0.0)
    b_s = gmlp_b_s[0]

    mem_kv_p, mem_k, mem_v = _mem_kv(mem_prompt, mem_norm[0], w_mem_kv[0], mem_k_norm[0])
    cos_p, sin_p = _rope_tables(jnp.arange(s, dtype=jnp.int32))
    bias_p = jnp.repeat(b_s.T, HEAD_DIM, axis=1)
    outs = _prompt_proj(x_prompt, common, cos_p, sin_p, _pair_ws(ws), bias_p, mem_k, mem_v)
    mixa_p, mixc_p, sgb_p = outs[0:3]
    tails = outs[12:15]
    mixb_p = _prompt_attn(outs[3:12], sgb_p, bp, s)
    y_prompt = _out_proj(x_prompt.reshape(bp * s, D_MODEL), (mixa_p, mixb_p, mixc_p), w_out[0])

    reps = CHUNK // t
    ws_s = jnp.stack([jnp.kron(jnp.eye(reps, dtype=F32), ws[g, :t, :t]) for g in range(N_GMLP_GROUPS)])
    bias_s = jnp.repeat(jnp.tile(b_s[:, :t], (1, reps)).T, HEAD_DIM, axis=1)
    cos_s, sin_s = _rope_tables(past_len + (jnp.arange(ROW_TILE, dtype=jnp.int32) % t))
    mixa_s, vln_s, q_s, k_s, v_s, qc_s, sgb_s, sgc_s = _sample_proj(
        x_sample.reshape(bd * t, D_MODEL), common, cos_s, sin_s, _pair_ws(ws_s), bias_s)
    wins = [st[0].reshape(bd, st.shape[2], 2 * LANES) for st in states]
    mem_s = cache_mem_kv[0].reshape(bd, N_MEM, 2 * MEM_WIDTH)
    nw0, nw1, nw2, mixbc_s = _sample_step(wins, mem_s, q_s, k_s, v_s, qc_s, sgb_s, sgc_s, t)
    y_sample = _out_proj(x_sample.reshape(bd * t, D_MODEL), (mixa_s, mixbc_s), w_out[0])

    heads = LANES // HEAD_DIM
    win_shape = lambda n, w: (1, n, w, 2, heads, HEAD_DIM)
    return (
        y_prompt.reshape(bp, s, D_MODEL),
        y_sample.reshape(bd, t, D_MODEL),
        tails[0].reshape(win_shape(bp, DIL_PAIRS[0][0])),
        tails[1].reshape(win_shape(bp, DIL_PAIRS[1][0])),
        tails[2].reshape(win_shape(bp, DIL_PAIRS[2][0])),
        mem_kv_p.reshape(1, bp, N_MEM, 2, MEM_HEADS, HEAD_DIM),
        nw0.reshape(win_shape(bd, DIL_PAIRS[0][0])),
        nw1.reshape(win_shape(bd, DIL_PAIRS[1][0])),
        nw2.reshape(win_shape(bd, DIL_PAIRS[2][0])),
        vln_s.reshape(1, bd, t, N_GMLP_GROUPS, HEAD_DIM),
    )
```

```python
import functools

import numpy as np
import jax
import jax.numpy as jnp
from jax import lax
from jax.experimental import pallas as pl
from jax.experimental.pallas import tpu as pltpu

F32 = jnp.float32
BF16 = jnp.bfloat16

D_MODEL = 1024
HEAD_DIM = 64
LANES = 128
N_GMLP_GROUPS = 6
GMLP_WIDTH = N_GMLP_GROUPS * HEAD_DIM
CHUNK = 128
DIL_PAIRS = ((128, 1), (512, 4), (2048, 16))
ATTN_WIDTH = len(DIL_PAIRS) * LANES
N_MEM = 256
PAST_LEN = 8192
MEM_HEADS = 4
MEM_WIDTH = MEM_HEADS * HEAD_DIM
MIX_WIDTH = GMLP_WIDTH + ATTN_WIDTH + MEM_WIDTH
N_IN = 3 * GMLP_WIDTH + 4 * ATTN_WIDTH + 2 * MEM_WIDTH
ROPE_THETA = 500000.0
ROT_DIM = HEAD_DIM // 4
ROT_HALF = ROT_DIM // 2
EPS = 1e-6
NEG = -1e30
SCALE = HEAD_DIM ** -0.5

OFF_U = 0
OFF_V = OFF_U + GMLP_WIDTH
OFF_GA = OFF_V + GMLP_WIDTH
OFF_Q = OFF_GA + GMLP_WIDTH
OFF_K = OFF_Q + ATTN_WIDTH
OFF_VV = OFF_K + ATTN_WIDTH
OFF_GB = OFF_VV + ATTN_WIDTH
OFF_QC = OFF_GB + ATTN_WIDTH
OFF_GC = OFF_QC + MEM_WIDTH

ROW_TILE = 512
ATTN_TILE = 2048
VMEM_LIMIT = 56 * 1024 * 1024


def _dot(a, b):
    return jnp.dot(a, b, preferred_element_type=F32)


def _dot_nt(a, b):
    return lax.dot_general(a, b, (((1,), (1,)), ((), ())), preferred_element_type=F32)


def _silu(x):
    return x * (1.0 / (1.0 + jnp.exp(-x)))


def _gelu(x):
    return 0.5 * x * (1.0 + lax.erf(x * np.float32(np.sqrt(0.5))))


def _lane_iota(shape):
    return lax.broadcasted_iota(jnp.int32, shape, len(shape) - 1)


def _head_sum(ss):
    r = lax.broadcasted_iota(jnp.int32, (LANES, LANES), 0) >> 6
    c = lax.broadcasted_iota(jnp.int32, (LANES, LANES), 1) >> 6
    ones_blk = jnp.where(r == c, 1.0, 0.0).astype(BF16)
    hi = ss.astype(BF16)
    lo = (ss - hi.astype(F32)).astype(BF16)
    return _dot(hi, ones_blk) + _dot(lo, ones_blk)


def _head_rms(x, gain):
    ms = _head_sum(x * x) * (1.0 / HEAD_DIM)
    return x * lax.rsqrt(ms + EPS) * gain


def _rope(x, cos, sin):
    lane = _lane_iota(x.shape) & (HEAD_DIM - 1)
    partner = jnp.where(lane < ROT_HALF, pltpu.roll(x, LANES - ROT_HALF, 1), pltpu.roll(x, ROT_HALF, 1))
    return x * cos + partner * sin


def _softmax_parts(s, axis=-1):
    mx = jnp.max(s, axis=axis, keepdims=True)
    e = jnp.exp(s - mx)
    return mx, e, jnp.sum(e, axis=axis, keepdims=True)


def _mem_kv_kernel(mem_ref, norm_ref, w_ref, kn_ref, kv_ref, k_ref, v_ref):
    x = mem_ref[0]
    h = x * lax.rsqrt(jnp.mean(x * x, axis=-1, keepdims=True) + EPS) * norm_ref[...]
    kv = _dot(h.astype(BF16), w_ref[...])
    for c in range(MEM_WIDTH // LANES):
        sl = slice(c * LANES, (c + 1) * LANES)
        kc = _head_rms(kv[:, sl], kn_ref[...])
        kv_ref[0, :, sl] = kc
        k_ref[0, :, sl] = kc.astype(BF16)
    v = kv[:, MEM_WIDTH:]
    kv_ref[0, :, MEM_WIDTH:] = v
    v_ref[0] = v.astype(BF16)


def _mem_kv(mem, mem_norm, w_mem_kv, mem_k_norm):
    b = mem.shape[0]
    return pl.pallas_call(
        _mem_kv_kernel,
        grid=(b,),
        in_specs=[
            pl.BlockSpec((1, N_MEM, D_MODEL), lambda i: (i, 0, 0)),
            pl.BlockSpec((1, D_MODEL), lambda i: (0, 0)),
            pl.BlockSpec((D_MODEL, 2 * MEM_WIDTH), lambda i: (0, 0)),
            pl.BlockSpec((1, LANES), lambda i: (0, 0)),
        ],
        out_specs=[
            pl.BlockSpec((1, N_MEM, 2 * MEM_WIDTH), lambda i: (i, 0, 0)),
            pl.BlockSpec((1, N_MEM, MEM_WIDTH), lambda i: (i, 0, 0)),
            pl.BlockSpec((1, N_MEM, MEM_WIDTH), lambda i: (i, 0, 0)),
        ],
        out_shape=[
            jax.ShapeDtypeStruct((b, N_MEM, 2 * MEM_WIDTH), F32),
            jax.ShapeDtypeStruct((b, N_MEM, MEM_WIDTH), BF16),
            jax.ShapeDtypeStruct((b, N_MEM, MEM_WIDTH), BF16),
        ],
        name="mem_kv",
    )(mem, mem_norm.reshape(1, D_MODEL), w_mem_kv.astype(BF16),
      jnp.tile(mem_k_norm, LANES // HEAD_DIM).reshape(1, LANES))


def _project(x_ref, gain_ref, w_ref, h_scr, z_scr):
    x = x_ref[...]
    h = x * lax.rsqrt(jnp.mean(x * x, axis=-1, keepdims=True) + EPS) * gain_ref[...]
    h_scr[...] = h.astype(BF16)
    for lo, width in ((OFF_U, GMLP_WIDTH), (OFF_V, GMLP_WIDTH), (OFF_GA, GMLP_WIDTH),
                      (OFF_Q, ATTN_WIDTH), (OFF_K, ATTN_WIDTH), (OFF_VV, ATTN_WIDTH),
                      (OFF_GB, ATTN_WIDTH), (OFF_QC, MEM_WIDTH), (OFF_GC, MEM_WIDTH)):
        z_scr[:, lo:lo + width] = _dot(h_scr[...], w_ref[:, lo:lo + width])


def _gmlp_chunk(z_scr, rows, lng_ref, lnb_ref, ws_ref, bias_ref):
    gu = _gelu(z_scr[rows, OFF_U:OFF_U + GMLP_WIDTH])
    gv = _gelu(z_scr[rows, OFF_V:OFF_V + GMLP_WIDTH])
    mu = jnp.mean(gv, axis=-1, keepdims=True)
    dv = gv - mu
    var = jnp.mean(dv * dv, axis=-1, keepdims=True)
    vln = dv * lax.rsqrt(var + EPS) * lng_ref[...] + lnb_ref[...]
    lane = _lane_iota((CHUNK, LANES))
    pieces = []
    for p in range(GMLP_WIDTH // LANES):
        sl = slice(p * LANES, (p + 1) * LANES)
        vp = vln[:, sl]
        rhs = jnp.concatenate([jnp.where(lane < HEAD_DIM, vp, 0.0),
                               jnp.where(lane < HEAD_DIM, 0.0, vp)], axis=0).astype(BF16)
        s = _dot(ws_ref[p], rhs) + bias_ref[:, sl]
        ga = z_scr[rows, OFF_GA + p * LANES:OFF_GA + (p + 1) * LANES]
        pieces.append(gu[:, sl] * s * _silu(ga))
    return pieces, vln


def _qk_chunk(z_scr, rows, off, g, norm_ref, cos, sin):
    x = z_scr[rows, off + g * LANES:off + (g + 1) * LANES]
    return _rope(_head_rms(x, norm_ref[...]), cos, sin)


def _stack_heads(x, n_heads):
    head = _lane_iota(x.shape) >> 6
    return jnp.concatenate([jnp.where(head == h, x, 0.0) for h in range(n_heads)], axis=0)


def _unstack_heads(y, n_heads):
    r = y.shape[0] // n_heads
    head = _lane_iota((r, y.shape[1])) >> 6
    out = y[0:r]
    for h in range(1, n_heads):
        out = jnp.where(head == h, y[h * r:(h + 1) * r], out)
    return out


def _mem_attend(qc, mk, mv):
    s = _dot_nt(_stack_heads(qc, MEM_HEADS).astype(BF16), mk) * SCALE
    _, e, l = _softmax_parts(s)
    pv = _dot(e.astype(BF16), mv) * (1.0 / l)
    return _unstack_heads(pv, MEM_HEADS)


def _prompt_proj_kernel(x_ref, gain_ref, w_ref, lng_ref, lnb_ref, qn_ref, kn_ref, mqn_ref,
                        cos_ref, sin_ref, ws_ref, bias_ref, mk_ref, mv_ref,
                        mixa_ref, mixc_ref, sgb_ref,
                        q0_ref, k0_ref, v0_ref, q1_ref, k1_ref, v1_ref, q2_ref, k2_ref, v2_ref,
                        t0_ref, t1_ref, t2_ref,
                        h_scr, z_scr, perm_scr):
    _project(x_ref, gain_ref, w_ref, h_scr, z_scr)
    nat_refs = (q0_ref, k0_ref, v0_ref)

    def chunk(c, carry):
        r0 = pl.multiple_of(c * CHUNK, CHUNK)
        rows = pl.ds(r0, CHUNK)
        pieces, _ = _gmlp_chunk(z_scr, rows, lng_ref, lnb_ref, ws_ref, bias_ref)
        for p, a in enumerate(pieces):
            mixa_ref[rows, p * LANES:(p + 1) * LANES] = a.astype(BF16)
        cos = cos_ref[rows, :]
        sin = sin_ref[rows, :]
        for g in range(len(DIL_PAIRS)):
            q = _qk_chunk(z_scr, rows, OFF_Q, g, qn_ref, cos, sin)
            k = _qk_chunk(z_scr, rows, OFF_K, g, kn_ref, cos, sin)
            v = z_scr[rows, OFF_VV + g * LANES:OFF_VV + (g + 1) * LANES]
            if g == 0:
                for ref, val in zip(nat_refs, (q, k, v)):
                    ref[0, rows, :] = val.astype(BF16)
            else:
                for j, val in enumerate((q, k, v)):
                    perm_scr[3 * (g - 1) + j, rows, :] = val
            tail = (t0_ref, t1_ref, t2_ref)[g]
            if g == 0:
                @pl.when(c == ROW_TILE // CHUNK - 1)
                def _():
                    tail[0, :, 0:LANES] = k
                    tail[0, :, LANES:2 * LANES] = v
            else:
                tail[0, rows, 0:LANES] = k
                tail[0, rows, LANES:2 * LANES] = v
        sgb_ref[rows, :] = _silu(z_scr[rows, OFF_GB:OFF_GB + ATTN_WIDTH])
        qc = jnp.concatenate(
            [_head_rms(z_scr[rows, OFF_QC + j * LANES:OFF_QC + (j + 1) * LANES], mqn_ref[...])
             for j in range(MEM_WIDTH // LANES)], axis=1)
        cm = _mem_attend(qc, mk_ref[0], mv_ref[0])
        mixc_ref[rows, :] = (cm * _silu(z_scr[rows, OFF_GC:OFF_GC + MEM_WIDTH])).astype(BF16)
        return carry

    lax.fori_loop(0, ROW_TILE // CHUNK, chunk, 0)

    d1 = DIL_PAIRS[1][1]
    d2 = DIL_PAIRS[2][1]
    for j, ref in enumerate((q1_ref, k1_ref, v1_ref)):
        for r in range(d1):
            ref[0, 0, r] = perm_scr[j, pl.ds(r, ROW_TILE // d1, stride=d1), :].astype(BF16)
    for j, ref in enumerate((q2_ref, k2_ref, v2_ref)):
        for r in range(d2):
            ref[0, 0, r, 0] = perm_scr[3 + j, pl.ds(r, ROW_TILE // d2, stride=d2), :].astype(BF16)


def _sample_proj_kernel(x_ref, gain_ref, w_ref, lng_ref, lnb_ref, qn_ref, kn_ref, mqn_ref,
                        cos_ref, sin_ref, ws_ref, bias_ref,
                        mixa_ref, vln_ref, q_ref, k_ref, v_ref, qc_ref, sgb_ref, sgc_ref,
                        h_scr, z_scr):
    _project(x_ref, gain_ref, w_ref, h_scr, z_scr)

    def chunk(c, carry):
        r0 = pl.multiple_of(c * CHUNK, CHUNK)
        rows = pl.ds(r0, CHUNK)
        pieces, vln = _gmlp_chunk(z_scr, rows, lng_ref, lnb_ref, ws_ref, bias_ref)
        for p, a in enumerate(pieces):
            mixa_ref[rows, p * LANES:(p + 1) * LANES] = a.astype(BF16)
        vln_ref[rows, :] = vln
        cos = cos_ref[rows, :]
        sin = sin_ref[rows, :]
        for g in range(len(DIL_PAIRS)):
            sl = slice(g * LANES, (g + 1) * LANES)
            q_ref[rows, sl] = _qk_chunk(z_scr, rows, OFF_Q, g, qn_ref, cos, sin)
            k_ref[rows, sl] = _qk_chunk(z_scr, rows, OFF_K, g, kn_ref, cos, sin)
            v_ref[rows, sl] = z_scr[rows, OFF_VV + g * LANES:OFF_VV + (g + 1) * LANES]
        sgb_ref[rows, :] = _silu(z_scr[rows, OFF_GB:OFF_GB + ATTN_WIDTH])
        for j in range(MEM_WIDTH // LANES):
            sl = slice(j * LANES, (j + 1) * LANES)
            qc_ref[rows, sl] = _head_rms(z_scr[rows, OFF_QC + j * LANES:OFF_QC + (j + 1) * LANES],
                                         mqn_ref[...])
        sgc_ref[rows, :] = _silu(z_scr[rows, OFF_GC:OFF_GC + MEM_WIDTH])
        return carry

    lax.fori_loop(0, ROW_TILE // CHUNK, chunk, 0)


def _rope_tables(pos):
    inv = ROPE_THETA ** (-jnp.arange(ROT_HALF, dtype=F32) * 2.0 / ROT_DIM)
    ang = pos.astype(F32)[:, None] * inv[None, :]
    cos, sin = jnp.cos(ang), jnp.sin(ang)
    n = pos.shape[0]
    pad1 = jnp.ones((n, HEAD_DIM - ROT_DIM), F32)
    pad0 = jnp.zeros((n, HEAD_DIM - ROT_DIM), F32)
    cos_h = jnp.concatenate([cos, cos, pad1], axis=1)
    sin_h = jnp.concatenate([-sin, sin, pad0], axis=1)
    return jnp.tile(cos_h, (1, LANES // HEAD_DIM)), jnp.tile(sin_h, (1, LANES // HEAD_DIM))


def _common_proj_operands(norm_gain, w_in, ln_g, ln_b, q_norm, k_norm, mem_q_norm):
    tile2 = lambda g: jnp.tile(g, LANES // HEAD_DIM).reshape(1, LANES)
    return (norm_gain.reshape(1, D_MODEL), w_in.astype(BF16), ln_g.reshape(1, GMLP_WIDTH),
            ln_b.reshape(1, GMLP_WIDTH), tile2(q_norm), tile2(k_norm), tile2(mem_q_norm))


def _common_proj_specs(const):
    return [
        const((1, D_MODEL)), const((D_MODEL, N_IN)), const((1, GMLP_WIDTH)), const((1, GMLP_WIDTH)),
        const((1, LANES)), const((1, LANES)), const((1, LANES)),
    ]


def _pair_ws(ws):
    return jnp.concatenate([ws[0::2], ws[1::2]], axis=2).astype(BF16)


def _prompt_proj(x, common, cos, sin, ws_pair, bias, mem_k, mem_v):
    b, s, _ = x.shape
    n_tiles = s // ROW_TILE
    w0, w1, w2 = (w for w, _ in DIL_PAIRS)
    d1, d2 = DIL_PAIRS[1][1], DIL_PAIRS[2][1]
    per_attn = ATTN_TILE // ROW_TILE
    x2 = x.reshape(b * s, D_MODEL)
    const = lambda shape: pl.BlockSpec(shape, lambda i, j: (0,) * len(shape))
    rows = lambda width: pl.BlockSpec((ROW_TILE, width), lambda i, j: (i * n_tiles + j, 0))
    in_specs = [rows(D_MODEL)] + _common_proj_specs(const) + [
        pl.BlockSpec((ROW_TILE, LANES), lambda i, j: (j, 0)),
        pl.BlockSpec((ROW_TILE, LANES), lambda i, j: (j, 0)),
        const((GMLP_WIDTH // LANES, CHUNK, 2 * CHUNK)), const((CHUNK, GMLP_WIDTH)),
        pl.BlockSpec((1, N_MEM, MEM_WIDTH), lambda i, j: (i, 0, 0)),
        pl.BlockSpec((1, N_MEM, MEM_WIDTH), lambda i, j: (i, 0, 0)),
    ]
    nat = pl.BlockSpec((1, ROW_TILE, LANES), lambda i, j: (i, j, 0))
    g1 = pl.BlockSpec((1, 1, d1, ROW_TILE // d1, LANES), lambda i, j: (i, j, 0, 0, 0))
    g2 = pl.BlockSpec((1, 1, d2, 1, ROW_TILE // d2, LANES),
                      lambda i, j: (i, j // per_attn, 0, j % per_attn, 0, 0))
    t0 = pl.BlockSpec((1, w0, 2 * LANES), lambda i, j: (i, 0, 0))
    t1 = pl.BlockSpec((1, ROW_TILE, 2 * LANES), lambda i, j: (i, 0, 0))
    t2 = pl.BlockSpec((1, ROW_TILE, 2 * LANES),
                      lambda i, j: (i, jnp.maximum(j - (n_tiles - w2 // ROW_TILE), 0), 0))
    out_specs = [rows(GMLP_WIDTH), rows(MEM_WIDTH), rows(ATTN_WIDTH),
                 nat, nat, nat, g1, g1, g1, g2, g2, g2, t0, t1, t2]
    nat_s = jax.ShapeDtypeStruct((b, s, LANES), BF16)
    g1_s = jax.ShapeDtypeStruct((b, s // w1, d1, w1 // d1, LANES), BF16)
    g2_s = jax.ShapeDtypeStruct((b, s // w2, d2, per_attn, ROW_TILE // d2, LANES), BF16)
    out_shape = [
        jax.ShapeDtypeStruct((b * s, GMLP_WIDTH), BF16),
        jax.ShapeDtypeStruct((b * s, MEM_WIDTH), BF16),
        jax.ShapeDtypeStruct((b * s, ATTN_WIDTH), F32),
        nat_s, nat_s, nat_s, g1_s, g1_s, g1_s, g2_s, g2_s, g2_s,
        jax.ShapeDtypeStruct((b, w0, 2 * LANES), F32),
        jax.ShapeDtypeStruct((b, w1, 2 * LANES), F32),
        jax.ShapeDtypeStruct((b, w2, 2 * LANES), F32),
    ]
    return pl.pallas_call(
        _prompt_proj_kernel,
        grid=(b, n_tiles),
        in_specs=in_specs,
        out_specs=out_specs,
        out_shape=out_shape,
        scratch_shapes=[
            pltpu.VMEM((ROW_TILE, D_MODEL), BF16),
            pltpu.VMEM((ROW_TILE, N_IN), F32),
            pltpu.VMEM((6, ROW_TILE, LANES), F32),
        ],
        compiler_params=pltpu.CompilerParams(
            dimension_semantics=("arbitrary", "arbitrary"), vmem_limit_bytes=VMEM_LIMIT),
        name="prompt_proj",
    )(x2, *common, cos, sin, ws_pair, bias, mem_k, mem_v)


def _sample_proj(x, common, cos, sin, ws_pair, bias):
    n = x.shape[0]
    const = lambda shape: pl.BlockSpec(shape, lambda i: (0,) * len(shape))
    rows = lambda width: pl.BlockSpec((ROW_TILE, width), lambda i: (i, 0))
    in_specs = [rows(D_MODEL)] + _common_proj_specs(const) + [
        const((ROW_TILE, LANES)), const((ROW_TILE, LANES)),
        const((GMLP_WIDTH // LANES, CHUNK, 2 * CHUNK)), const((CHUNK, GMLP_WIDTH)),
    ]
    widths = (GMLP_WIDTH, GMLP_WIDTH, ATTN_WIDTH, ATTN_WIDTH, ATTN_WIDTH, MEM_WIDTH, ATTN_WIDTH, MEM_WIDTH)
    dtypes = (BF16,) + (F32,) * 7
    return pl.pallas_call(
        _sample_proj_kernel,
        grid=(n // ROW_TILE,),
        in_specs=in_specs,
        out_specs=[rows(w) for w in widths],
        out_shape=[jax.ShapeDtypeStruct((n, w), dt) for w, dt in zip(widths, dtypes)],
        scratch_shapes=[
            pltpu.VMEM((ROW_TILE, D_MODEL), BF16),
            pltpu.VMEM((ROW_TILE, N_IN), F32),
        ],
        compiler_params=pltpu.CompilerParams(
            dimension_semantics=("arbitrary",), vmem_limit_bytes=VMEM_LIMIT),
        name="sample_proj",
    )(x, *common, cos, sin, ws_pair, bias)


def _window_unit(q, kk, vv, valid):
    lane_lo = _lane_iota(q.shape) < HEAD_DIM
    zero = jnp.zeros_like(q)
    qs = jnp.concatenate([jnp.where(lane_lo, q, zero), jnp.where(lane_lo, zero, q)], axis=0)
    s = _dot_nt(qs, kk) * SCALE
    s = jnp.where(valid, s, NEG)
    mx, e, l = _softmax_parts(s)
    pv = _dot(e.astype(BF16), vv) * (1.0 / l)
    lse = mx + jnp.log(l)
    n = q.shape[0]
    o = jnp.where(lane_lo, pv[:n], pv[n:])
    lse_b = jnp.where(lane_lo, jnp.broadcast_to(lse[:n], q.shape), jnp.broadcast_to(lse[n:], q.shape))
    return o, lse_b


def _prompt_attn_kernel(q0_ref, k0_ref, v0_ref, k0p_ref, v0p_ref,
                        q1_ref, k1_ref, v1_ref, k1p_ref, v1p_ref,
                        q2_ref, k2_ref, v2_ref, k2p_ref, v2p_ref,
                        sgb_ref, mix_ref, o_scr, lse_scr):
    m = pl.program_id(1)
    blk = CHUNK
    qo = lax.broadcasted_iota(jnp.int32, (2 * blk, 2 * blk), 0) & (blk - 1)
    ko = lax.broadcasted_iota(jnp.int32, (2 * blk, 2 * blk), 1)
    band = (ko >= qo) & (ko <= qo + blk)

    def valid_for(first):
        return band & (ko >= blk * first.astype(jnp.int32))

    d1, d2 = DIL_PAIRS[1][1], DIL_PAIRS[2][1]
    w1 = DIL_PAIRS[1][0]
    for b0 in range(ATTN_TILE // blk):
        q = q0_ref[0, b0 * blk:(b0 + 1) * blk, :]
        if b0 == 0:
            kk = jnp.concatenate([k0p_ref[0], k0_ref[0, 0:blk, :]], axis=0)
            vv = jnp.concatenate([v0p_ref[0], v0_ref[0, 0:blk, :]], axis=0)
            valid = valid_for(m == 0)
        else:
            kk = k0_ref[0, (b0 - 1) * blk:(b0 + 1) * blk, :]
            vv = v0_ref[0, (b0 - 1) * blk:(b0 + 1) * blk, :]
            valid = band
        o, lse = _window_unit(q, kk, vv, valid)
        o_scr[0, b0 * blk:(b0 + 1) * blk, :] = o
        lse_scr[0, b0 * blk:(b0 + 1) * blk, :] = lse
    for nn in range(ATTN_TILE // w1):
        for r in range(d1):
            q = q1_ref[0, nn, r]
            if nn == 0:
                kk = jnp.concatenate([k1p_ref[0, 0, r], k1_ref[0, 0, r]], axis=0)
                vv = jnp.concatenate([v1p_ref[0, 0, r], v1_ref[0, 0, r]], axis=0)
                valid = valid_for(m == 0)
            else:
                kk = jnp.concatenate([k1_ref[0, nn - 1, r], k1_ref[0, nn, r]], axis=0)
                vv = jnp.concatenate([v1_ref[0, nn - 1, r], v1_ref[0, nn, r]], axis=0)
                valid = band
            o, lse = _window_unit(q, kk, vv, valid)
            dst = pl.ds(nn * w1 + r, blk, stride=d1)
            o_scr[1, dst, :] = o
            lse_scr[1, dst, :] = lse
    for r in range(d2):
        q = q2_ref[0, 0, r].reshape(blk, LANES)
        kk = jnp.concatenate([k2p_ref[0, 0, r].reshape(blk, LANES), k2_ref[0, 0, r].reshape(blk, LANES)], axis=0)
        vv = jnp.concatenate([v2p_ref[0, 0, r].reshape(blk, LANES), v2_ref[0, 0, r].reshape(blk, LANES)], axis=0)
        o, lse = _window_unit(q, kk, vv, valid_for(m == 0))
        dst = pl.ds(r, blk, stride=d2)
        o_scr[2, dst, :] = o
        lse_scr[2, dst, :] = lse

    def combine(c, carry):
        rows = pl.ds(pl.multiple_of(c * blk, blk), blk)
        ls = [lse_scr[g, rows, :] for g in range(3)]
        mx = jnp.maximum(jnp.maximum(ls[0], ls[1]), ls[2])
        es = [jnp.exp(x - mx) for x in ls]
        inv = 1.0 / (es[0] + es[1] + es[2])
        for g in range(3):
            sl = slice(g * LANES, (g + 1) * LANES)
            mix_ref[rows, sl] = (o_scr[g, rows, :] * (es[g] * inv) * sgb_ref[rows, sl]).astype(BF16)
        return carry

    lax.fori_loop(0, ATTN_TILE // blk, combine, 0)


def _prompt_attn(feats, sgb, b, s):
    q0, k0, v0, q1, k1, v1, q2, k2, v2 = feats
    w1, d1 = DIL_PAIRS[1]
    w2, d2 = DIL_PAIRS[2]
    n_tiles = s // ATTN_TILE
    per1 = ATTN_TILE // w1
    per_row = ATTN_TILE // ROW_TILE
    prev = lambda m, k: jnp.maximum(m * k - 1, 0)
    nat = pl.BlockSpec((1, ATTN_TILE, LANES), lambda i, m: (i, m, 0))
    nat_p = pl.BlockSpec((1, CHUNK, LANES), lambda i, m: (i, prev(m, ATTN_TILE // CHUNK), 0))
    g1 = pl.BlockSpec((1, per1, d1, w1 // d1, LANES), lambda i, m: (i, m, 0, 0, 0))
    g1_p = pl.BlockSpec((1, 1, d1, w1 // d1, LANES), lambda i, m: (i, prev(m, per1), 0, 0, 0))
    g2 = pl.BlockSpec((1, 1, d2, per_row, ROW_TILE // d2, LANES), lambda i, m: (i, m, 0, 0, 0, 0))
    g2_p = pl.BlockSpec((1, 1, d2, per_row, ROW_TILE // d2, LANES), lambda i, m: (i, prev(m, 1), 0, 0, 0, 0))
    rows = pl.BlockSpec((ATTN_TILE, ATTN_WIDTH), lambda i, m: (i * n_tiles + m, 0))
    return pl.pallas_call(
        _prompt_attn_kernel,
        grid=(b, n_tiles),
        in_specs=[nat, nat, nat, nat_p, nat_p, g1, g1, g1, g1_p, g1_p, g2, g2, g2, g2_p, g2_p, rows],
        out_specs=rows,
        out_shape=jax.ShapeDtypeStruct((b * s, ATTN_WIDTH), BF16),
        scratch_shapes=[
            pltpu.VMEM((3, ATTN_TILE, LANES), F32),
            pltpu.VMEM((3, ATTN_TILE, LANES), F32),
        ],
        compiler_params=pltpu.CompilerParams(
            dimension_semantics=("arbitrary", "arbitrary"), vmem_limit_bytes=VMEM_LIMIT),
        name="prompt_attn",
    )(q0, k0, v0, k0, v0, q1, k1, v1, k1, v1, q2, k2, v2, k2, v2, sgb)


def _sample_step_kernel(w0_ref, w1_ref, w2_ref, mem_ref, q_ref, k_ref, v_ref, qc_ref, sgb_ref, sgc_ref,
                        o0_ref, o1_ref, o2_ref, mix_ref, *, t_new, seqs):
    win_refs = (w0_ref, w1_ref, w2_ref)
    out_refs = (o0_ref, o1_ref, o2_ref)
    pad = CHUNK - t_new
    for bb in range(seqs):
        rows = slice(bb * t_new, (bb + 1) * t_new)
        outs, lses = [], []
        for g, (window, dil) in enumerate(DIL_PAIRS):
            wref, oref = win_refs[g], out_refs[g]
            length = wref.shape[2]
            sl = slice(g * LANES, (g + 1) * LANES)
            qg, kg, vg = q_ref[rows, sl], k_ref[rows, sl], v_ref[rows, sl]
            lane_lo = _lane_iota(qg.shape) < HEAD_DIM
            qs = jnp.concatenate([jnp.where(lane_lo, qg, 0.0), jnp.where(lane_lo, 0.0, qg)],
                                 axis=0).astype(BF16)
            k_old_t = wref[bb, 0:LANES, :].astype(BF16)
            v_old_t = wref[bb, LANES:2 * LANES, :].astype(BF16)
            zpad = jnp.zeros((pad, LANES), F32)
            k_new = jnp.concatenate([kg, zpad], axis=0)
            v_new = jnp.concatenate([vg, zpad], axis=0)
            s_old = _dot(qs, k_old_t) * SCALE
            s_new = _dot_nt(qs, k_new.astype(BF16)) * SCALE
            tq = lax.broadcasted_iota(jnp.int32, s_old.shape, 0) & (t_new - 1)
            dist = length + tq - lax.broadcasted_iota(jnp.int32, s_old.shape, 1)
            ok_old = ((dist & (dil - 1)) == 0) & (dist <= window)
            tqn = lax.broadcasted_iota(jnp.int32, s_new.shape, 0) & (t_new - 1)
            dn = tqn - lax.broadcasted_iota(jnp.int32, s_new.shape, 1)
            ok_new = (dn >= 0) & ((dn & (dil - 1)) == 0) & (dn <= window)
            s_old = jnp.where(ok_old, s_old, NEG)
            s_new = jnp.where(ok_new, s_new, NEG)
            mx = jnp.maximum(jnp.max(s_old, axis=-1, keepdims=True), jnp.max(s_new, axis=-1, keepdims=True))
            e_old = jnp.exp(s_old - mx)
            e_new = jnp.exp(s_new - mx)
            l = jnp.sum(e_old, axis=-1, keepdims=True) + jnp.sum(e_new, axis=-1, keepdims=True)
            pv = (_dot_nt(e_old.astype(BF16), v_old_t)
                  + _dot(e_new.astype(BF16), v_new.astype(BF16))) * (1.0 / l)
            lse = mx + jnp.log(l)
            outs.append(jnp.where(lane_lo, pv[:t_new], pv[t_new:]))
            lses.append(jnp.where(lane_lo, jnp.broadcast_to(lse[:t_new], qg.shape),
                                  jnp.broadcast_to(lse[t_new:], qg.shape)))
            lane = _lane_iota((2 * LANES, LANES))
            keep = lane < LANES - t_new
            new_t = jnp.concatenate([k_new.T, v_new.T], axis=0)
            nxt = pltpu.roll(wref[bb, :, 0:LANES], LANES - t_new, 1)
            for c in range(length // LANES):
                cur = nxt
                if c + 1 < length // LANES:
                    nxt = pltpu.roll(wref[bb, :, (c + 1) * LANES:(c + 2) * LANES], LANES - t_new, 1)
                else:
                    nxt = pltpu.roll(new_t, LANES - t_new, 1)
                oref[bb, :, c * LANES:(c + 1) * LANES] = jnp.where(keep, cur, nxt)
        mx = jnp.maximum(jnp.maximum(lses[0], lses[1]), lses[2])
        es = [jnp.exp(x - mx) for x in lses]
        inv = 1.0 / (es[0] + es[1] + es[2])
        for g in range(3):
            sl = slice(g * LANES, (g + 1) * LANES)
            mix_ref[rows, sl] = outs[g] * (es[g] * inv) * sgb_ref[rows, sl]
        mk_t = mem_ref[bb, 0:MEM_WIDTH, :].astype(BF16)
        mv_t = mem_ref[bb, MEM_WIDTH:2 * MEM_WIDTH, :].astype(BF16)
        s = _dot(_stack_heads(qc_ref[rows, :], MEM_HEADS).astype(BF16), mk_t) * SCALE
        _, e, l = _softmax_parts(s)
        cm = _unstack_heads(_dot_nt(e.astype(BF16), mv_t) * (1.0 / l), MEM_HEADS)
        mix_ref[rows, ATTN_WIDTH:ATTN_WIDTH + MEM_WIDTH] = cm * sgc_ref[rows, :]


def _sample_step(wins, mem, q, k, v, qc, sgb, sgc, t_new, seqs=1):
    bd = mem.shape[0]
    win_spec = lambda w: pl.BlockSpec((seqs,) + w.shape[1:], lambda i: (i, 0, 0))
    rows = lambda width: pl.BlockSpec((seqs * t_new, width), lambda i: (i, 0))
    return pl.pallas_call(
        functools.partial(_sample_step_kernel, t_new=t_new, seqs=seqs),
        grid=(bd // seqs,),
        in_specs=[win_spec(w) for w in wins] + [win_spec(mem)] + [
            rows(ATTN_WIDTH), rows(ATTN_WIDTH), rows(ATTN_WIDTH), rows(MEM_WIDTH),
            rows(ATTN_WIDTH), rows(MEM_WIDTH)],
        out_specs=[win_spec(w) for w in wins] + [rows(ATTN_WIDTH + MEM_WIDTH)],
        out_shape=[jax.ShapeDtypeStruct(w.shape, F32) for w in wins]
        + [jax.ShapeDtypeStruct((bd * t_new, ATTN_WIDTH + MEM_WIDTH), F32)],
        compiler_params=pltpu.CompilerParams(
            dimension_semantics=("arbitrary",), vmem_limit_bytes=VMEM_LIMIT),
        name="sample_step",
    )(*wins, mem, q, k, v, qc, sgb, sgc)


def _out_proj_kernel(x_ref, *refs):
    *mix_refs, w_ref, y_ref = refs
    mix = jnp.concatenate([r[...].astype(BF16) for r in mix_refs], axis=1)
    y_ref[...] = x_ref[...] + _dot(mix, w_ref[...])


def _out_proj(x, mixes, w_out):
    n = x.shape[0]
    rows = lambda width: pl.BlockSpec((ROW_TILE, width), lambda i: (i, 0))
    return pl.pallas_call(
        _out_proj_kernel,
        grid=(n // ROW_TILE,),
        in_specs=[rows(D_MODEL)] + [rows(mx.shape[1]) for mx in mixes]
        + [pl.BlockSpec((MIX_WIDTH, D_MODEL), lambda i: (0, 0))],
        out_specs=rows(D_MODEL),
        out_shape=jax.ShapeDtypeStruct((n, D_MODEL), F32),
        compiler_params=pltpu.CompilerParams(
            dimension_semantics=("arbitrary",), vmem_limit_bytes=VMEM_LIMIT),
        name="out_proj",
    )(x, *mixes, w_out.astype(BF16))


def kernel(x_prompt, x_sample, state_win0_kv, state_win1_kv, state_win2_kv, cache_mem_kv, mem_prompt, norm_gain, w_in, gmlp_ln_gain, gmlp_ln_bias, gmlp_w_s, gmlp_b_s, attn_q_norm, attn_k_norm, mem_norm, w_mem_kv, mem_q_norm, mem_k_norm, w_out):
    depth = norm_gain.shape[0]
    assert depth == 1, "single-layer step only"
    bp, s, _ = x_prompt.shape
    bd, t, _ = x_sample.shape
    past_len = PAST_LEN
    assert s % ATTN_TILE == 0 and (bd * t) % ROW_TILE == 0 and CHUNK % t == 0 and t & (t - 1) == 0
    states = (state_win0_kv, state_win1_kv, state_win2_kv)
    for st, (w, d) in zip(states, DIL_PAIRS):
        assert st.shape[2] == w and w // d == CHUNK and d & (d - 1) == 0

    common = _common_proj_operands(norm_gain[0], w_in[0], gmlp_ln_gain[0], gmlp_ln_bias[0],
                                   attn_q_norm[0], attn_k_norm[0], mem_q_norm[0])
    tril = jnp.tril(jnp.ones((CHUNK, CHUNK), dtype=bool))
    ws = jnp.where(tril[None], gmlp_w_s[0], 0.0)
    b_s = gmlp_b_s[0]

    mem_kv_p, mem_k, mem_v = _mem_kv(mem_prompt, mem_norm[0], w_mem_kv[0], mem_k_norm[0])
    cos_p, sin_p = _rope_tables(jnp.arange(s, dtype=jnp.int32))
    bias_p = jnp.repeat(b_s.T, HEAD_DIM, axis=1)
    outs = _prompt_proj(x_prompt, common, cos_p, sin_p, _pair_ws(ws), bias_p, mem_k, mem_v)
    mixa_p, mixc_p, sgb_p = outs[0:3]
    tails = outs[12:15]
    mixb_p = _prompt_attn(outs[3:12], sgb_p, bp, s)
    y_prompt = _out_proj(x_prompt.reshape(bp * s, D_MODEL), (mixa_p, mixb_p, mixc_p), w_out[0])

    reps = CHUNK // t
    ws_s = jnp.stack([jnp.kron(jnp.eye(reps, dtype=F32), ws[g, :t, :t]) for g in range(N_GMLP_GROUPS)])
    bias_s = jnp.repeat(jnp.tile(b_s[:, :t], (1, reps)).T, HEAD_DIM, axis=1)
    cos_s, sin_s = _rope_tables(past_len + (jnp.arange(ROW_TILE, dtype=jnp.int32) % t))
    mixa_s, vln_s, q_s, k_s, v_s, qc_s, sgb_s, sgc_s = _sample_proj(
        x_sample.reshape(bd * t, D_MODEL), common, cos_s, sin_s, _pair_ws(ws_s), bias_s)
    feat_major = lambda a: jnp.transpose(a, (0, 2, 3, 4, 1)).reshape(a.shape[0], -1, a.shape[1])
    wins = [feat_major(st[0]) for st in states]
    mem_s = feat_major(cache_mem_kv[0])
    new_wins_t = _sample_step(wins, mem_s, q_s, k_s, v_s, qc_s, sgb_s, sgc_s, t)
    mixbc_s = new_wins_t[3]
    heads = LANES // HEAD_DIM
    nw0, nw1, nw2 = [
        jnp.transpose(w.reshape(bd, 2, heads, HEAD_DIM, w.shape[2]), (0, 4, 1, 2, 3)) for w in new_wins_t[:3]]
    y_sample = _out_proj(x_sample.reshape(bd * t, D_MODEL), (mixa_s, mixbc_s), w_out[0])

    win_shape = lambda n, w: (1, n, w, 2, heads, HEAD_DIM)
    return (
        y_prompt.reshape(bp, s, D_MODEL),
        y_sample.reshape(bd, t, D_MODEL),
        tails[0].reshape(win_shape(bp, DIL_PAIRS[0][0])),
        tails[1].reshape(win_shape(bp, DIL_PAIRS[1][0])),
        tails[2].reshape(win_shape(bp, DIL_PAIRS[2][0])),
        mem_kv_p.reshape(1, bp, N_MEM, 2, MEM_HEADS, HEAD_DIM),
        nw0.reshape(win_shape(bd, DIL_PAIRS[0][0])),
        nw1.reshape(win_shape(bd, DIL_PAIRS[1][0])),
        nw2.reshape(win_shape(bd, DIL_PAIRS[2][0])),
        vln_s.reshape(1, bd, t, N_GMLP_GROUPS, HEAD_DIM),
    )
```

```python
import functools

import numpy as np
import jax
import jax.numpy as jnp
from jax import lax
from jax.experimental import pallas as pl
from jax.experimental.pallas import tpu as pltpu

F32 = jnp.float32
BF16 = jnp.bfloat16

D_MODEL = 1024
HEAD_DIM = 64
LANES = 128
N_GMLP_GROUPS = 6
GMLP_WIDTH = N_GMLP_GROUPS * HEAD_DIM
CHUNK = 128
DIL_PAIRS = ((128, 1), (512, 4), (2048, 16))
ATTN_WIDTH = len(DIL_PAIRS) * LANES
N_MEM = 256
PAST_LEN = 8192
MEM_HEADS = 4
MEM_WIDTH = MEM_HEADS * HEAD_DIM
MIX_WIDTH = GMLP_WIDTH + ATTN_WIDTH + MEM_WIDTH
N_IN = 3 * GMLP_WIDTH + 4 * ATTN_WIDTH + 2 * MEM_WIDTH
ROPE_THETA = 500000.0
ROT_DIM = HEAD_DIM // 4
ROT_HALF = ROT_DIM // 2
EPS = 1e-6
NEG = -1e30
SCALE = HEAD_DIM ** -0.5

OFF_U = 0
OFF_V = OFF_U + GMLP_WIDTH
OFF_GA = OFF_V + GMLP_WIDTH
OFF_Q = OFF_GA + GMLP_WIDTH
OFF_K = OFF_Q + ATTN_WIDTH
OFF_VV = OFF_K + ATTN_WIDTH
OFF_GB = OFF_VV + ATTN_WIDTH
OFF_QC = OFF_GB + ATTN_WIDTH
OFF_GC = OFF_QC + MEM_WIDTH

ROW_TILE = 512
ATTN_TILE = 2048
SEQS_PER_STEP = 2
OUT_TILE = 1024
VMEM_LIMIT = 56 * 1024 * 1024


def _dot(a, b):
    return jnp.dot(a, b, preferred_element_type=F32)


def _dot_nt(a, b):
    return lax.dot_general(a, b, (((1,), (1,)), ((), ())), preferred_element_type=F32)


def _silu(x):
    return x * (1.0 / (1.0 + jnp.exp(-x)))


def _gelu(x):
    return 0.5 * x * (1.0 + lax.erf(x * np.float32(np.sqrt(0.5))))


def _lane_iota(shape):
    return lax.broadcasted_iota(jnp.int32, shape, len(shape) - 1)


def _head_sum(ss):
    r = (lax.broadcasted_iota(jnp.int32, (2 * LANES, LANES), 0) >> 6) & 1
    c = lax.broadcasted_iota(jnp.int32, (2 * LANES, LANES), 1) >> 6
    ones_blk = jnp.where(r == c, 1.0, 0.0).astype(BF16)
    hi = ss.astype(BF16)
    lo = (ss - hi.astype(F32)).astype(BF16)
    return _dot(jnp.concatenate([hi, lo], axis=1), ones_blk)


def _head_rms(x, gain):
    ms = _head_sum(x * x) * (1.0 / HEAD_DIM)
    return x * lax.rsqrt(ms + EPS) * gain


def _rope(x, cos, sin):
    lane = _lane_iota(x.shape) & (HEAD_DIM - 1)
    partner = jnp.where(lane < ROT_HALF, pltpu.roll(x, LANES - ROT_HALF, 1), pltpu.roll(x, ROT_HALF, 1))
    return x * cos + partner * sin


def _softmax_parts(s, axis=-1):
    mx = jnp.max(s, axis=axis, keepdims=True)
    e = jnp.exp(s - mx)
    return mx, e, jnp.sum(e, axis=axis, keepdims=True)


def _mem_kv_kernel(mem_ref, norm_ref, w_ref, kn_ref, kv_ref, k_ref, v_ref):
    x = mem_ref[0]
    h = x * lax.rsqrt(jnp.mean(x * x, axis=-1, keepdims=True) + EPS) * norm_ref[...]
    kv = _dot(h.astype(BF16), w_ref[...])
    for c in range(MEM_WIDTH // LANES):
        sl = slice(c * LANES, (c + 1) * LANES)
        kc = _head_rms(kv[:, sl], kn_ref[...])
        kv_ref[0, :, sl] = kc
        k_ref[0, :, sl] = kc.astype(BF16)
    v = kv[:, MEM_WIDTH:]
    kv_ref[0, :, MEM_WIDTH:] = v
    v_ref[0] = v.astype(BF16)


def _mem_kv(mem, mem_norm, w_mem_kv, mem_k_norm):
    b = mem.shape[0]
    return pl.pallas_call(
        _mem_kv_kernel,
        grid=(b,),
        in_specs=[
            pl.BlockSpec((1, N_MEM, D_MODEL), lambda i: (i, 0, 0)),
            pl.BlockSpec((1, D_MODEL), lambda i: (0, 0)),
            pl.BlockSpec((D_MODEL, 2 * MEM_WIDTH), lambda i: (0, 0)),
            pl.BlockSpec((1, LANES), lambda i: (0, 0)),
        ],
        out_specs=[
            pl.BlockSpec((1, N_MEM, 2 * MEM_WIDTH), lambda i: (i, 0, 0)),
            pl.BlockSpec((1, N_MEM, MEM_WIDTH), lambda i: (i, 0, 0)),
            pl.BlockSpec((1, N_MEM, MEM_WIDTH), lambda i: (i, 0, 0)),
        ],
        out_shape=[
            jax.ShapeDtypeStruct((b, N_MEM, 2 * MEM_WIDTH), F32),
            jax.ShapeDtypeStruct((b, N_MEM, MEM_WIDTH), BF16),
            jax.ShapeDtypeStruct((b, N_MEM, MEM_WIDTH), BF16),
        ],
        name="mem_kv",
    )(mem, mem_norm.reshape(1, D_MODEL), w_mem_kv.astype(BF16),
      jnp.tile(mem_k_norm, LANES // HEAD_DIM).reshape(1, LANES))


def _project(x_ref, gain_ref, w_ref, h_scr, z_scr):
    x = x_ref[...]
    h = x * lax.rsqrt(jnp.mean(x * x, axis=-1, keepdims=True) + EPS) * gain_ref[...]
    h_scr[...] = h.astype(BF16)
    for lo, width in ((OFF_U, GMLP_WIDTH), (OFF_V, GMLP_WIDTH), (OFF_GA, GMLP_WIDTH),
                      (OFF_Q, ATTN_WIDTH), (OFF_K, ATTN_WIDTH), (OFF_VV, ATTN_WIDTH),
                      (OFF_GB, ATTN_WIDTH), (OFF_QC, MEM_WIDTH), (OFF_GC, MEM_WIDTH)):
        z_scr[:, lo:lo + width] = _dot(h_scr[...], w_ref[:, lo:lo + width])


def _gmlp_chunk(z_scr, rows, lng_ref, lnb_ref, ws_ref, bias_ref):
    gu = _gelu(z_scr[rows, OFF_U:OFF_U + GMLP_WIDTH])
    gv = _gelu(z_scr[rows, OFF_V:OFF_V + GMLP_WIDTH])
    mu = jnp.mean(gv, axis=-1, keepdims=True)
    dv = gv - mu
    var = jnp.mean(dv * dv, axis=-1, keepdims=True)
    vln = dv * lax.rsqrt(var + EPS) * lng_ref[...] + lnb_ref[...]
    lane = _lane_iota((CHUNK, LANES))
    pieces = []
    for p in range(GMLP_WIDTH // LANES):
        sl = slice(p * LANES, (p + 1) * LANES)
        vp = vln[:, sl]
        rhs = jnp.concatenate([jnp.where(lane < HEAD_DIM, vp, 0.0),
                               jnp.where(lane < HEAD_DIM, 0.0, vp)], axis=0).astype(BF16)
        s = _dot(ws_ref[p], rhs) + bias_ref[:, sl]
        ga = z_scr[rows, OFF_GA + p * LANES:OFF_GA + (p + 1) * LANES]
        pieces.append(gu[:, sl] * s * _silu(ga))
    return pieces, vln


def _qk_chunk(z_scr, rows, off, g, norm_ref, cos, sin):
    x = z_scr[rows, off + g * LANES:off + (g + 1) * LANES]
    return _rope(_head_rms(x, norm_ref[...]), cos, sin)


def _stack_heads(x, n_heads):
    head = _lane_iota(x.shape) >> 6
    return jnp.concatenate([jnp.where(head == h, x, 0.0) for h in range(n_heads)], axis=0)


def _unstack_heads(y, n_heads):
    r = y.shape[0] // n_heads
    head = _lane_iota((r, y.shape[1])) >> 6
    out = y[0:r]
    for h in range(1, n_heads):
        out = jnp.where(head == h, y[h * r:(h + 1) * r], out)
    return out


def _mem_attend(qc, mk, mv):
    s = _dot_nt(_stack_heads(qc, MEM_HEADS).astype(BF16), mk) * SCALE
    _, e, l = _softmax_parts(s)
    pv = _dot(e.astype(BF16), mv) * (1.0 / l)
    return _unstack_heads(pv, MEM_HEADS)


def _prompt_proj_kernel(x_ref, gain_ref, w_ref, lng_ref, lnb_ref, qn_ref, kn_ref, mqn_ref,
                        cos_ref, sin_ref, ws_ref, bias_ref, mk_ref, mv_ref,
                        mixa_ref, mixc_ref, sgb_ref,
                        q0_ref, k0_ref, v0_ref, q1_ref, k1_ref, v1_ref, q2_ref, k2_ref, v2_ref,
                        t0_ref, t1_ref, t2_ref,
                        h_scr, z_scr, perm_scr):
    x = x_ref[...]
    h = x * lax.rsqrt(jnp.mean(x * x, axis=-1, keepdims=True) + EPS) * gain_ref[...]
    h_scr[...] = h.astype(BF16)

    def project(lo, width):
        z_scr[:, lo:lo + width] = _dot(h_scr[...], w_ref[:, lo:lo + width])

    chunks = [slice(c * CHUNK, (c + 1) * CHUNK) for c in range(ROW_TILE // CHUNK)]
    nat_refs = (q0_ref, k0_ref, v0_ref)

    for lo in (OFF_U, OFF_V, OFF_GA):
        project(lo, GMLP_WIDTH)
    for rows in chunks:
        pieces, _ = _gmlp_chunk(z_scr, rows, lng_ref, lnb_ref, ws_ref, bias_ref)
        for p, a in enumerate(pieces):
            mixa_ref[rows, p * LANES:(p + 1) * LANES] = a.astype(BF16)

    for lo in (OFF_Q, OFF_K, OFF_VV):
        project(lo, ATTN_WIDTH)
    for rows in chunks:
        cos = cos_ref[rows, :]
        sin = sin_ref[rows, :]
        for g in range(len(DIL_PAIRS)):
            q = _qk_chunk(z_scr, rows, OFF_Q, g, qn_ref, cos, sin)
            k = _qk_chunk(z_scr, rows, OFF_K, g, kn_ref, cos, sin)
            v = z_scr[rows, OFF_VV + g * LANES:OFF_VV + (g + 1) * LANES]
            if g == 0:
                for ref, val in zip(nat_refs, (q, k, v)):
                    ref[0, rows, :] = val.astype(BF16)
                if rows is chunks[-1]:
                    t0_ref[0, :, 0:LANES] = k
                    t0_ref[0, :, LANES:2 * LANES] = v
            else:
                for j, val in enumerate((q, k, v)):
                    perm_scr[3 * (g - 1) + j, rows, :] = val
                tail = (t1_ref, t2_ref)[g - 1]
                tail[0, rows, 0:LANES] = k
                tail[0, rows, LANES:2 * LANES] = v

    project(OFF_GB, ATTN_WIDTH)
    for rows in chunks:
        sgb_ref[rows, :] = _silu(z_scr[rows, OFF_GB:OFF_GB + ATTN_WIDTH])

    project(OFF_QC, MEM_WIDTH)
    project(OFF_GC, MEM_WIDTH)
    for rows in chunks:
        qc = jnp.concatenate(
            [_head_rms(z_scr[rows, OFF_QC + j * LANES:OFF_QC + (j + 1) * LANES], mqn_ref[...])
             for j in range(MEM_WIDTH // LANES)], axis=1)
        cm = _mem_attend(qc, mk_ref[0], mv_ref[0])
        mixc_ref[rows, :] = (cm * _silu(z_scr[rows, OFF_GC:OFF_GC + MEM_WIDTH])).astype(BF16)

    d1 = DIL_PAIRS[1][1]
    d2 = DIL_PAIRS[2][1]
    for j, ref in enumerate((q1_ref, k1_ref, v1_ref)):
        for r in range(d1):
            ref[0, 0, r] = perm_scr[j, pl.ds(r, ROW_TILE // d1, stride=d1), :].astype(BF16)
    for j, ref in enumerate((q2_ref, k2_ref, v2_ref)):
        for r in range(d2):
            ref[0, 0, r, 0] = perm_scr[3 + j, pl.ds(r, ROW_TILE // d2, stride=d2), :].astype(BF16)


def _sample_proj_kernel(x_ref, gain_ref, w_ref, lng_ref, lnb_ref, qn_ref, kn_ref, mqn_ref,
                        cos_ref, sin_ref, ws_ref, bias_ref,
                        mixa_ref, vln_ref, q_ref, k_ref, v_ref, qc_ref, sgb_ref, sgc_ref,
                        h_scr, z_scr):
    _project(x_ref, gain_ref, w_ref, h_scr, z_scr)

    def chunk(c, carry):
        r0 = pl.multiple_of(c * CHUNK, CHUNK)
        rows = pl.ds(r0, CHUNK)
        pieces, vln = _gmlp_chunk(z_scr, rows, lng_ref, lnb_ref, ws_ref, bias_ref)
        for p, a in enumerate(pieces):
            mixa_ref[rows, p * LANES:(p + 1) * LANES] = a.astype(BF16)
        vln_ref[rows, :] = vln
        cos = cos_ref[rows, :]
        sin = sin_ref[rows, :]
        for g in range(len(DIL_PAIRS)):
            sl = slice(g * LANES, (g + 1) * LANES)
            q_ref[rows, sl] = _qk_chunk(z_scr, rows, OFF_Q, g, qn_ref, cos, sin)
            k_ref[rows, sl] = _qk_chunk(z_scr, rows, OFF_K, g, kn_ref, cos, sin)
            v_ref[rows, sl] = z_scr[rows, OFF_VV + g * LANES:OFF_VV + (g + 1) * LANES]
        sgb_ref[rows, :] = _silu(z_scr[rows, OFF_GB:OFF_GB + ATTN_WIDTH])
        for j in range(MEM_WIDTH // LANES):
            sl = slice(j * LANES, (j + 1) * LANES)
            qc_ref[rows, sl] = _head_rms(z_scr[rows, OFF_QC + j * LANES:OFF_QC + (j + 1) * LANES],
                                         mqn_ref[...])
        sgc_ref[rows, :] = _silu(z_scr[rows, OFF_GC:OFF_GC + MEM_WIDTH])
        return carry

    lax.fori_loop(0, ROW_TILE // CHUNK, chunk, 0)


def _rope_tables(pos):
    inv = ROPE_THETA ** (-jnp.arange(ROT_HALF, dtype=F32) * 2.0 / ROT_DIM)
    ang = pos.astype(F32)[:, None] * inv[None, :]
    cos, sin = jnp.cos(ang), jnp.sin(ang)
    n = pos.shape[0]
    pad1 = jnp.ones((n, HEAD_DIM - ROT_DIM), F32)
    pad0 = jnp.zeros((n, HEAD_DIM - ROT_DIM), F32)
    cos_h = jnp.concatenate([cos, cos, pad1], axis=1)
    sin_h = jnp.concatenate([-sin, sin, pad0], axis=1)
    return jnp.tile(cos_h, (1, LANES // HEAD_DIM)), jnp.tile(sin_h, (1, LANES // HEAD_DIM))


def _common_proj_operands(norm_gain, w_in, ln_g, ln_b, q_norm, k_norm, mem_q_norm):
    tile2 = lambda g: jnp.tile(g, LANES // HEAD_DIM).reshape(1, LANES)
    return (norm_gain.reshape(1, D_MODEL), w_in.astype(BF16), ln_g.reshape(1, GMLP_WIDTH),
            ln_b.reshape(1, GMLP_WIDTH), tile2(q_norm), tile2(k_norm), tile2(mem_q_norm))


def _common_proj_specs(const):
    return [
        const((1, D_MODEL)), const((D_MODEL, N_IN)), const((1, GMLP_WIDTH)), const((1, GMLP_WIDTH)),
        const((1, LANES)), const((1, LANES)), const((1, LANES)),
    ]


def _pair_ws(ws):
    return jnp.concatenate([ws[0::2], ws[1::2]], axis=2).astype(BF16)


def _prompt_proj(x, common, cos, sin, ws_pair, bias, mem_k, mem_v):
    b, s, _ = x.shape
    n_tiles = s // ROW_TILE
    w0, w1, w2 = (w for w, _ in DIL_PAIRS)
    d1, d2 = DIL_PAIRS[1][1], DIL_PAIRS[2][1]
    per_attn = ATTN_TILE // ROW_TILE
    x2 = x.reshape(b * s, D_MODEL)
    const = lambda shape: pl.BlockSpec(shape, lambda i, j: (0,) * len(shape))
    rows = lambda width: pl.BlockSpec((ROW_TILE, width), lambda i, j: (i * n_tiles + j, 0))
    in_specs = [rows(D_MODEL)] + _common_proj_specs(const) + [
        pl.BlockSpec((ROW_TILE, LANES), lambda i, j: (j, 0)),
        pl.BlockSpec((ROW_TILE, LANES), lambda i, j: (j, 0)),
        const((GMLP_WIDTH // LANES, CHUNK, 2 * CHUNK)), const((CHUNK, GMLP_WIDTH)),
        pl.BlockSpec((1, N_MEM, MEM_WIDTH), lambda i, j: (i, 0, 0)),
        pl.BlockSpec((1, N_MEM, MEM_WIDTH), lambda i, j: (i, 0, 0)),
    ]
    nat = pl.BlockSpec((1, ROW_TILE, LANES), lambda i, j: (i, j, 0))
    g1 = pl.BlockSpec((1, 1, d1, ROW_TILE // d1, LANES), lambda i, j: (i, j, 0, 0, 0))
    g2 = pl.BlockSpec((1, 1, d2, 1, ROW_TILE // d2, LANES),
                      lambda i, j: (i, j // per_attn, 0, j % per_attn, 0, 0))
    t0 = pl.BlockSpec((1, w0, 2 * LANES), lambda i, j: (i, 0, 0))
    t1 = pl.BlockSpec((1, ROW_TILE, 2 * LANES), lambda i, j: (i, 0, 0))
    t2 = pl.BlockSpec((1, ROW_TILE, 2 * LANES),
                      lambda i, j: (i, jnp.maximum(j - (n_tiles - w2 // ROW_TILE), 0), 0))
    out_specs = [rows(GMLP_WIDTH), rows(MEM_WIDTH), rows(ATTN_WIDTH),
                 nat, nat, nat, g1, g1, g1, g2, g2, g2, t0, t1, t2]
    nat_s = jax.ShapeDtypeStruct((b, s, LANES), BF16)
    g1_s = jax.ShapeDtypeStruct((b, s // w1, d1, w1 // d1, LANES), BF16)
    g2_s = jax.ShapeDtypeStruct((b, s // w2, d2, per_attn, ROW_TILE // d2, LANES), BF16)
    out_shape = [
        jax.ShapeDtypeStruct((b * s, GMLP_WIDTH), BF16),
        jax.ShapeDtypeStruct((b * s, MEM_WIDTH), BF16),
        jax.ShapeDtypeStruct((b * s, ATTN_WIDTH), F32),
        nat_s, nat_s, nat_s, g1_s, g1_s, g1_s, g2_s, g2_s, g2_s,
        jax.ShapeDtypeStruct((b, w0, 2 * LANES), F32),
        jax.ShapeDtypeStruct((b, w1, 2 * LANES), F32),
        jax.ShapeDtypeStruct((b, w2, 2 * LANES), F32),
    ]
    return pl.pallas_call(
        _prompt_proj_kernel,
        grid=(b, n_tiles),
        in_specs=in_specs,
        out_specs=out_specs,
        out_shape=out_shape,
        scratch_shapes=[
            pltpu.VMEM((ROW_TILE, D_MODEL), BF16),
            pltpu.VMEM((ROW_TILE, N_IN), F32),
            pltpu.VMEM((6, ROW_TILE, LANES), F32),
        ],
        compiler_params=pltpu.CompilerParams(
            dimension_semantics=("arbitrary", "arbitrary"), vmem_limit_bytes=VMEM_LIMIT),
        name="prompt_proj",
    )(x2, *common, cos, sin, ws_pair, bias, mem_k, mem_v)


def _sample_proj(x, common, cos, sin, ws_pair, bias):
    n = x.shape[0]
    const = lambda shape: pl.BlockSpec(shape, lambda i: (0,) * len(shape))
    rows = lambda width: pl.BlockSpec((ROW_TILE, width), lambda i: (i, 0))
    in_specs = [rows(D_MODEL)] + _common_proj_specs(const) + [
        const((ROW_TILE, LANES)), const((ROW_TILE, LANES)),
        const((GMLP_WIDTH // LANES, CHUNK, 2 * CHUNK)), const((CHUNK, GMLP_WIDTH)),
    ]
    widths = (GMLP_WIDTH, GMLP_WIDTH, ATTN_WIDTH, ATTN_WIDTH, ATTN_WIDTH, MEM_WIDTH, ATTN_WIDTH, MEM_WIDTH)
    dtypes = (BF16,) + (F32,) * 7
    return pl.pallas_call(
        _sample_proj_kernel,
        grid=(n // ROW_TILE,),
        in_specs=in_specs,
        out_specs=[rows(w) for w in widths],
        out_shape=[jax.ShapeDtypeStruct((n, w), dt) for w, dt in zip(widths, dtypes)],
        scratch_shapes=[
            pltpu.VMEM((ROW_TILE, D_MODEL), BF16),
            pltpu.VMEM((ROW_TILE, N_IN), F32),
        ],
        compiler_params=pltpu.CompilerParams(
            dimension_semantics=("arbitrary",), vmem_limit_bytes=VMEM_LIMIT),
        name="sample_proj",
    )(x, *common, cos, sin, ws_pair, bias)


def _window_unit(q, kk, vv, valid):
    lane_lo = _lane_iota(q.shape) < HEAD_DIM
    zero = jnp.zeros_like(q)
    qs = jnp.concatenate([jnp.where(lane_lo, q, zero), jnp.where(lane_lo, zero, q)], axis=0)
    s = _dot_nt(qs, kk) * SCALE
    s = jnp.where(valid, s, NEG)
    mx, e, l = _softmax_parts(s)
    pv = _dot(e.astype(BF16), vv) * (1.0 / l)
    lse = mx + jnp.log(l)
    n = q.shape[0]
    o = jnp.where(lane_lo, pv[:n], pv[n:])
    lse_b = jnp.where(lane_lo, jnp.broadcast_to(lse[:n], q.shape), jnp.broadcast_to(lse[n:], q.shape))
    return o, lse_b


def _prompt_attn_kernel(q0_ref, k0_ref, v0_ref, k0p_ref, v0p_ref,
                        q1_ref, k1_ref, v1_ref, k1p_ref, v1p_ref,
                        q2_ref, k2_ref, v2_ref, k2p_ref, v2p_ref,
                        sgb_ref, mix_ref, o_scr, lse_scr):
    m = pl.program_id(1)
    blk = CHUNK
    qo = lax.broadcasted_iota(jnp.int32, (2 * blk, 2 * blk), 0) & (blk - 1)
    ko = lax.broadcasted_iota(jnp.int32, (2 * blk, 2 * blk), 1)
    band = (ko >= qo) & (ko <= qo + blk)

    def valid_for(first):
        return band & (ko >= blk * first.astype(jnp.int32))

    d1, d2 = DIL_PAIRS[1][1], DIL_PAIRS[2][1]
    w1 = DIL_PAIRS[1][0]
    for b0 in range(ATTN_TILE // blk):
        q = q0_ref[0, b0 * blk:(b0 + 1) * blk, :]
        if b0 == 0:
            kk = jnp.concatenate([k0p_ref[0], k0_ref[0, 0:blk, :]], axis=0)
            vv = jnp.concatenate([v0p_ref[0], v0_ref[0, 0:blk, :]], axis=0)
            valid = valid_for(m == 0)
        else:
            kk = k0_ref[0, (b0 - 1) * blk:(b0 + 1) * blk, :]
            vv = v0_ref[0, (b0 - 1) * blk:(b0 + 1) * blk, :]
            valid = band
        o, lse = _window_unit(q, kk, vv, valid)
        o_scr[0, b0 * blk:(b0 + 1) * blk, :] = o
        lse_scr[0, b0 * blk:(b0 + 1) * blk, :] = lse
    for nn in range(ATTN_TILE // w1):
        for r in range(d1):
            q = q1_ref[0, nn, r]
            if nn == 0:
                kk = jnp.concatenate([k1p_ref[0, 0, r], k1_ref[0, 0, r]], axis=0)
                vv = jnp.concatenate([v1p_ref[0, 0, r], v1_ref[0, 0, r]], axis=0)
                valid = valid_for(m == 0)
            else:
                kk = jnp.concatenate([k1_ref[0, nn - 1, r], k1_ref[0, nn, r]], axis=0)
                vv = jnp.concatenate([v1_ref[0, nn - 1, r], v1_ref[0, nn, r]], axis=0)
                valid = band
            o, lse = _window_unit(q, kk, vv, valid)
            dst = pl.ds(nn * w1 + r, blk, stride=d1)
            o_scr[1, dst, :] = o
            lse_scr[1, dst, :] = lse
    for r in range(d2):
        q = q2_ref[0, 0, r].reshape(blk, LANES)
        kk = jnp.concatenate([k2p_ref[0, 0, r].reshape(blk, LANES), k2_ref[0, 0, r].reshape(blk, LANES)], axis=0)
        vv = jnp.concatenate([v2p_ref[0, 0, r].reshape(blk, LANES), v2_ref[0, 0, r].reshape(blk, LANES)], axis=0)
        o, lse = _window_unit(q, kk, vv, valid_for(m == 0))
        dst = pl.ds(r, blk, stride=d2)
        o_scr[2, dst, :] = o
        lse_scr[2, dst, :] = lse

    def combine(c, carry):
        rows = pl.ds(pl.multiple_of(c * blk, blk), blk)
        ls = [lse_scr[g, rows, :] for g in range(3)]
        mx = jnp.maximum(jnp.maximum(ls[0], ls[1]), ls[2])
        es = [jnp.exp(x - mx) for x in ls]
        inv = 1.0 / (es[0] + es[1] + es[2])
        for g in range(3):
            sl = slice(g * LANES, (g + 1) * LANES)
            mix_ref[rows, sl] = (o_scr[g, rows, :] * (es[g] * inv) * sgb_ref[rows, sl]).astype(BF16)
        return carry

    lax.fori_loop(0, ATTN_TILE // blk, combine, 0)


def _prompt_attn(feats, sgb, b, s):
    q0, k0, v0, q1, k1, v1, q2, k2, v2 = feats
    w1, d1 = DIL_PAIRS[1]
    w2, d2 = DIL_PAIRS[2]
    n_tiles = s // ATTN_TILE
    per1 = ATTN_TILE // w1
    per_row = ATTN_TILE // ROW_TILE
    prev = lambda m, k: jnp.maximum(m * k - 1, 0)
    nat = pl.BlockSpec((1, ATTN_TILE, LANES), lambda i, m: (i, m, 0))
    nat_p = pl.BlockSpec((1, CHUNK, LANES), lambda i, m: (i, prev(m, ATTN_TILE // CHUNK), 0))
    g1 = pl.BlockSpec((1, per1, d1, w1 // d1, LANES), lambda i, m: (i, m, 0, 0, 0))
    g1_p = pl.BlockSpec((1, 1, d1, w1 // d1, LANES), lambda i, m: (i, prev(m, per1), 0, 0, 0))
    g2 = pl.BlockSpec((1, 1, d2, per_row, ROW_TILE // d2, LANES), lambda i, m: (i, m, 0, 0, 0, 0))
    g2_p = pl.BlockSpec((1, 1, d2, per_row, ROW_TILE // d2, LANES), lambda i, m: (i, prev(m, 1), 0, 0, 0, 0))
    rows = pl.BlockSpec((ATTN_TILE, ATTN_WIDTH), lambda i, m: (i * n_tiles + m, 0))
    return pl.pallas_call(
        _prompt_attn_kernel,
        grid=(b, n_tiles),
        in_specs=[nat, nat, nat, nat_p, nat_p, g1, g1, g1, g1_p, g1_p, g2, g2, g2, g2_p, g2_p, rows],
        out_specs=rows,
        out_shape=jax.ShapeDtypeStruct((b * s, ATTN_WIDTH), BF16),
        scratch_shapes=[
            pltpu.VMEM((3, ATTN_TILE, LANES), F32),
            pltpu.VMEM((3, ATTN_TILE, LANES), F32),
        ],
        compiler_params=pltpu.CompilerParams(
            dimension_semantics=("arbitrary", "arbitrary"), vmem_limit_bytes=VMEM_LIMIT),
        name="prompt_attn",
    )(q0, k0, v0, k0, v0, q1, k1, v1, k1, v1, q2, k2, v2, k2, v2, sgb)


def _sample_step_kernel(w0_ref, w1_ref, w2_ref, mem_ref, q_ref, k_ref, v_ref, qc_ref, sgb_ref, sgc_ref,
                        o0_ref, o1_ref, o2_ref, mix_ref, *, t_new, seqs):
    win_refs = (w0_ref, w1_ref, w2_ref)
    out_refs = (o0_ref, o1_ref, o2_ref)
    pad = CHUNK - t_new
    for bb in range(seqs):
        rows = slice(bb * t_new, (bb + 1) * t_new)
        outs, lses = [], []
        for g, (window, dil) in enumerate(DIL_PAIRS):
            wref, oref = win_refs[g], out_refs[g]
            length = wref.shape[2]
            sl = slice(g * LANES, (g + 1) * LANES)
            qg, kg, vg = q_ref[rows, sl], k_ref[rows, sl], v_ref[rows, sl]
            lane_lo = _lane_iota(qg.shape) < HEAD_DIM
            qs = jnp.concatenate([jnp.where(lane_lo, qg, 0.0), jnp.where(lane_lo, 0.0, qg)],
                                 axis=0).astype(BF16)
            k_old_t = wref[bb, 0:LANES, :].astype(BF16)
            v_old_t = wref[bb, LANES:2 * LANES, :].astype(BF16)
            zpad = jnp.zeros((pad, LANES), F32)
            k_new = jnp.concatenate([kg, zpad], axis=0)
            v_new = jnp.concatenate([vg, zpad], axis=0)
            s_old = _dot(qs, k_old_t) * SCALE
            s_new = _dot_nt(qs, k_new.astype(BF16)) * SCALE
            tq = lax.broadcasted_iota(jnp.int32, s_old.shape, 0) & (t_new - 1)
            dist = length + tq - lax.broadcasted_iota(jnp.int32, s_old.shape, 1)
            ok_old = ((dist & (dil - 1)) == 0) & (dist <= window)
            tqn = lax.broadcasted_iota(jnp.int32, s_new.shape, 0) & (t_new - 1)
            dn = tqn - lax.broadcasted_iota(jnp.int32, s_new.shape, 1)
            ok_new = (dn >= 0) & ((dn & (dil - 1)) == 0) & (dn <= window)
            s_old = jnp.where(ok_old, s_old, NEG)
            s_new = jnp.where(ok_new, s_new, NEG)
            mx = jnp.maximum(jnp.max(s_old, axis=-1, keepdims=True), jnp.max(s_new, axis=-1, keepdims=True))
            e_old = jnp.exp(s_old - mx)
            e_new = jnp.exp(s_new - mx)
            l = jnp.sum(e_old, axis=-1, keepdims=True) + jnp.sum(e_new, axis=-1, keepdims=True)
            pv = (_dot_nt(e_old.astype(BF16), v_old_t)
                  + _dot(e_new.astype(BF16), v_new.astype(BF16))) * (1.0 / l)
            lse = mx + jnp.log(l)
            outs.append(jnp.where(lane_lo, pv[:t_new], pv[t_new:]))
            lses.append(jnp.where(lane_lo, jnp.broadcast_to(lse[:t_new], qg.shape),
                                  jnp.broadcast_to(lse[t_new:], qg.shape)))
            lane = _lane_iota((2 * LANES, LANES))
            keep = lane < LANES - t_new
            new_t = jnp.concatenate([k_new.T, v_new.T], axis=0)
            nxt = pltpu.roll(wref[bb, :, 0:LANES], LANES - t_new, 1)
            for c in range(length // LANES):
                cur = nxt
                if c + 1 < length // LANES:
                    nxt = pltpu.roll(wref[bb, :, (c + 1) * LANES:(c + 2) * LANES], LANES - t_new, 1)
                else:
                    nxt = pltpu.roll(new_t, LANES - t_new, 1)
                oref[bb, :, c * LANES:(c + 1) * LANES] = jnp.where(keep, cur, nxt)
        mx = jnp.maximum(jnp.maximum(lses[0], lses[1]), lses[2])
        es = [jnp.exp(x - mx) for x in lses]
        inv = 1.0 / (es[0] + es[1] + es[2])
        for g in range(3):
            sl = slice(g * LANES, (g + 1) * LANES)
            mix_ref[rows, sl] = outs[g] * (es[g] * inv) * sgb_ref[rows, sl]
        mk_t = mem_ref[bb, 0:MEM_WIDTH, :].astype(BF16)
        mv_t = mem_ref[bb, MEM_WIDTH:2 * MEM_WIDTH, :].astype(BF16)
        s = _dot(_stack_heads(qc_ref[rows, :], MEM_HEADS).astype(BF16), mk_t) * SCALE
        _, e, l = _softmax_parts(s)
        cm = _unstack_heads(_dot_nt(e.astype(BF16), mv_t) * (1.0 / l), MEM_HEADS)
        mix_ref[rows, ATTN_WIDTH:ATTN_WIDTH + MEM_WIDTH] = cm * sgc_ref[rows, :]


def _sample_step(wins, mem, q, k, v, qc, sgb, sgc, t_new, seqs=SEQS_PER_STEP):
    bd = mem.shape[0]
    win_spec = lambda w: pl.BlockSpec((seqs,) + w.shape[1:], lambda i: (i, 0, 0))
    rows = lambda width: pl.BlockSpec((seqs * t_new, width), lambda i: (i, 0))
    return pl.pallas_call(
        functools.partial(_sample_step_kernel, t_new=t_new, seqs=seqs),
        grid=(bd // seqs,),
        in_specs=[win_spec(w) for w in wins] + [win_spec(mem)] + [
            rows(ATTN_WIDTH), rows(ATTN_WIDTH), rows(ATTN_WIDTH), rows(MEM_WIDTH),
            rows(ATTN_WIDTH), rows(MEM_WIDTH)],
        out_specs=[win_spec(w) for w in wins] + [rows(ATTN_WIDTH + MEM_WIDTH)],
        out_shape=[jax.ShapeDtypeStruct(w.shape, F32) for w in wins]
        + [jax.ShapeDtypeStruct((bd * t_new, ATTN_WIDTH + MEM_WIDTH), F32)],
        compiler_params=pltpu.CompilerParams(
            dimension_semantics=("arbitrary",), vmem_limit_bytes=VMEM_LIMIT),
        name="sample_step",
    )(*wins, mem, q, k, v, qc, sgb, sgc)


def _out_proj_kernel(x_ref, *refs):
    *mix_refs, w_ref, y_ref = refs
    mix = jnp.concatenate([r[...].astype(BF16) for r in mix_refs], axis=1)
    y_ref[...] = x_ref[...] + _dot(mix, w_ref[...])


def _out_proj(x, mixes, w_out):
    n = x.shape[0]
    rows = lambda width: pl.BlockSpec((OUT_TILE, width), lambda i: (i, 0))
    return pl.pallas_call(
        _out_proj_kernel,
        grid=(n // OUT_TILE,),
        in_specs=[rows(D_MODEL)] + [rows(mx.shape[1]) for mx in mixes]
        + [pl.BlockSpec((MIX_WIDTH, D_MODEL), lambda i: (0, 0))],
        out_specs=rows(D_MODEL),
        out_shape=jax.ShapeDtypeStruct((n, D_MODEL), F32),
        compiler_params=pltpu.CompilerParams(
            dimension_semantics=("arbitrary",), vmem_limit_bytes=VMEM_LIMIT),
        name="out_proj",
    )(x, *mixes, w_out.astype(BF16))


def kernel(x_prompt, x_sample, state_win0_kv, state_win1_kv, state_win2_kv, cache_mem_kv, mem_prompt, norm_gain, w_in, gmlp_ln_gain, gmlp_ln_bias, gmlp_w_s, gmlp_b_s, attn_q_norm, attn_k_norm, mem_norm, w_mem_kv, mem_q_norm, mem_k_norm, w_out):
    depth = norm_gain.shape[0]
    assert depth == 1, "single-layer step only"
    bp, s, _ = x_prompt.shape
    bd, t, _ = x_sample.shape
    past_len = PAST_LEN
    assert s % ATTN_TILE == 0 and (bd * t) % ROW_TILE == 0 and CHUNK % t == 0 and t & (t - 1) == 0
    states = (state_win0_kv, state_win1_kv, state_win2_kv)
    for st, (w, d) in zip(states, DIL_PAIRS):
        assert st.shape[2] == w and w // d == CHUNK and d & (d - 1) == 0

    common = _common_proj_operands(norm_gain[0], w_in[0], gmlp_ln_gain[0], gmlp_ln_bias[0],
                                   attn_q_norm[0], attn_k_norm[0], mem_q_norm[0])
    tril = jnp.tril(jnp.ones((CHUNK, CHUNK), dtype=bool))
    ws = jnp.where(tril[None], gmlp_w_s[0], 0.0)
    b_s = gmlp_b_s[0]

    mem_kv_p, mem_k, mem_v = _mem_kv(mem_prompt, mem_norm[0], w_mem_kv[0], mem_k_norm[0])
    cos_p, sin_p = _rope_tables(jnp.arange(s, dtype=jnp.int32))
    bias_p = jnp.repeat(b_s.T, HEAD_DIM, axis=1)
    outs = _prompt_proj(x_prompt, common, cos_p, sin_p, _pair_ws(ws), bias_p, mem_k, mem_v)
    mixa_p, mixc_p, sgb_p = outs[0:3]
    tails = outs[12:15]
    mixb_p = _prompt_attn(outs[3:12], sgb_p, bp, s)
    y_prompt = _out_proj(x_prompt.reshape(bp * s, D_MODEL), (mixa_p, mixb_p, mixc_p), w_out[0])

    reps = CHUNK // t
    ws_s = jnp.stack([jnp.kron(jnp.eye(reps, dtype=F32), ws[g, :t, :t]) for g in range(N_GMLP_GROUPS)])
    bias_s = jnp.repeat(jnp.tile(b_s[:, :t], (1, reps)).T, HEAD_DIM, axis=1)
    cos_s, sin_s = _rope_tables(past_len + (jnp.arange(ROW_TILE, dtype=jnp.int32) % t))
    mixa_s, vln_s, q_s, k_s, v_s, qc_s, sgb_s, sgc_s = _sample_proj(
        x_sample.reshape(bd * t, D_MODEL), common, cos_s, sin_s, _pair_ws(ws_s), bias_s)
    feat_major = lambda a: jnp.transpose(a, (0, 2, 3, 4, 1)).reshape(a.shape[0], -1, a.shape[1])
    wins = [feat_major(st[0]) for st in states]
    mem_s = feat_major(cache_mem_kv[0])
    new_wins_t = _sample_step(wins, mem_s, q_s, k_s, v_s, qc_s, sgb_s, sgc_s, t)
    mixbc_s = new_wins_t[3]
    heads = LANES // HEAD_DIM
    nw0, nw1, nw2 = [
        jnp.transpose(w.reshape(bd, 2, heads, HEAD_DIM, w.shape[2]), (0, 4, 1, 2, 3)) for w in new_wins_t[:3]]
    y_sample = _out_proj(x_sample.reshape(bd * t, D_MODEL), (mixa_s, mixbc_s), w_out[0])

    win_shape = lambda n, w: (1, n, w, 2, heads, HEAD_DIM)
    return (
        y_prompt.reshape(bp, s, D_MODEL),
        y_sample.reshape(bd, t, D_MODEL),
        tails[0].reshape(win_shape(bp, DIL_PAIRS[0][0])),
        tails[1].reshape(win_shape(bp, DIL_PAIRS[1][0])),
        tails[2].reshape(win_shape(bp, DIL_PAIRS[2][0])),
        mem_kv_p.reshape(1, bp, N_MEM, 2, MEM_HEADS, HEAD_DIM),
        nw0.reshape(win_shape(bd, DIL_PAIRS[0][0])),
        nw1.reshape(win_shape(bd, DIL_PAIRS[1][0])),
        nw2.reshape(win_shape(bd, DIL_PAIRS[2][0])),
        vln_s.reshape(1, bd, t, N_GMLP_GROUPS, HEAD_DIM),
    )
```

```python
import functools

import numpy as np
import jax
import jax.numpy as jnp
from jax import lax
from jax.experimental import pallas as pl
from jax.experimental.pallas import tpu as pltpu

F32 = jnp.float32
BF16 = jnp.bfloat16

D_MODEL = 1024
HEAD_DIM = 64
LANES = 128
N_GMLP_GROUPS = 6
GMLP_WIDTH = N_GMLP_GROUPS * HEAD_DIM
CHUNK = 128
DIL_PAIRS = ((128, 1), (512, 4), (2048, 16))
ATTN_WIDTH = len(DIL_PAIRS) * LANES
N_MEM = 256
PAST_LEN = 8192
MEM_HEADS = 4
MEM_WIDTH = MEM_HEADS * HEAD_DIM
MIX_WIDTH = GMLP_WIDTH + ATTN_WIDTH + MEM_WIDTH
N_IN = 3 * GMLP_WIDTH + 4 * ATTN_WIDTH + 2 * MEM_WIDTH
ROPE_THETA = 500000.0
ROT_DIM = HEAD_DIM // 4
ROT_HALF = ROT_DIM // 2
EPS = 1e-6
NEG = -1e30
SCALE = HEAD_DIM ** -0.5

OFF_U = 0
OFF_V = OFF_U + GMLP_WIDTH
OFF_GA = OFF_V + GMLP_WIDTH
OFF_Q = OFF_GA + GMLP_WIDTH
OFF_K = OFF_Q + ATTN_WIDTH
OFF_VV = OFF_K + ATTN_WIDTH
OFF_GB = OFF_VV + ATTN_WIDTH
OFF_QC = OFF_GB + ATTN_WIDTH
OFF_GC = OFF_QC + MEM_WIDTH

ROW_TILE = 256
ATTN_TILE = 2048
SEQS_PER_STEP = 2
OUT_TILE = 1024
VMEM_LIMIT = 56 * 1024 * 1024


def _dot(a, b):
    return jnp.dot(a, b, preferred_element_type=F32)


def _dot_nt(a, b):
    return lax.dot_general(a, b, (((1,), (1,)), ((), ())), preferred_element_type=F32)


def _silu(x):
    return x * (1.0 / (1.0 + jnp.exp(-x)))


def _gelu(x):
    return 0.5 * x * (1.0 + lax.erf(x * np.float32(np.sqrt(0.5))))


def _lane_iota(shape):
    return lax.broadcasted_iota(jnp.int32, shape, len(shape) - 1)


def _head_sum(ss):
    r = (lax.broadcasted_iota(jnp.int32, (2 * LANES, LANES), 0) >> 6) & 1
    c = lax.broadcasted_iota(jnp.int32, (2 * LANES, LANES), 1) >> 6
    ones_blk = jnp.where(r == c, 1.0, 0.0).astype(BF16)
    hi = ss.astype(BF16)
    lo = (ss - hi.astype(F32)).astype(BF16)
    return _dot(jnp.concatenate([hi, lo], axis=1), ones_blk)


def _head_rms(x, gain):
    ms = _head_sum(x * x) * (1.0 / HEAD_DIM)
    return x * lax.rsqrt(ms + EPS) * gain


def _rope(x, cos, sin):
    lane = _lane_iota(x.shape) & (HEAD_DIM - 1)
    partner = jnp.where(lane < ROT_HALF, pltpu.roll(x, LANES - ROT_HALF, 1), pltpu.roll(x, ROT_HALF, 1))
    return x * cos + partner * sin


def _softmax_parts(s, axis=-1):
    mx = jnp.max(s, axis=axis, keepdims=True)
    e = jnp.exp(s - mx)
    return mx, e, jnp.sum(e, axis=axis, keepdims=True)


def _mem_kv_kernel(mem_ref, norm_ref, w_ref, kn_ref, kv_ref, k_ref, v_ref):
    x = mem_ref[0]
    h = x * lax.rsqrt(jnp.mean(x * x, axis=-1, keepdims=True) + EPS) * norm_ref[...]
    kv = _dot(h.astype(BF16), w_ref[...])
    for c in range(MEM_WIDTH // LANES):
        sl = slice(c * LANES, (c + 1) * LANES)
        kc = _head_rms(kv[:, sl], kn_ref[...])
        kv_ref[0, :, sl] = kc
        k_ref[0, :, sl] = kc.astype(BF16)
    v = kv[:, MEM_WIDTH:]
    kv_ref[0, :, MEM_WIDTH:] = v
    v_ref[0] = v.astype(BF16)


def _mem_kv(mem, mem_norm, w_mem_kv, mem_k_norm):
    b = mem.shape[0]
    return pl.pallas_call(
        _mem_kv_kernel,
        grid=(b,),
        in_specs=[
            pl.BlockSpec((1, N_MEM, D_MODEL), lambda i: (i, 0, 0)),
            pl.BlockSpec((1, D_MODEL), lambda i: (0, 0)),
            pl.BlockSpec((D_MODEL, 2 * MEM_WIDTH), lambda i: (0, 0)),
            pl.BlockSpec((1, LANES), lambda i: (0, 0)),
        ],
        out_specs=[
            pl.BlockSpec((1, N_MEM, 2 * MEM_WIDTH), lambda i: (i, 0, 0)),
            pl.BlockSpec((1, N_MEM, MEM_WIDTH), lambda i: (i, 0, 0)),
            pl.BlockSpec((1, N_MEM, MEM_WIDTH), lambda i: (i, 0, 0)),
        ],
        out_shape=[
            jax.ShapeDtypeStruct((b, N_MEM, 2 * MEM_WIDTH), F32),
            jax.ShapeDtypeStruct((b, N_MEM, MEM_WIDTH), BF16),
            jax.ShapeDtypeStruct((b, N_MEM, MEM_WIDTH), BF16),
        ],
        name="mem_kv",
    )(mem, mem_norm.reshape(1, D_MODEL), w_mem_kv.astype(BF16),
      jnp.tile(mem_k_norm, LANES // HEAD_DIM).reshape(1, LANES))


def _project(x_ref, gain_ref, w_ref, h_scr, z_scr):
    x = x_ref[...]
    h = x * lax.rsqrt(jnp.mean(x * x, axis=-1, keepdims=True) + EPS) * gain_ref[...]
    h_scr[...] = h.astype(BF16)
    for lo, width in ((OFF_U, GMLP_WIDTH), (OFF_V, GMLP_WIDTH), (OFF_GA, GMLP_WIDTH),
                      (OFF_Q, ATTN_WIDTH), (OFF_K, ATTN_WIDTH), (OFF_VV, ATTN_WIDTH),
                      (OFF_GB, ATTN_WIDTH), (OFF_QC, MEM_WIDTH), (OFF_GC, MEM_WIDTH)):
        z_scr[:, lo:lo + width] = _dot(h_scr[...], w_ref[:, lo:lo + width])


def _gmlp_chunk(z_scr, rows, lng_ref, lnb_ref, ws_ref, bias_ref):
    gu = _gelu(z_scr[rows, OFF_U:OFF_U + GMLP_WIDTH])
    gv = _gelu(z_scr[rows, OFF_V:OFF_V + GMLP_WIDTH])
    mu = jnp.mean(gv, axis=-1, keepdims=True)
    dv = gv - mu
    var = jnp.mean(dv * dv, axis=-1, keepdims=True)
    vln = dv * lax.rsqrt(var + EPS) * lng_ref[...] + lnb_ref[...]
    lane = _lane_iota((CHUNK, LANES))
    pieces = []
    for p in range(GMLP_WIDTH // LANES):
        sl = slice(p * LANES, (p + 1) * LANES)
        vp = vln[:, sl]
        rhs = jnp.concatenate([jnp.where(lane < HEAD_DIM, vp, 0.0),
                               jnp.where(lane < HEAD_DIM, 0.0, vp)], axis=0).astype(BF16)
        s = _dot(ws_ref[p], rhs) + bias_ref[:, sl]
        ga = z_scr[rows, OFF_GA + p * LANES:OFF_GA + (p + 1) * LANES]
        pieces.append(gu[:, sl] * s * _silu(ga))
    return pieces, vln


def _qk_chunk(z_scr, rows, off, g, norm_ref, cos, sin):
    x = z_scr[rows, off + g * LANES:off + (g + 1) * LANES]
    return _rope(_head_rms(x, norm_ref[...]), cos, sin)


def _stack_heads(x, n_heads):
    head = _lane_iota(x.shape) >> 6
    return jnp.concatenate([jnp.where(head == h, x, 0.0) for h in range(n_heads)], axis=0)


def _unstack_heads(y, n_heads):
    r = y.shape[0] // n_heads
    head = _lane_iota((r, y.shape[1])) >> 6
    out = y[0:r]
    for h in range(1, n_heads):
        out = jnp.where(head == h, y[h * r:(h + 1) * r], out)
    return out


def _mem_attend(qc, mk, mv):
    s = _dot_nt(_stack_heads(qc, MEM_HEADS).astype(BF16), mk) * SCALE
    _, e, l = _softmax_parts(s)
    pv = _dot(e.astype(BF16), mv) * (1.0 / l)
    return _unstack_heads(pv, MEM_HEADS)


def _prompt_proj_kernel(x_ref, gain_ref, w_ref, lng_ref, lnb_ref, qn_ref, kn_ref, mqn_ref,
                        cos_ref, sin_ref, ws_ref, bias_ref, mk_ref, mv_ref,
                        w0_ref, w1_ref, w2_ref, mem_ref, sq_ref, sk_ref, sv_ref, sqc_ref, ssgb_ref, ssgc_ref,
                        mixa_ref, mixc_ref, sgb_ref,
                        q0_ref, k0_ref, v0_ref, q1_ref, k1_ref, v1_ref, q2_ref, k2_ref, v2_ref,
                        t0_ref, t1_ref, t2_ref,
                        o0_ref, o1_ref, o2_ref, smix_ref,
                        h_scr, z_scr, perm_scr, *, t_new):
    win_refs, out_refs = (w0_ref, w1_ref, w2_ref), (o0_ref, o1_ref, o2_ref)
    sample_parts = {}

    def sample_window(bb, g):
        sample_parts[bb, g] = _sample_window(win_refs[g], out_refs[g], sq_ref, sk_ref, sv_ref, bb, g, t_new)

    def sample_finish(bb):
        outs, lses = zip(*(sample_parts[bb, g] for g in range(len(DIL_PAIRS))))
        _sample_finish(outs, lses, mem_ref, sqc_ref, ssgb_ref, ssgc_ref, smix_ref, bb, t_new)

    sample_work = []
    for bb in range(SEQS_PER_STEP):
        sample_work += [functools.partial(sample_window, bb, g) for g in reversed(range(len(DIL_PAIRS)))]
        sample_work.append(functools.partial(sample_finish, bb))
    n_slots = 4

    def run_sample_slot(slot):
        per = -(-len(sample_work) // n_slots)
        for job in sample_work[slot * per:(slot + 1) * per]:
            job()

    x = x_ref[...]
    h = x * lax.rsqrt(jnp.mean(x * x, axis=-1, keepdims=True) + EPS) * gain_ref[...]
    h_scr[...] = h.astype(BF16)

    def project(lo, width):
        z_scr[:, lo:lo + width] = _dot(h_scr[...], w_ref[:, lo:lo + width])

    chunks = [slice(c * CHUNK, (c + 1) * CHUNK) for c in range(ROW_TILE // CHUNK)]
    nat_refs = (q0_ref, k0_ref, v0_ref)

    run_sample_slot(0)
    for lo in (OFF_U, OFF_V, OFF_GA):
        project(lo, GMLP_WIDTH)
    run_sample_slot(1)
    for rows in chunks:
        pieces, _ = _gmlp_chunk(z_scr, rows, lng_ref, lnb_ref, ws_ref, bias_ref)
        for p, a in enumerate(pieces):
            mixa_ref[rows, p * LANES:(p + 1) * LANES] = a.astype(BF16)

    for lo in (OFF_Q, OFF_K, OFF_VV):
        project(lo, ATTN_WIDTH)
    run_sample_slot(2)
    for rows in chunks:
        cos = cos_ref[rows, :]
        sin = sin_ref[rows, :]
        for g in range(len(DIL_PAIRS)):
            q = _qk_chunk(z_scr, rows, OFF_Q, g, qn_ref, cos, sin)
            k = _qk_chunk(z_scr, rows, OFF_K, g, kn_ref, cos, sin)
            v = z_scr[rows, OFF_VV + g * LANES:OFF_VV + (g + 1) * LANES]
            if g == 0:
                for ref, val in zip(nat_refs, (q, k, v)):
                    ref[0, rows, :] = val.astype(BF16)
                if rows is chunks[-1]:
                    t0_ref[0, :, 0:LANES] = k
                    t0_ref[0, :, LANES:2 * LANES] = v
            else:
                for j, val in enumerate((q, k, v)):
                    perm_scr[3 * (g - 1) + j, rows, :] = val
                tail = (t1_ref, t2_ref)[g - 1]
                tail[0, rows, 0:LANES] = k
                tail[0, rows, LANES:2 * LANES] = v

    project(OFF_GB, ATTN_WIDTH)
    for rows in chunks:
        sgb_ref[rows, :] = _silu(z_scr[rows, OFF_GB:OFF_GB + ATTN_WIDTH])

    project(OFF_QC, MEM_WIDTH)
    project(OFF_GC, MEM_WIDTH)
    run_sample_slot(3)
    for rows in chunks:
        qc = jnp.concatenate(
            [_head_rms(z_scr[rows, OFF_QC + j * LANES:OFF_QC + (j + 1) * LANES], mqn_ref[...])
             for j in range(MEM_WIDTH // LANES)], axis=1)
        cm = _mem_attend(qc, mk_ref[0], mv_ref[0])
        mixc_ref[rows, :] = (cm * _silu(z_scr[rows, OFF_GC:OFF_GC + MEM_WIDTH])).astype(BF16)

    d1 = DIL_PAIRS[1][1]
    d2 = DIL_PAIRS[2][1]
    for j, ref in enumerate((q1_ref, k1_ref, v1_ref)):
        for r in range(d1):
            ref[0, 0, r, 0] = perm_scr[j, pl.ds(r, ROW_TILE // d1, stride=d1), :].astype(BF16)
    for j, ref in enumerate((q2_ref, k2_ref, v2_ref)):
        for r in range(d2):
            ref[0, 0, r, 0] = perm_scr[3 + j, pl.ds(r, ROW_TILE // d2, stride=d2), :].astype(BF16)


def _sample_proj_kernel(x_ref, gain_ref, w_ref, lng_ref, lnb_ref, qn_ref, kn_ref, mqn_ref,
                        cos_ref, sin_ref, ws_ref, bias_ref,
                        mixa_ref, vln_ref, q_ref, k_ref, v_ref, qc_ref, sgb_ref, sgc_ref,
                        h_scr, z_scr):
    _project(x_ref, gain_ref, w_ref, h_scr, z_scr)

    def chunk(c, carry):
        r0 = pl.multiple_of(c * CHUNK, CHUNK)
        rows = pl.ds(r0, CHUNK)
        pieces, vln = _gmlp_chunk(z_scr, rows, lng_ref, lnb_ref, ws_ref, bias_ref)
        for p, a in enumerate(pieces):
            mixa_ref[rows, p * LANES:(p + 1) * LANES] = a.astype(BF16)
        vln_ref[rows, :] = vln
        cos = cos_ref[rows, :]
        sin = sin_ref[rows, :]
        for g in range(len(DIL_PAIRS)):
            sl = slice(g * LANES, (g + 1) * LANES)
            q_ref[rows, sl] = _qk_chunk(z_scr, rows, OFF_Q, g, qn_ref, cos, sin)
            k_ref[rows, sl] = _qk_chunk(z_scr, rows, OFF_K, g, kn_ref, cos, sin)
            v_ref[rows, sl] = z_scr[rows, OFF_VV + g * LANES:OFF_VV + (g + 1) * LANES]
        sgb_ref[rows, :] = _silu(z_scr[rows, OFF_GB:OFF_GB + ATTN_WIDTH])
        for j in range(MEM_WIDTH // LANES):
            sl = slice(j * LANES, (j + 1) * LANES)
            qc_ref[rows, sl] = _head_rms(z_scr[rows, OFF_QC + j * LANES:OFF_QC + (j + 1) * LANES],
                                         mqn_ref[...])
        sgc_ref[rows, :] = _silu(z_scr[rows, OFF_GC:OFF_GC + MEM_WIDTH])
        return carry

    lax.fori_loop(0, ROW_TILE // CHUNK, chunk, 0)


def _rope_tables(pos):
    inv = ROPE_THETA ** (-jnp.arange(ROT_HALF, dtype=F32) * 2.0 / ROT_DIM)
    ang = pos.astype(F32)[:, None] * inv[None, :]
    cos, sin = jnp.cos(ang), jnp.sin(ang)
    n = pos.shape[0]
    pad1 = jnp.ones((n, HEAD_DIM - ROT_DIM), F32)
    pad0 = jnp.zeros((n, HEAD_DIM - ROT_DIM), F32)
    cos_h = jnp.concatenate([cos, cos, pad1], axis=1)
    sin_h = jnp.concatenate([-sin, sin, pad0], axis=1)
    return jnp.tile(cos_h, (1, LANES // HEAD_DIM)), jnp.tile(sin_h, (1, LANES // HEAD_DIM))


def _common_proj_operands(norm_gain, w_in, ln_g, ln_b, q_norm, k_norm, mem_q_norm):
    tile2 = lambda g: jnp.tile(g, LANES // HEAD_DIM).reshape(1, LANES)
    return (norm_gain.reshape(1, D_MODEL), w_in.astype(BF16), ln_g.reshape(1, GMLP_WIDTH),
            ln_b.reshape(1, GMLP_WIDTH), tile2(q_norm), tile2(k_norm), tile2(mem_q_norm))


def _common_proj_specs(const):
    return [
        const((1, D_MODEL)), const((D_MODEL, N_IN)), const((1, GMLP_WIDTH)), const((1, GMLP_WIDTH)),
        const((1, LANES)), const((1, LANES)), const((1, LANES)),
    ]


def _pair_ws(ws):
    return jnp.concatenate([ws[0::2], ws[1::2]], axis=2).astype(BF16)


def _prompt_proj(x, common, cos, sin, ws_pair, bias, mem_k, mem_v, wins, mem_s, sample_feats, t_new):
    b, s, _ = x.shape
    n_tiles = s // ROW_TILE
    bd = mem_s.shape[0]
    assert b * n_tiles * SEQS_PER_STEP == bd
    w0, w1, w2 = (w for w, _ in DIL_PAIRS)
    d1, d2 = DIL_PAIRS[1][1], DIL_PAIRS[2][1]
    per1, per2 = w1 // ROW_TILE, w2 // ROW_TILE
    x2 = x.reshape(b * s, D_MODEL)
    const = lambda shape: pl.BlockSpec(shape, lambda i, j: (0,) * len(shape))
    rows = lambda width: pl.BlockSpec((ROW_TILE, width), lambda i, j: (i * n_tiles + j, 0))
    seq_blk = lambda a: pl.BlockSpec((SEQS_PER_STEP,) + a.shape[1:], lambda i, j: (i * n_tiles + j, 0, 0))
    seq_rows = lambda width: pl.BlockSpec((SEQS_PER_STEP * t_new, width), lambda i, j: (i * n_tiles + j, 0))
    in_specs = [rows(D_MODEL)] + _common_proj_specs(const) + [
        pl.BlockSpec((ROW_TILE, LANES), lambda i, j: (j, 0)),
        pl.BlockSpec((ROW_TILE, LANES), lambda i, j: (j, 0)),
        const((GMLP_WIDTH // LANES, CHUNK, 2 * CHUNK)), const((CHUNK, GMLP_WIDTH)),
        pl.BlockSpec((1, N_MEM, MEM_WIDTH), lambda i, j: (i, 0, 0)),
        pl.BlockSpec((1, N_MEM, MEM_WIDTH), lambda i, j: (i, 0, 0)),
    ] + [seq_blk(w) for w in wins] + [seq_blk(mem_s)] + [seq_rows(f.shape[1]) for f in sample_feats]
    nat = pl.BlockSpec((1, ROW_TILE, LANES), lambda i, j: (i, j, 0))
    g1 = pl.BlockSpec((1, 1, d1, 1, ROW_TILE // d1, LANES), lambda i, j: (i, j // per1, 0, j % per1, 0, 0))
    g2 = pl.BlockSpec((1, 1, d2, 1, ROW_TILE // d2, LANES), lambda i, j: (i, j // per2, 0, j % per2, 0, 0))
    last = lambda n_blk: (lambda i, j: (i, jnp.maximum(j - (n_tiles - n_blk), 0), 0))
    t0 = pl.BlockSpec((1, w0, 2 * LANES), last(1))
    t1 = pl.BlockSpec((1, ROW_TILE, 2 * LANES), last(per1))
    t2 = pl.BlockSpec((1, ROW_TILE, 2 * LANES), last(per2))
    out_specs = [rows(GMLP_WIDTH), rows(MEM_WIDTH), rows(ATTN_WIDTH),
                 nat, nat, nat, g1, g1, g1, g2, g2, g2, t0, t1, t2] \
        + [seq_blk(w) for w in wins] + [seq_rows(ATTN_WIDTH + MEM_WIDTH)]
    nat_s = jax.ShapeDtypeStruct((b, s, LANES), BF16)
    g1_s = jax.ShapeDtypeStruct((b, s // w1, d1, per1, ROW_TILE // d1, LANES), BF16)
    g2_s = jax.ShapeDtypeStruct((b, s // w2, d2, per2, ROW_TILE // d2, LANES), BF16)
    out_shape = [
        jax.ShapeDtypeStruct((b * s, GMLP_WIDTH), BF16),
        jax.ShapeDtypeStruct((b * s, MEM_WIDTH), BF16),
        jax.ShapeDtypeStruct((b * s, ATTN_WIDTH), F32),
        nat_s, nat_s, nat_s, g1_s, g1_s, g1_s, g2_s, g2_s, g2_s,
        jax.ShapeDtypeStruct((b, w0, 2 * LANES), F32),
        jax.ShapeDtypeStruct((b, w1, 2 * LANES), F32),
        jax.ShapeDtypeStruct((b, w2, 2 * LANES), F32),
    ] + [jax.ShapeDtypeStruct(w.shape, F32) for w in wins] \
      + [jax.ShapeDtypeStruct((bd * t_new, ATTN_WIDTH + MEM_WIDTH), F32)]
    return pl.pallas_call(
        functools.partial(_prompt_proj_kernel, t_new=t_new),
        grid=(b, n_tiles),
        in_specs=in_specs,
        out_specs=out_specs,
        out_shape=out_shape,
        scratch_shapes=[
            pltpu.VMEM((ROW_TILE, D_MODEL), BF16),
            pltpu.VMEM((ROW_TILE, N_IN), F32),
            pltpu.VMEM((6, ROW_TILE, LANES), F32),
        ],
        compiler_params=pltpu.CompilerParams(
            dimension_semantics=("arbitrary", "arbitrary"), vmem_limit_bytes=VMEM_LIMIT),
        name="prompt_proj",
    )(x2, *common, cos, sin, ws_pair, bias, mem_k, mem_v, *wins, mem_s, *sample_feats)


def _sample_proj(x, common, cos, sin, ws_pair, bias):
    n = x.shape[0]
    const = lambda shape: pl.BlockSpec(shape, lambda i: (0,) * len(shape))
    rows = lambda width: pl.BlockSpec((ROW_TILE, width), lambda i: (i, 0))
    in_specs = [rows(D_MODEL)] + _common_proj_specs(const) + [
        const((ROW_TILE, LANES)), const((ROW_TILE, LANES)),
        const((GMLP_WIDTH // LANES, CHUNK, 2 * CHUNK)), const((CHUNK, GMLP_WIDTH)),
    ]
    widths = (GMLP_WIDTH, GMLP_WIDTH, ATTN_WIDTH, ATTN_WIDTH, ATTN_WIDTH, MEM_WIDTH, ATTN_WIDTH, MEM_WIDTH)
    dtypes = (BF16,) + (F32,) * 7
    return pl.pallas_call(
        _sample_proj_kernel,
        grid=(n // ROW_TILE,),
        in_specs=in_specs,
        out_specs=[rows(w) for w in widths],
        out_shape=[jax.ShapeDtypeStruct((n, w), dt) for w, dt in zip(widths, dtypes)],
        scratch_shapes=[
            pltpu.VMEM((ROW_TILE, D_MODEL), BF16),
            pltpu.VMEM((ROW_TILE, N_IN), F32),
        ],
        compiler_params=pltpu.CompilerParams(
            dimension_semantics=("arbitrary",), vmem_limit_bytes=VMEM_LIMIT),
        name="sample_proj",
    )(x, *common, cos, sin, ws_pair, bias)


def _window_unit(q, kk, vv, valid):
    lane_lo = _lane_iota(q.shape) < HEAD_DIM
    zero = jnp.zeros_like(q)
    qs = jnp.concatenate([jnp.where(lane_lo, q, zero), jnp.where(lane_lo, zero, q)], axis=0)
    s = _dot_nt(qs, kk) * SCALE
    s = jnp.where(valid, s, NEG)
    mx, e, l = _softmax_parts(s)
    pv = _dot(e.astype(BF16), vv) * (1.0 / l)
    lse = mx + jnp.log(l)
    n = q.shape[0]
    o = jnp.where(lane_lo, pv[:n], pv[n:])
    lse_b = jnp.where(lane_lo, jnp.broadcast_to(lse[:n], q.shape), jnp.broadcast_to(lse[n:], q.shape))
    return o, lse_b


def _prompt_attn_kernel(q0_ref, k0_ref, v0_ref, k0p_ref, v0p_ref,
                        q1_ref, k1_ref, v1_ref, k1p_ref, v1p_ref,
                        q2_ref, k2_ref, v2_ref, k2p_ref, v2p_ref,
                        sgb_ref, mix_ref, o_scr, lse_scr):
    m = pl.program_id(1)
    blk = CHUNK
    qo = lax.broadcasted_iota(jnp.int32, (2 * blk, 2 * blk), 0) & (blk - 1)
    ko = lax.broadcasted_iota(jnp.int32, (2 * blk, 2 * blk), 1)
    band = (ko >= qo) & (ko <= qo + blk)

    def valid_for(first):
        return band & (ko >= blk * first.astype(jnp.int32))

    d1, d2 = DIL_PAIRS[1][1], DIL_PAIRS[2][1]
    w1 = DIL_PAIRS[1][0]
    for b0 in range(ATTN_TILE // blk):
        q = q0_ref[0, b0 * blk:(b0 + 1) * blk, :]
        if b0 == 0:
            kk = jnp.concatenate([k0p_ref[0], k0_ref[0, 0:blk, :]], axis=0)
            vv = jnp.concatenate([v0p_ref[0], v0_ref[0, 0:blk, :]], axis=0)
            valid = valid_for(m == 0)
        else:
            kk = k0_ref[0, (b0 - 1) * blk:(b0 + 1) * blk, :]
            vv = v0_ref[0, (b0 - 1) * blk:(b0 + 1) * blk, :]
            valid = band
        o, lse = _window_unit(q, kk, vv, valid)
        o_scr[0, b0 * blk:(b0 + 1) * blk, :] = o
        lse_scr[0, b0 * blk:(b0 + 1) * blk, :] = lse
    for nn in range(ATTN_TILE // w1):
        for r in range(d1):
            blk_of = lambda ref, n: ref[0, n, r].reshape(blk, LANES)
            q = blk_of(q1_ref, nn)
            if nn == 0:
                kk = jnp.concatenate([blk_of(k1p_ref, 0), blk_of(k1_ref, 0)], axis=0)
                vv = jnp.concatenate([blk_of(v1p_ref, 0), blk_of(v1_ref, 0)], axis=0)
                valid = valid_for(m == 0)
            else:
                kk = jnp.concatenate([blk_of(k1_ref, nn - 1), blk_of(k1_ref, nn)], axis=0)
                vv = jnp.concatenate([blk_of(v1_ref, nn - 1), blk_of(v1_ref, nn)], axis=0)
                valid = band
            o, lse = _window_unit(q, kk, vv, valid)
            dst = pl.ds(nn * w1 + r, blk, stride=d1)
            o_scr[1, dst, :] = o
            lse_scr[1, dst, :] = lse
    for r in range(d2):
        q = q2_ref[0, 0, r].reshape(blk, LANES)
        kk = jnp.concatenate([k2p_ref[0, 0, r].reshape(blk, LANES), k2_ref[0, 0, r].reshape(blk, LANES)], axis=0)
        vv = jnp.concatenate([v2p_ref[0, 0, r].reshape(blk, LANES), v2_ref[0, 0, r].reshape(blk, LANES)], axis=0)
        o, lse = _window_unit(q, kk, vv, valid_for(m == 0))
        dst = pl.ds(r, blk, stride=d2)
        o_scr[2, dst, :] = o
        lse_scr[2, dst, :] = lse

    def combine(c, carry):
        rows = pl.ds(pl.multiple_of(c * blk, blk), blk)
        ls = [lse_scr[g, rows, :] for g in range(3)]
        mx = jnp.maximum(jnp.maximum(ls[0], ls[1]), ls[2])
        es = [jnp.exp(x - mx) for x in ls]
        inv = 1.0 / (es[0] + es[1] + es[2])
        for g in range(3):
            sl = slice(g * LANES, (g + 1) * LANES)
            mix_ref[rows, sl] = (o_scr[g, rows, :] * (es[g] * inv) * sgb_ref[rows, sl]).astype(BF16)
        return carry

    lax.fori_loop(0, ATTN_TILE // blk, combine, 0)


def _prompt_attn(feats, sgb, b, s):
    q0, k0, v0, q1, k1, v1, q2, k2, v2 = feats
    w1, d1 = DIL_PAIRS[1]
    w2, d2 = DIL_PAIRS[2]
    n_tiles = s // ATTN_TILE
    per1 = ATTN_TILE // w1
    prev = lambda m, k: jnp.maximum(m * k - 1, 0)
    nat = pl.BlockSpec((1, ATTN_TILE, LANES), lambda i, m: (i, m, 0))
    nat_p = pl.BlockSpec((1, CHUNK, LANES), lambda i, m: (i, prev(m, ATTN_TILE // CHUNK), 0))
    g1 = pl.BlockSpec((1, per1) + q1.shape[2:], lambda i, m: (i, m, 0, 0, 0, 0))
    g1_p = pl.BlockSpec((1, 1) + q1.shape[2:], lambda i, m: (i, prev(m, per1), 0, 0, 0, 0))
    g2 = pl.BlockSpec((1, 1) + q2.shape[2:], lambda i, m: (i, m, 0, 0, 0, 0))
    g2_p = pl.BlockSpec((1, 1) + q2.shape[2:], lambda i, m: (i, prev(m, 1), 0, 0, 0, 0))
    rows = pl.BlockSpec((ATTN_TILE, ATTN_WIDTH), lambda i, m: (i * n_tiles + m, 0))
    return pl.pallas_call(
        _prompt_attn_kernel,
        grid=(b, n_tiles),
        in_specs=[nat, nat, nat, nat_p, nat_p, g1, g1, g1, g1_p, g1_p, g2, g2, g2, g2_p, g2_p, rows],
        out_specs=rows,
        out_shape=jax.ShapeDtypeStruct((b * s, ATTN_WIDTH), BF16),
        scratch_shapes=[
            pltpu.VMEM((3, ATTN_TILE, LANES), F32),
            pltpu.VMEM((3, ATTN_TILE, LANES), F32),
        ],
        compiler_params=pltpu.CompilerParams(
            dimension_semantics=("arbitrary", "arbitrary"), vmem_limit_bytes=VMEM_LIMIT),
        name="prompt_attn",
    )(q0, k0, v0, k0, v0, q1, k1, v1, k1, v1, q2, k2, v2, k2, v2, sgb)


def _sample_window(wref, oref, q_ref, k_ref, v_ref, bb, g, t_new):
    window, dil = DIL_PAIRS[g]
    length = wref.shape[2]
    rows = slice(bb * t_new, (bb + 1) * t_new)
    sl = slice(g * LANES, (g + 1) * LANES)
    qg, kg, vg = q_ref[rows, sl], k_ref[rows, sl], v_ref[rows, sl]
    lane_lo = _lane_iota(qg.shape) < HEAD_DIM
    qs = jnp.concatenate([jnp.where(lane_lo, qg, 0.0), jnp.where(lane_lo, 0.0, qg)],
                         axis=0).astype(BF16)
    k_old_t = wref[bb, 0:LANES, :].astype(BF16)
    v_old_t = wref[bb, LANES:2 * LANES, :].astype(BF16)
    zpad = jnp.zeros((CHUNK - t_new, LANES), F32)
    k_new = jnp.concatenate([kg, zpad], axis=0)
    v_new = jnp.concatenate([vg, zpad], axis=0)
    s_old = _dot(qs, k_old_t) * SCALE
    s_new = _dot_nt(qs, k_new.astype(BF16)) * SCALE
    tq = lax.broadcasted_iota(jnp.int32, s_old.shape, 0) & (t_new - 1)
    dist = length + tq - lax.broadcasted_iota(jnp.int32, s_old.shape, 1)
    ok_old = ((dist & (dil - 1)) == 0) & (dist <= window)
    tqn = lax.broadcasted_iota(jnp.int32, s_new.shape, 0) & (t_new - 1)
    dn = tqn - lax.broadcasted_iota(jnp.int32, s_new.shape, 1)
    ok_new = (dn >= 0) & ((dn & (dil - 1)) == 0) & (dn <= window)
    s_old = jnp.where(ok_old, s_old, NEG)
    s_new = jnp.where(ok_new, s_new, NEG)
    mx = jnp.maximum(jnp.max(s_old, axis=-1, keepdims=True), jnp.max(s_new, axis=-1, keepdims=True))
    e_old = jnp.exp(s_old - mx)
    e_new = jnp.exp(s_new - mx)
    l = jnp.sum(e_old, axis=-1, keepdims=True) + jnp.sum(e_new, axis=-1, keepdims=True)
    pv = (_dot_nt(e_old.astype(BF16), v_old_t)
          + _dot(e_new.astype(BF16), v_new.astype(BF16))) * (1.0 / l)
    lse = mx + jnp.log(l)
    out = jnp.where(lane_lo, pv[:t_new], pv[t_new:])
    lse_b = jnp.where(lane_lo, jnp.broadcast_to(lse[:t_new], qg.shape),
                      jnp.broadcast_to(lse[t_new:], qg.shape))
    keep = _lane_iota((2 * LANES, LANES)) < LANES - t_new
    new_t = jnp.concatenate([k_new.T, v_new.T], axis=0)
    nxt = pltpu.roll(wref[bb, :, 0:LANES], LANES - t_new, 1)
    for c in range(length // LANES):
        cur = nxt
        if c + 1 < length // LANES:
            nxt = pltpu.roll(wref[bb, :, (c + 1) * LANES:(c + 2) * LANES], LANES - t_new, 1)
        else:
            nxt = pltpu.roll(new_t, LANES - t_new, 1)
        oref[bb, :, c * LANES:(c + 1) * LANES] = jnp.where(keep, cur, nxt)
    return out, lse_b


def _sample_finish(outs, lses, mem_ref, qc_ref, sgb_ref, sgc_ref, mix_ref, bb, t_new):
    rows = slice(bb * t_new, (bb + 1) * t_new)
    mx = jnp.maximum(jnp.maximum(lses[0], lses[1]), lses[2])
    es = [jnp.exp(x - mx) for x in lses]
    inv = 1.0 / (es[0] + es[1] + es[2])
    for g in range(len(DIL_PAIRS)):
        sl = slice(g * LANES, (g + 1) * LANES)
        mix_ref[rows, sl] = outs[g] * (es[g] * inv) * sgb_ref[rows, sl]
    mk_t = mem_ref[bb, 0:MEM_WIDTH, :].astype(BF16)
    mv_t = mem_ref[bb, MEM_WIDTH:2 * MEM_WIDTH, :].astype(BF16)
    s = _dot(_stack_heads(qc_ref[rows, :], MEM_HEADS).astype(BF16), mk_t) * SCALE
    _, e, l = _softmax_parts(s)
    cm = _unstack_heads(_dot_nt(e.astype(BF16), mv_t) * (1.0 / l), MEM_HEADS)
    mix_ref[rows, ATTN_WIDTH:ATTN_WIDTH + MEM_WIDTH] = cm * sgc_ref[rows, :]


def _out_proj_kernel(x_ref, *refs):
    *mix_refs, w_ref, y_ref = refs
    mix = jnp.concatenate([r[...].astype(BF16) for r in mix_refs], axis=1)
    y_ref[...] = x_ref[...] + _dot(mix, w_ref[...])


def _out_proj(x, mixes, w_out):
    n = x.shape[0]
    rows = lambda width: pl.BlockSpec((OUT_TILE, width), lambda i: (i, 0))
    return pl.pallas_call(
        _out_proj_kernel,
        grid=(n // OUT_TILE,),
        in_specs=[rows(D_MODEL)] + [rows(mx.shape[1]) for mx in mixes]
        + [pl.BlockSpec((MIX_WIDTH, D_MODEL), lambda i: (0, 0))],
        out_specs=rows(D_MODEL),
        out_shape=jax.ShapeDtypeStruct((n, D_MODEL), F32),
        compiler_params=pltpu.CompilerParams(
            dimension_semantics=("arbitrary",), vmem_limit_bytes=VMEM_LIMIT),
        name="out_proj",
    )(x, *mixes, w_out.astype(BF16))


def kernel(x_prompt, x_sample, state_win0_kv, state_win1_kv, state_win2_kv, cache_mem_kv, mem_prompt, norm_gain, w_in, gmlp_ln_gain, gmlp_ln_bias, gmlp_w_s, gmlp_b_s, attn_q_norm, attn_k_norm, mem_norm, w_mem_kv, mem_q_norm, mem_k_norm, w_out):
    depth = norm_gain.shape[0]
    assert depth == 1, "single-layer step only"
    bp, s, _ = x_prompt.shape
    bd, t, _ = x_sample.shape
    past_len = PAST_LEN
    assert s % ATTN_TILE == 0 and (bd * t) % OUT_TILE == 0 and (bp * s) % OUT_TILE == 0
    assert CHUNK % t == 0 and t & (t - 1) == 0 and DIL_PAIRS[1][0] % ROW_TILE == 0
    states = (state_win0_kv, state_win1_kv, state_win2_kv)
    for st, (w, d) in zip(states, DIL_PAIRS):
        assert st.shape[2] == w and w // d == CHUNK and d & (d - 1) == 0

    common = _common_proj_operands(norm_gain[0], w_in[0], gmlp_ln_gain[0], gmlp_ln_bias[0],
                                   attn_q_norm[0], attn_k_norm[0], mem_q_norm[0])
    tril = jnp.tril(jnp.ones((CHUNK, CHUNK), dtype=bool))
    ws = jnp.where(tril[None], gmlp_w_s[0], 0.0)
    b_s = gmlp_b_s[0]

    reps = CHUNK // t
    ws_s = jnp.stack([jnp.kron(jnp.eye(reps, dtype=F32), ws[g, :t, :t]) for g in range(N_GMLP_GROUPS)])
    bias_s = jnp.repeat(jnp.tile(b_s[:, :t], (1, reps)).T, HEAD_DIM, axis=1)
    cos_s, sin_s = _rope_tables(past_len + (jnp.arange(ROW_TILE, dtype=jnp.int32) % t))
    mixa_s, vln_s, q_s, k_s, v_s, qc_s, sgb_s, sgc_s = _sample_proj(
        x_sample.reshape(bd * t, D_MODEL), common, cos_s, sin_s, _pair_ws(ws_s), bias_s)
    feat_major = lambda a: jnp.transpose(a, (0, 2, 3, 4, 1)).reshape(a.shape[0], -1, a.shape[1])
    wins = [feat_major(st[0]) for st in states]
    mem_s = feat_major(cache_mem_kv[0])

    mem_kv_p, mem_k, mem_v = _mem_kv(mem_prompt, mem_norm[0], w_mem_kv[0], mem_k_norm[0])
    cos_p, sin_p = _rope_tables(jnp.arange(s, dtype=jnp.int32))
    bias_p = jnp.repeat(b_s.T, HEAD_DIM, axis=1)
    outs = _prompt_proj(x_prompt, common, cos_p, sin_p, _pair_ws(ws), bias_p, mem_k, mem_v,
                        wins, mem_s, (q_s, k_s, v_s, qc_s, sgb_s, sgc_s), t)
    mixa_p, mixc_p, sgb_p = outs[0:3]
    tails = outs[12:15]
    new_wins_t = outs[15:19]
    mixb_p = _prompt_attn(outs[3:12], sgb_p, bp, s)
    y_prompt = _out_proj(x_prompt.reshape(bp * s, D_MODEL), (mixa_p, mixb_p, mixc_p), w_out[0])

    mixbc_s = new_wins_t[3]
    heads = LANES // HEAD_DIM
    nw0, nw1, nw2 = [
        jnp.transpose(w.reshape(bd, 2, heads, HEAD_DIM, w.shape[2]), (0, 4, 1, 2, 3)) for w in new_wins_t[:3]]
    y_sample = _out_proj(x_sample.reshape(bd * t, D_MODEL), (mixa_s, mixbc_s), w_out[0])

    win_shape = lambda n, w: (1, n, w, 2, heads, HEAD_DIM)
    return (
        y_prompt.reshape(bp, s, D_MODEL),
        y_sample.reshape(bd, t, D_MODEL),
        tails[0].reshape(win_shape(bp, DIL_PAIRS[0][0])),
        tails[1].reshape(win_shape(bp, DIL_PAIRS[1][0])),
        tails[2].reshape(win_shape(bp, DIL_PAIRS[2][0])),
        mem_kv_p.reshape(1, bp, N_MEM, 2, MEM_HEADS, HEAD_DIM),
        nw0.reshape(win_shape(bd, DIL_PAIRS[0][0])),
        nw1.reshape(win_shape(bd, DIL_PAIRS[1][0])),
        nw2.reshape(win_shape(bd, DIL_PAIRS[2][0])),
        vln_s.reshape(1, bd, t, N_GMLP_GROUPS, HEAD_DIM),
    )
```

```python
import functools

import numpy as np
import jax
import jax.numpy as jnp
from jax import lax
from jax.experimental import pallas as pl
from jax.experimental.pallas import tpu as pltpu

F32 = jnp.float32
BF16 = jnp.bfloat16

D_MODEL = 1024
HEAD_DIM = 64
LANES = 128
N_GMLP_GROUPS = 6
GMLP_WIDTH = N_GMLP_GROUPS * HEAD_DIM
CHUNK = 128
DIL_PAIRS = ((128, 1), (512, 4), (2048, 16))
ATTN_WIDTH = len(DIL_PAIRS) * LANES
N_MEM = 256
PAST_LEN = 8192
MEM_HEADS = 4
MEM_WIDTH = MEM_HEADS * HEAD_DIM
MIX_WIDTH = GMLP_WIDTH + ATTN_WIDTH + MEM_WIDTH
N_IN = 3 * GMLP_WIDTH + 4 * ATTN_WIDTH + 2 * MEM_WIDTH
ROPE_THETA = 500000.0
ROT_DIM = HEAD_DIM // 4
ROT_HALF = ROT_DIM // 2
EPS = 1e-6
NEG = -1e30
SCALE = HEAD_DIM ** -0.5

OFF_U = 0
OFF_V = OFF_U + GMLP_WIDTH
OFF_GA = OFF_V + GMLP_WIDTH
OFF_Q = OFF_GA + GMLP_WIDTH
OFF_K = OFF_Q + ATTN_WIDTH
OFF_VV = OFF_K + ATTN_WIDTH
OFF_GB = OFF_VV + ATTN_WIDTH
OFF_QC = OFF_GB + ATTN_WIDTH
OFF_GC = OFF_QC + MEM_WIDTH

ROW_TILE = 256
ATTN_TILE = 2048
SEQS_PER_STEP = 2
OUT_TILE = 1024
VMEM_LIMIT = 56 * 1024 * 1024


def _dot(a, b):
    return jnp.dot(a, b, preferred_element_type=F32)


def _dot_nt(a, b):
    return lax.dot_general(a, b, (((1,), (1,)), ((), ())), preferred_element_type=F32)


def _silu(x):
    return x * (1.0 / (1.0 + jnp.exp(-x)))


def _gelu(x):
    return 0.5 * x * (1.0 + lax.erf(x * np.float32(np.sqrt(0.5))))


def _lane_iota(shape):
    return lax.broadcasted_iota(jnp.int32, shape, len(shape) - 1)


def _head_sum(ss):
    r = (lax.broadcasted_iota(jnp.int32, (2 * LANES, LANES), 0) >> 6) & 1
    c = lax.broadcasted_iota(jnp.int32, (2 * LANES, LANES), 1) >> 6
    ones_blk = jnp.where(r == c, 1.0, 0.0).astype(BF16)
    hi = ss.astype(BF16)
    lo = (ss - hi.astype(F32)).astype(BF16)
    return _dot(jnp.concatenate([hi, lo], axis=1), ones_blk)


def _head_rms(x, gain):
    ms = _head_sum(x * x) * (1.0 / HEAD_DIM)
    return x * lax.rsqrt(ms + EPS) * gain


def _rope(x, cos, sin):
    lane = _lane_iota(x.shape) & (HEAD_DIM - 1)
    partner = jnp.where(lane < ROT_HALF, pltpu.roll(x, LANES - ROT_HALF, 1), pltpu.roll(x, ROT_HALF, 1))
    return x * cos + partner * sin


def _softmax_parts(s, axis=-1):
    mx = jnp.max(s, axis=axis, keepdims=True)
    e = jnp.exp(s - mx)
    return mx, e, jnp.sum(e, axis=axis, keepdims=True)


def _mem_kv_kernel(mem_ref, norm_ref, w_ref, kn_ref, kv_ref, k_ref, v_ref):
    x = mem_ref[0]
    h = x * lax.rsqrt(jnp.mean(x * x, axis=-1, keepdims=True) + EPS) * norm_ref[...]
    kv = _dot(h.astype(BF16), w_ref[...])
    for c in range(MEM_WIDTH // LANES):
        sl = slice(c * LANES, (c + 1) * LANES)
        kc = _head_rms(kv[:, sl], kn_ref[...])
        kv_ref[0, :, sl] = kc
        k_ref[0, :, sl] = kc.astype(BF16)
    v = kv[:, MEM_WIDTH:]
    kv_ref[0, :, MEM_WIDTH:] = v
    v_ref[0] = v.astype(BF16)


def _mem_kv(mem, mem_norm, w_mem_kv, mem_k_norm):
    b = mem.shape[0]
    return pl.pallas_call(
        _mem_kv_kernel,
        grid=(b,),
        in_specs=[
            pl.BlockSpec((1, N_MEM, D_MODEL), lambda i: (i, 0, 0)),
            pl.BlockSpec((1, D_MODEL), lambda i: (0, 0)),
            pl.BlockSpec((D_MODEL, 2 * MEM_WIDTH), lambda i: (0, 0)),
            pl.BlockSpec((1, LANES), lambda i: (0, 0)),
        ],
        out_specs=[
            pl.BlockSpec((1, N_MEM, 2 * MEM_WIDTH), lambda i: (i, 0, 0)),
            pl.BlockSpec((1, N_MEM, MEM_WIDTH), lambda i: (i, 0, 0)),
            pl.BlockSpec((1, N_MEM, MEM_WIDTH), lambda i: (i, 0, 0)),
        ],
        out_shape=[
            jax.ShapeDtypeStruct((b, N_MEM, 2 * MEM_WIDTH), F32),
            jax.ShapeDtypeStruct((b, N_MEM, MEM_WIDTH), BF16),
            jax.ShapeDtypeStruct((b, N_MEM, MEM_WIDTH), BF16),
        ],
        name="mem_kv",
    )(mem, mem_norm.reshape(1, D_MODEL), w_mem_kv.astype(BF16),
      jnp.tile(mem_k_norm, LANES // HEAD_DIM).reshape(1, LANES))


def _project(x_ref, gain_ref, w_ref, h_scr, z_scr):
    x = x_ref[...]
    h = x * lax.rsqrt(jnp.mean(x * x, axis=-1, keepdims=True) + EPS) * gain_ref[...]
    h_scr[...] = h.astype(BF16)
    for lo, width in ((OFF_U, GMLP_WIDTH), (OFF_V, GMLP_WIDTH), (OFF_GA, GMLP_WIDTH),
                      (OFF_Q, ATTN_WIDTH), (OFF_K, ATTN_WIDTH), (OFF_VV, ATTN_WIDTH),
                      (OFF_GB, ATTN_WIDTH), (OFF_QC, MEM_WIDTH), (OFF_GC, MEM_WIDTH)):
        z_scr[:, lo:lo + width] = _dot(h_scr[...], w_ref[:, lo:lo + width])


def _gmlp_chunk(z_scr, rows, lng_ref, lnb_ref, ws_ref, bias_ref):
    gu = _gelu(z_scr[rows, OFF_U:OFF_U + GMLP_WIDTH])
    gv = _gelu(z_scr[rows, OFF_V:OFF_V + GMLP_WIDTH])
    mu = jnp.mean(gv, axis=-1, keepdims=True)
    dv = gv - mu
    var = jnp.mean(dv * dv, axis=-1, keepdims=True)
    vln = dv * lax.rsqrt(var + EPS) * lng_ref[...] + lnb_ref[...]
    lane = _lane_iota((CHUNK, LANES))
    pieces = []
    for p in range(GMLP_WIDTH // LANES):
        sl = slice(p * LANES, (p + 1) * LANES)
        vp = vln[:, sl]
        rhs = jnp.concatenate([jnp.where(lane < HEAD_DIM, vp, 0.0),
                               jnp.where(lane < HEAD_DIM, 0.0, vp)], axis=0).astype(BF16)
        s = _dot(ws_ref[p], rhs) + bias_ref[:, sl]
        ga = z_scr[rows, OFF_GA + p * LANES:OFF_GA + (p + 1) * LANES]
        pieces.append(gu[:, sl] * s * _silu(ga))
    return pieces, vln


def _qk_chunk(z_scr, rows, off, g, norm_ref, cos, sin):
    x = z_scr[rows, off + g * LANES:off + (g + 1) * LANES]
    return _rope(_head_rms(x, norm_ref[...]), cos, sin)


def _stack_heads(x, n_heads):
    head = _lane_iota(x.shape) >> 6
    return jnp.concatenate([jnp.where(head == h, x, 0.0) for h in range(n_heads)], axis=0)


def _unstack_heads(y, n_heads):
    r = y.shape[0] // n_heads
    head = _lane_iota((r, y.shape[1])) >> 6
    out = y[0:r]
    for h in range(1, n_heads):
        out = jnp.where(head == h, y[h * r:(h + 1) * r], out)
    return out


def _mem_attend(qc, mk, mv):
    s = _dot_nt(_stack_heads(qc, MEM_HEADS).astype(BF16), mk) * SCALE
    _, e, l = _softmax_parts(s)
    pv = _dot(e.astype(BF16), mv) * (1.0 / l)
    return _unstack_heads(pv, MEM_HEADS)


def _prompt_proj_kernel(x_ref, gain_ref, w_ref, lng_ref, lnb_ref, qn_ref, kn_ref, mqn_ref,
                        cos_ref, sin_ref, ws_ref, bias_ref, mk_ref, mv_ref,
                        w0_ref, w1_ref, w2_ref, mem_ref, sq_ref, sk_ref, sv_ref, sqc_ref, ssgb_ref, ssgc_ref,
                        mixa_ref, mixc_ref, sgb_ref,
                        q0_ref, k0_ref, v0_ref, q1_ref, k1_ref, v1_ref, q2_ref, k2_ref, v2_ref,
                        t0_ref, t1_ref, t2_ref,
                        o0_ref, o1_ref, o2_ref, smix_ref,
                        h_scr, z_scr, perm_scr, kv16_scr, mem16_scr, *, t_new):
    win_refs, out_refs = (w0_ref, w1_ref, w2_ref), (o0_ref, o1_ref, o2_ref)
    kv16_offs, _ = _window_offsets(win_refs)
    seqs = range(SEQS_PER_STEP)
    groups = range(len(DIL_PAIRS))
    windows = {(bb, g): _SampleWindow(win_refs[g], out_refs[g], kv16_scr, kv16_offs[g],
                                      sq_ref, sk_ref, sv_ref, bb, g, t_new) for bb in seqs for g in groups}
    memories = [_SampleMemory(mem16_scr, sqc_ref, bb, t_new) for bb in seqs]
    sample_pieces = list(windows.values()) + memories

    def sample_finish():
        for bb in seqs:
            outs, lses = zip(*(windows[bb, g].values() for g in groups))
            _sample_finish(outs, lses, memories[bb].values(), ssgb_ref, ssgc_ref, smix_ref, bb, t_new)

    _stage_bf16(win_refs, mem_ref, kv16_scr, mem16_scr)

    x = x_ref[...]
    h = x * lax.rsqrt(jnp.mean(x * x, axis=-1, keepdims=True) + EPS) * gain_ref[...]
    h_scr[...] = h.astype(BF16)

    def project(lo, width):
        z_scr[:, lo:lo + width] = _dot(h_scr[...], w_ref[:, lo:lo + width])

    chunks = [slice(c * CHUNK, (c + 1) * CHUNK) for c in range(ROW_TILE // CHUNK)]
    nat_refs = (q0_ref, k0_ref, v0_ref)

    pm, gm, qk = {}, {}, {}

    def pm_queries():
        qs = []
        for rows in chunks:
            qc = jnp.concatenate(
                [_head_rms(z_scr[rows, OFF_QC + j * LANES:OFF_QC + (j + 1) * LANES], mqn_ref[...])
                 for j in range(MEM_WIDTH // LANES)], axis=1)
            qs.append(_stack_heads(qc, MEM_HEADS).astype(BF16))
        pm["qs"] = jnp.concatenate(qs, axis=0)

    def pm_scores():
        pm["s"] = _dot_nt(pm["qs"], mk_ref[0]) * SCALE

    def pm_softmax():
        _, e, l = _softmax_parts(pm["s"])
        pm["e"], pm["inv_l"] = e.astype(BF16), 1.0 / l

    def pm_values():
        pv = _dot(pm["e"], mv_ref[0]) * pm["inv_l"]
        per = MEM_HEADS * CHUNK
        for c, rows in enumerate(chunks):
            cm = _unstack_heads(pv[c * per:(c + 1) * per], MEM_HEADS)
            mixc_ref[rows, :] = (cm * _silu(z_scr[rows, OFF_GC:OFF_GC + MEM_WIDTH])).astype(BF16)

    def gm_prepare():
        lane = _lane_iota((CHUNK, LANES))
        for c, rows in enumerate(chunks):
            z_scr[rows, OFF_U:OFF_U + GMLP_WIDTH] = _gelu(z_scr[rows, OFF_U:OFF_U + GMLP_WIDTH])
            gv = _gelu(z_scr[rows, OFF_V:OFF_V + GMLP_WIDTH])
            mu = jnp.mean(gv, axis=-1, keepdims=True)
            dv = gv - mu
            var = jnp.mean(dv * dv, axis=-1, keepdims=True)
            vln = dv * lax.rsqrt(var + EPS) * lng_ref[...] + lnb_ref[...]
            for p in range(GMLP_WIDTH // LANES):
                vp = vln[:, p * LANES:(p + 1) * LANES]
                gm[c, p] = jnp.concatenate([jnp.where(lane < HEAD_DIM, vp, 0.0),
                                            jnp.where(lane < HEAD_DIM, 0.0, vp)], axis=0).astype(BF16)

    def gm_spatial():
        for key in list(gm):
            gm[key] = _dot(ws_ref[key[1]], gm[key])

    def gm_gate():
        for c, rows in enumerate(chunks):
            for p in range(GMLP_WIDTH // LANES):
                sl = slice(p * LANES, (p + 1) * LANES)
                s = gm[c, p] + bias_ref[:, sl]
                ga = z_scr[rows, OFF_GA + p * LANES:OFF_GA + (p + 1) * LANES]
                gu = z_scr[rows, OFF_U + p * LANES:OFF_U + (p + 1) * LANES]
                mixa_ref[rows, sl] = (gu * s * _silu(ga)).astype(BF16)

    def qk_sums():
        for c, rows in enumerate(chunks):
            for g in groups:
                for off in (OFF_Q, OFF_K):
                    x = z_scr[rows, off + g * LANES:off + (g + 1) * LANES]
                    qk[c, g, off] = _head_sum(x * x)

    def qk_normed(c, rows, g, off, norm_ref, cos, sin):
        x = z_scr[rows, off + g * LANES:off + (g + 1) * LANES]
        return _rope(x * lax.rsqrt(qk[c, g, off] * (1.0 / HEAD_DIM) + EPS) * norm_ref[...], cos, sin)

    project(OFF_QC, MEM_WIDTH)
    project(OFF_GC, MEM_WIDTH)
    for piece in sample_pieces:
        piece.scores()
    project(OFF_U, GMLP_WIDTH)
    for piece in sample_pieces:
        piece.softmax()
    pm_queries()
    project(OFF_V, GMLP_WIDTH)
    pm_scores()
    project(OFF_GA, GMLP_WIDTH)
    pm_softmax()
    gm_prepare()
    for g in groups:
        windows[0, g].shift()
    project(OFF_Q, ATTN_WIDTH)
    sample_finish()
    pm_values()
    gm_spatial()
    project(OFF_K, ATTN_WIDTH)
    gm_gate()
    project(OFF_VV, ATTN_WIDTH)
    qk_sums()
    project(OFF_GB, ATTN_WIDTH)
    for bb in seqs[1:]:
        for g in groups:
            windows[bb, g].shift()
    for c, rows in enumerate(chunks):
        cos = cos_ref[rows, :]
        sin = sin_ref[rows, :]
        for g in range(len(DIL_PAIRS)):
            q = qk_normed(c, rows, g, OFF_Q, qn_ref, cos, sin)
            k = qk_normed(c, rows, g, OFF_K, kn_ref, cos, sin)
            v = z_scr[rows, OFF_VV + g * LANES:OFF_VV + (g + 1) * LANES]
            if g == 0:
                for ref, val in zip(nat_refs, (q, k, v)):
                    ref[0, rows, :] = val.astype(BF16)
                if rows is chunks[-1]:
                    t0_ref[0, :, 0:LANES] = k
                    t0_ref[0, :, LANES:2 * LANES] = v
            else:
                for j, val in enumerate((q, k, v)):
                    perm_scr[3 * (g - 1) + j, rows, :] = val
                tail = (t1_ref, t2_ref)[g - 1]
                tail[0, rows, 0:LANES] = k
                tail[0, rows, LANES:2 * LANES] = v

    for rows in chunks:
        sgb_ref[rows, :] = _silu(z_scr[rows, OFF_GB:OFF_GB + ATTN_WIDTH])

    d1 = DIL_PAIRS[1][1]
    d2 = DIL_PAIRS[2][1]
    for j, ref in enumerate((q1_ref, k1_ref, v1_ref)):
        for r in range(d1):
            ref[0, 0, r, 0] = perm_scr[j, pl.ds(r, ROW_TILE // d1, stride=d1), :].astype(BF16)
    for j, ref in enumerate((q2_ref, k2_ref, v2_ref)):
        for r in range(d2):
            ref[0, 0, r, 0] = perm_scr[3 + j, pl.ds(r, ROW_TILE // d2, stride=d2), :].astype(BF16)


def _sample_proj_kernel(x_ref, gain_ref, w_ref, lng_ref, lnb_ref, qn_ref, kn_ref, mqn_ref,
                        cos_ref, sin_ref, ws_ref, bias_ref,
                        mixa_ref, vln_ref, q_ref, k_ref, v_ref, qc_ref, sgb_ref, sgc_ref,
                        h_scr, z_scr):
    _project(x_ref, gain_ref, w_ref, h_scr, z_scr)

    def chunk(c, carry):
        r0 = pl.multiple_of(c * CHUNK, CHUNK)
        rows = pl.ds(r0, CHUNK)
        pieces, vln = _gmlp_chunk(z_scr, rows, lng_ref, lnb_ref, ws_ref, bias_ref)
        for p, a in enumerate(pieces):
            mixa_ref[rows, p * LANES:(p + 1) * LANES] = a.astype(BF16)
        vln_ref[rows, :] = vln
        cos = cos_ref[rows, :]
        sin = sin_ref[rows, :]
        for g in range(len(DIL_PAIRS)):
            sl = slice(g * LANES, (g + 1) * LANES)
            q_ref[rows, sl] = _qk_chunk(z_scr, rows, OFF_Q, g, qn_ref, cos, sin)
            k_ref[rows, sl] = _qk_chunk(z_scr, rows, OFF_K, g, kn_ref, cos, sin)
            v_ref[rows, sl] = z_scr[rows, OFF_VV + g * LANES:OFF_VV + (g + 1) * LANES]
        sgb_ref[rows, :] = _silu(z_scr[rows, OFF_GB:OFF_GB + ATTN_WIDTH])
        for j in range(MEM_WIDTH // LANES):
            sl = slice(j * LANES, (j + 1) * LANES)
            qc_ref[rows, sl] = _head_rms(z_scr[rows, OFF_QC + j * LANES:OFF_QC + (j + 1) * LANES],
                                         mqn_ref[...])
        sgc_ref[rows, :] = _silu(z_scr[rows, OFF_GC:OFF_GC + MEM_WIDTH])
        return carry

    lax.fori_loop(0, ROW_TILE // CHUNK, chunk, 0)


def _rope_tables(pos):
    inv = ROPE_THETA ** (-jnp.arange(ROT_HALF, dtype=F32) * 2.0 / ROT_DIM)
    ang = pos.astype(F32)[:, None] * inv[None, :]
    cos, sin = jnp.cos(ang), jnp.sin(ang)
    n = pos.shape[0]
    pad1 = jnp.ones((n, HEAD_DIM - ROT_DIM), F32)
    pad0 = jnp.zeros((n, HEAD_DIM - ROT_DIM), F32)
    cos_h = jnp.concatenate([cos, cos, pad1], axis=1)
    sin_h = jnp.concatenate([-sin, sin, pad0], axis=1)
    return jnp.tile(cos_h, (1, LANES // HEAD_DIM)), jnp.tile(sin_h, (1, LANES // HEAD_DIM))


def _common_proj_operands(norm_gain, w_in, ln_g, ln_b, q_norm, k_norm, mem_q_norm):
    tile2 = lambda g: jnp.tile(g, LANES // HEAD_DIM).reshape(1, LANES)
    return (norm_gain.reshape(1, D_MODEL), w_in.astype(BF16), ln_g.reshape(1, GMLP_WIDTH),
            ln_b.reshape(1, GMLP_WIDTH), tile2(q_norm), tile2(k_norm), tile2(mem_q_norm))


def _common_proj_specs(const):
    return [
        const((1, D_MODEL)), const((D_MODEL, N_IN)), const((1, GMLP_WIDTH)), const((1, GMLP_WIDTH)),
        const((1, LANES)), const((1, LANES)), const((1, LANES)),
    ]


def _pair_ws(ws):
    return jnp.concatenate([ws[0::2], ws[1::2]], axis=2).astype(BF16)


def _prompt_proj(x, common, cos, sin, ws_pair, bias, mem_k, mem_v, wins, mem_s, sample_feats, t_new):
    b, s, _ = x.shape
    n_tiles = s // ROW_TILE
    bd = mem_s.shape[0]
    assert b * n_tiles * SEQS_PER_STEP == bd
    w0, w1, w2 = (w for w, _ in DIL_PAIRS)
    d1, d2 = DIL_PAIRS[1][1], DIL_PAIRS[2][1]
    per1, per2 = w1 // ROW_TILE, w2 // ROW_TILE
    x2 = x.reshape(b * s, D_MODEL)
    const = lambda shape: pl.BlockSpec(shape, lambda i, j: (0,) * len(shape))
    rows = lambda width: pl.BlockSpec((ROW_TILE, width), lambda i, j: (i * n_tiles + j, 0))
    seq_blk = lambda a: pl.BlockSpec((SEQS_PER_STEP,) + a.shape[1:], lambda i, j: (i * n_tiles + j, 0, 0))
    seq_rows = lambda width: pl.BlockSpec((SEQS_PER_STEP * t_new, width), lambda i, j: (i * n_tiles + j, 0))
    in_specs = [rows(D_MODEL)] + _common_proj_specs(const) + [
        pl.BlockSpec((ROW_TILE, LANES), lambda i, j: (j, 0)),
        pl.BlockSpec((ROW_TILE, LANES), lambda i, j: (j, 0)),
        const((GMLP_WIDTH // LANES, CHUNK, 2 * CHUNK)), const((CHUNK, GMLP_WIDTH)),
        pl.BlockSpec((1, N_MEM, MEM_WIDTH), lambda i, j: (i, 0, 0)),
        pl.BlockSpec((1, N_MEM, MEM_WIDTH), lambda i, j: (i, 0, 0)),
    ] + [seq_blk(w) for w in wins] + [seq_blk(mem_s)] + [seq_rows(f.shape[1]) for f in sample_feats]
    nat = pl.BlockSpec((1, ROW_TILE, LANES), lambda i, j: (i, j, 0))
    g1 = pl.BlockSpec((1, 1, d1, 1, ROW_TILE // d1, LANES), lambda i, j: (i, j // per1, 0, j % per1, 0, 0))
    g2 = pl.BlockSpec((1, 1, d2, 1, ROW_TILE // d2, LANES), lambda i, j: (i, j // per2, 0, j % per2, 0, 0))
    last = lambda n_blk: (lambda i, j: (i, jnp.maximum(j - (n_tiles - n_blk), 0), 0))
    t0 = pl.BlockSpec((1, w0, 2 * LANES), last(1))
    t1 = pl.BlockSpec((1, ROW_TILE, 2 * LANES), last(per1))
    t2 = pl.BlockSpec((1, ROW_TILE, 2 * LANES), last(per2))
    out_specs = [rows(GMLP_WIDTH), rows(MEM_WIDTH), rows(ATTN_WIDTH),
                 nat, nat, nat, g1, g1, g1, g2, g2, g2, t0, t1, t2] \
        + [seq_blk(w) for w in wins] + [seq_rows(ATTN_WIDTH + MEM_WIDTH)]
    nat_s = jax.ShapeDtypeStruct((b, s, LANES), BF16)
    g1_s = jax.ShapeDtypeStruct((b, s // w1, d1, per1, ROW_TILE // d1, LANES), BF16)
    g2_s = jax.ShapeDtypeStruct((b, s // w2, d2, per2, ROW_TILE // d2, LANES), BF16)
    out_shape = [
        jax.ShapeDtypeStruct((b * s, GMLP_WIDTH), BF16),
        jax.ShapeDtypeStruct((b * s, MEM_WIDTH), BF16),
        jax.ShapeDtypeStruct((b * s, ATTN_WIDTH), F32),
        nat_s, nat_s, nat_s, g1_s, g1_s, g1_s, g2_s, g2_s, g2_s,
        jax.ShapeDtypeStruct((b, w0, 2 * LANES), F32),
        jax.ShapeDtypeStruct((b, w1, 2 * LANES), F32),
        jax.ShapeDtypeStruct((b, w2, 2 * LANES), F32),
    ] + [jax.ShapeDtypeStruct(w.shape, F32) for w in wins] \
      + [jax.ShapeDtypeStruct((bd * t_new, ATTN_WIDTH + MEM_WIDTH), F32)]
    return pl.pallas_call(
        functools.partial(_prompt_proj_kernel, t_new=t_new),
        grid=(b, n_tiles),
        in_specs=in_specs,
        out_specs=out_specs,
        out_shape=out_shape,
        scratch_shapes=[
            pltpu.VMEM((ROW_TILE, D_MODEL), BF16),
            pltpu.VMEM((ROW_TILE, N_IN), F32),
            pltpu.VMEM((6, ROW_TILE, LANES), F32),
            pltpu.VMEM((SEQS_PER_STEP, 2 * LANES, sum(w.shape[2] for w in wins)), BF16),
            pltpu.VMEM((SEQS_PER_STEP,) + mem_s.shape[1:], BF16),
        ],
        compiler_params=pltpu.CompilerParams(
            dimension_semantics=("arbitrary", "arbitrary"), vmem_limit_bytes=VMEM_LIMIT),
        name="prompt_proj",
    )(x2, *common, cos, sin, ws_pair, bias, mem_k, mem_v, *wins, mem_s, *sample_feats)


def _sample_proj(x, common, cos, sin, ws_pair, bias):
    n = x.shape[0]
    const = lambda shape: pl.BlockSpec(shape, lambda i: (0,) * len(shape))
    rows = lambda width: pl.BlockSpec((ROW_TILE, width), lambda i: (i, 0))
    in_specs = [rows(D_MODEL)] + _common_proj_specs(const) + [
        const((ROW_TILE, LANES)), const((ROW_TILE, LANES)),
        const((GMLP_WIDTH // LANES, CHUNK, 2 * CHUNK)), const((CHUNK, GMLP_WIDTH)),
    ]
    widths = (GMLP_WIDTH, GMLP_WIDTH, ATTN_WIDTH, ATTN_WIDTH, ATTN_WIDTH, MEM_WIDTH, ATTN_WIDTH, MEM_WIDTH)
    dtypes = (BF16,) + (F32,) * 7
    return pl.pallas_call(
        _sample_proj_kernel,
        grid=(n // ROW_TILE,),
        in_specs=in_specs,
        out_specs=[rows(w) for w in widths],
        out_shape=[jax.ShapeDtypeStruct((n, w), dt) for w, dt in zip(widths, dtypes)],
        scratch_shapes=[
            pltpu.VMEM((ROW_TILE, D_MODEL), BF16),
            pltpu.VMEM((ROW_TILE, N_IN), F32),
        ],
        compiler_params=pltpu.CompilerParams(
            dimension_semantics=("arbitrary",), vmem_limit_bytes=VMEM_LIMIT),
        name="sample_proj",
    )(x, *common, cos, sin, ws_pair, bias)


def _window_unit(q, kk, vv, valid):
    lane_lo = _lane_iota(q.shape) < HEAD_DIM
    zero = jnp.zeros_like(q)
    qs = jnp.concatenate([jnp.where(lane_lo, q, zero), jnp.where(lane_lo, zero, q)], axis=0)
    s = _dot_nt(qs, kk) * SCALE
    s = jnp.where(valid, s, NEG)
    mx, e, l = _softmax_parts(s)
    pv = _dot(e.astype(BF16), vv) * (1.0 / l)
    lse = mx + jnp.log(l)
    n = q.shape[0]
    o = jnp.where(lane_lo, pv[:n], pv[n:])
    lse_b = jnp.where(lane_lo, jnp.broadcast_to(lse[:n], q.shape), jnp.broadcast_to(lse[n:], q.shape))
    return o, lse_b


def _prompt_attn_kernel(q0_ref, k0_ref, v0_ref, k0p_ref, v0p_ref,
                        q1_ref, k1_ref, v1_ref, k1p_ref, v1p_ref,
                        q2_ref, k2_ref, v2_ref, k2p_ref, v2p_ref,
                        sgb_ref, mix_ref, o_scr, lse_scr):
    m = pl.program_id(1)
    blk = CHUNK
    qo = lax.broadcasted_iota(jnp.int32, (2 * blk, 2 * blk), 0) & (blk - 1)
    ko = lax.broadcasted_iota(jnp.int32, (2 * blk, 2 * blk), 1)
    band = (ko >= qo) & (ko <= qo + blk)

    def valid_for(first):
        return band & (ko >= blk * first.astype(jnp.int32))

    d1, d2 = DIL_PAIRS[1][1], DIL_PAIRS[2][1]
    w1 = DIL_PAIRS[1][0]
    for b0 in range(ATTN_TILE // blk):
        q = q0_ref[0, b0 * blk:(b0 + 1) * blk, :]
        if b0 == 0:
            kk = jnp.concatenate([k0p_ref[0], k0_ref[0, 0:blk, :]], axis=0)
            vv = jnp.concatenate([v0p_ref[0], v0_ref[0, 0:blk, :]], axis=0)
            valid = valid_for(m == 0)
        else:
            kk = k0_ref[0, (b0 - 1) * blk:(b0 + 1) * blk, :]
            vv = v0_ref[0, (b0 - 1) * blk:(b0 + 1) * blk, :]
            valid = band
        o, lse = _window_unit(q, kk, vv, valid)
        o_scr[0, b0 * blk:(b0 + 1) * blk, :] = o
        lse_scr[0, b0 * blk:(b0 + 1) * blk, :] = lse
    for nn in range(ATTN_TILE // w1):
        for r in range(d1):
            blk_of = lambda ref, n: ref[0, n, r].reshape(blk, LANES)
            q = blk_of(q1_ref, nn)
            if nn == 0:
                kk = jnp.concatenate([blk_of(k1p_ref, 0), blk_of(k1_ref, 0)], axis=0)
                vv = jnp.concatenate([blk_of(v1p_ref, 0), blk_of(v1_ref, 0)], axis=0)
                valid = valid_for(m == 0)
            else:
                kk = jnp.concatenate([blk_of(k1_ref, nn - 1), blk_of(k1_ref, nn)], axis=0)
                vv = jnp.concatenate([blk_of(v1_ref, nn - 1), blk_of(v1_ref, nn)], axis=0)
                valid = band
            o, lse = _window_unit(q, kk, vv, valid)
            dst = pl.ds(nn * w1 + r, blk, stride=d1)
            o_scr[1, dst, :] = o
            lse_scr[1, dst, :] = lse
    for r in range(d2):
        q = q2_ref[0, 0, r].reshape(blk, LANES)
        kk = jnp.concatenate([k2p_ref[0, 0, r].reshape(blk, LANES), k2_ref[0, 0, r].reshape(blk, LANES)], axis=0)
        vv = jnp.concatenate([v2p_ref[0, 0, r].reshape(blk, LANES), v2_ref[0, 0, r].reshape(blk, LANES)], axis=0)
        o, lse = _window_unit(q, kk, vv, valid_for(m == 0))
        dst = pl.ds(r, blk, stride=d2)
        o_scr[2, dst, :] = o
        lse_scr[2, dst, :] = lse

    def combine(c, carry):
        rows = pl.ds(pl.multiple_of(c * blk, blk), blk)
        ls = [lse_scr[g, rows, :] for g in range(3)]
        mx = jnp.maximum(jnp.maximum(ls[0], ls[1]), ls[2])
        es = [jnp.exp(x - mx) for x in ls]
        inv = 1.0 / (es[0] + es[1] + es[2])
        for g in range(3):
            sl = slice(g * LANES, (g + 1) * LANES)
            mix_ref[rows, sl] = (o_scr[g, rows, :] * (es[g] * inv) * sgb_ref[rows, sl]).astype(BF16)
        return carry

    lax.fori_loop(0, ATTN_TILE // blk, combine, 0)


def _prompt_attn(feats, sgb, b, s):
    q0, k0, v0, q1, k1, v1, q2, k2, v2 = feats
    w1, d1 = DIL_PAIRS[1]
    w2, d2 = DIL_PAIRS[2]
    n_tiles = s // ATTN_TILE
    per1 = ATTN_TILE // w1
    prev = lambda m, k: jnp.maximum(m * k - 1, 0)
    nat = pl.BlockSpec((1, ATTN_TILE, LANES), lambda i, m: (i, m, 0))
    nat_p = pl.BlockSpec((1, CHUNK, LANES), lambda i, m: (i, prev(m, ATTN_TILE // CHUNK), 0))
    g1 = pl.BlockSpec((1, per1) + q1.shape[2:], lambda i, m: (i, m, 0, 0, 0, 0))
    g1_p = pl.BlockSpec((1, 1) + q1.shape[2:], lambda i, m: (i, prev(m, per1), 0, 0, 0, 0))
    g2 = pl.BlockSpec((1, 1) + q2.shape[2:], lambda i, m: (i, m, 0, 0, 0, 0))
    g2_p = pl.BlockSpec((1, 1) + q2.shape[2:], lambda i, m: (i, prev(m, 1), 0, 0, 0, 0))
    rows = pl.BlockSpec((ATTN_TILE, ATTN_WIDTH), lambda i, m: (i * n_tiles + m, 0))
    return pl.pallas_call(
        _prompt_attn_kernel,
        grid=(b, n_tiles),
        in_specs=[nat, nat, nat, nat_p, nat_p, g1, g1, g1, g1_p, g1_p, g2, g2, g2, g2_p, g2_p, rows],
        out_specs=rows,
        out_shape=jax.ShapeDtypeStruct((b * s, ATTN_WIDTH), BF16),
        scratch_shapes=[
            pltpu.VMEM((3, ATTN_TILE, LANES), F32),
            pltpu.VMEM((3, ATTN_TILE, LANES), F32),
        ],
        compiler_params=pltpu.CompilerParams(
            dimension_semantics=("arbitrary", "arbitrary"), vmem_limit_bytes=VMEM_LIMIT),
        name="prompt_attn",
    )(q0, k0, v0, k0, v0, q1, k1, v1, k1, v1, q2, k2, v2, k2, v2, sgb)


def _window_offsets(win_refs):
    offs, total = [], 0
    for r in win_refs:
        offs.append(total)
        total += r.shape[2]
    return offs, total


def _stage_bf16(win_refs, mem_ref, kv16_ref, mem16_ref):
    offs, _ = _window_offsets(win_refs)
    for bb in range(mem_ref.shape[0]):
        for wref, off in zip(win_refs, offs):
            kv16_ref[bb, :, off:off + wref.shape[2]] = wref[bb].astype(BF16)
        mem16_ref[bb] = mem_ref[bb].astype(BF16)


class _SampleWindow:
    def __init__(self, wref, oref, kv16_ref, kv16_off, q_ref, k_ref, v_ref, bb, g, t_new):
        self.wref, self.oref, self.bb, self.t_new = wref, oref, bb, t_new
        self.window, self.dil = DIL_PAIRS[g]
        self.length = wref.shape[2]
        rows = slice(bb * t_new, (bb + 1) * t_new)
        sl = slice(g * LANES, (g + 1) * LANES)
        self.load_new = lambda: (q_ref[rows, sl], k_ref[rows, sl], v_ref[rows, sl])
        self.kv16 = lambda half: kv16_ref[bb, half * LANES:(half + 1) * LANES, kv16_off:kv16_off + self.length]

    def scores(self):
        t_new = self.t_new
        qg, kg, vg = self.load_new()
        self.lane_lo = _lane_iota(qg.shape) < HEAD_DIM
        qs = jnp.concatenate([jnp.where(self.lane_lo, qg, 0.0), jnp.where(self.lane_lo, 0.0, qg)],
                             axis=0).astype(BF16)
        zpad = jnp.zeros((CHUNK - t_new, LANES), F32)
        self.k_new = jnp.concatenate([kg, zpad], axis=0)
        self.v_new = jnp.concatenate([vg, zpad], axis=0)
        self.s_old = _dot(qs, self.kv16(0)) * SCALE
        self.s_new = _dot_nt(qs, self.k_new.astype(BF16)) * SCALE

    def softmax(self):
        t_new, dil, window = self.t_new, self.dil, self.window
        s_old, s_new = self.s_old, self.s_new
        tq = lax.broadcasted_iota(jnp.int32, s_old.shape, 0) & (t_new - 1)
        dist = self.length + tq - lax.broadcasted_iota(jnp.int32, s_old.shape, 1)
        ok_old = ((dist & (dil - 1)) == 0) & (dist <= window)
        tqn = lax.broadcasted_iota(jnp.int32, s_new.shape, 0) & (t_new - 1)
        dn = tqn - lax.broadcasted_iota(jnp.int32, s_new.shape, 1)
        ok_new = (dn >= 0) & ((dn & (dil - 1)) == 0) & (dn <= window)
        s_old = jnp.where(ok_old, s_old, NEG)
        s_new = jnp.where(ok_new, s_new, NEG)
        mx = jnp.maximum(jnp.max(s_old, axis=-1, keepdims=True), jnp.max(s_new, axis=-1, keepdims=True))
        e_old = jnp.exp(s_old - mx)
        e_new = jnp.exp(s_new - mx)
        l = jnp.sum(e_old, axis=-1, keepdims=True) + jnp.sum(e_new, axis=-1, keepdims=True)
        self.e_old, self.e_new = e_old.astype(BF16), e_new.astype(BF16)
        self.inv_l = 1.0 / l
        self.lse = mx + jnp.log(l)

    def values(self):
        t_new, lane_lo = self.t_new, self.lane_lo
        pv = (_dot_nt(self.e_old, self.kv16(1)) + _dot(self.e_new, self.v_new.astype(BF16))) * self.inv_l
        out = jnp.where(lane_lo, pv[:t_new], pv[t_new:])
        shape = (t_new, LANES)
        lse_b = jnp.where(lane_lo, jnp.broadcast_to(self.lse[:t_new], shape),
                          jnp.broadcast_to(self.lse[t_new:], shape))
        return out, lse_b

    def shift(self):
        t_new, wref, oref, bb = self.t_new, self.wref, self.oref, self.bb
        _, kg, vg = self.load_new()
        zpad = jnp.zeros((CHUNK - t_new, LANES), F32)
        new_t = jnp.concatenate([jnp.concatenate([kg, zpad], axis=0).T,
                                 jnp.concatenate([vg, zpad], axis=0).T], axis=0)
        keep = _lane_iota((2 * LANES, LANES)) < LANES - t_new
        nxt = pltpu.roll(wref[bb, :, 0:LANES], LANES - t_new, 1)
        n_blk = self.length // LANES
        for c in range(n_blk):
            cur = nxt
            if c + 1 < n_blk:
                nxt = pltpu.roll(wref[bb, :, (c + 1) * LANES:(c + 2) * LANES], LANES - t_new, 1)
            else:
                nxt = pltpu.roll(new_t, LANES - t_new, 1)
            oref[bb, :, c * LANES:(c + 1) * LANES] = jnp.where(keep, cur, nxt)


class _SampleMemory:
    def __init__(self, mem16_ref, qc_ref, bb, t_new):
        self.rows = slice(bb * t_new, (bb + 1) * t_new)
        self.qc_ref = qc_ref
        self.half = lambda h: mem16_ref[bb, h * MEM_WIDTH:(h + 1) * MEM_WIDTH, :]

    def scores(self):
        self.s = _dot(_stack_heads(self.qc_ref[self.rows, :], MEM_HEADS).astype(BF16), self.half(0)) * SCALE

    def softmax(self):
        _, e, l = _softmax_parts(self.s)
        self.e, self.inv_l = e.astype(BF16), 1.0 / l

    def values(self):
        return _unstack_heads(_dot_nt(self.e, self.half(1)) * self.inv_l, MEM_HEADS)


def _sample_finish(outs, lses, cm, sgb_ref, sgc_ref, mix_ref, bb, t_new):
    rows = slice(bb * t_new, (bb + 1) * t_new)
    mx = jnp.maximum(jnp.maximum(lses[0], lses[1]), lses[2])
    es = [jnp.exp(x - mx) for x in lses]
    inv = 1.0 / (es[0] + es[1] + es[2])
    for g in range(len(DIL_PAIRS)):
        sl = slice(g * LANES, (g + 1) * LANES)
        mix_ref[rows, sl] = outs[g] * (es[g] * inv) * sgb_ref[rows, sl]
    mix_ref[rows, ATTN_WIDTH:ATTN_WIDTH + MEM_WIDTH] = cm * sgc_ref[rows, :]


def _out_proj_kernel(x_ref, *refs):
    *mix_refs, w_ref, y_ref = refs
    mix = jnp.concatenate([r[...].astype(BF16) for r in mix_refs], axis=1)
    y_ref[...] = x_ref[...] + _dot(mix, w_ref[...])


def _out_proj(x, mixes, w_out):
    n = x.shape[0]
    rows = lambda width: pl.BlockSpec((OUT_TILE, width), lambda i: (i, 0))
    return pl.pallas_call(
        _out_proj_kernel,
        grid=(n // OUT_TILE,),
        in_specs=[rows(D_MODEL)] + [rows(mx.shape[1]) for mx in mixes]
        + [pl.BlockSpec((MIX_WIDTH, D_MODEL), lambda i: (0, 0))],
        out_specs=rows(D_MODEL),
        out_shape=jax.ShapeDtypeStruct((n, D_MODEL), F32),
        compiler_params=pltpu.CompilerParams(
            dimension_semantics=("arbitrary",), vmem_limit_bytes=VMEM_LIMIT),
        name="out_proj",
    )(x, *mixes, w_out.astype(BF16))


def kernel(x_prompt, x_sample, state_win0_kv, state_win1_kv, state_win2_kv, cache_mem_kv, mem_prompt, norm_gain, w_in, gmlp_ln_gain, gmlp_ln_bias, gmlp_w_s, gmlp_b_s, attn_q_norm, attn_k_norm, mem_norm, w_mem_kv, mem_q_norm, mem_k_norm, w_out):
    depth = norm_gain.shape[0]
    assert depth == 1, "single-layer step only"
    bp, s, _ = x_prompt.shape
    bd, t, _ = x_sample.shape
    past_len = PAST_LEN
    assert s % ATTN_TILE == 0 and (bd * t) % OUT_TILE == 0 and (bp * s) % OUT_TILE == 0
    assert CHUNK % t == 0 and t & (t - 1) == 0 and DIL_PAIRS[1][0] % ROW_TILE == 0
    states = (state_win0_kv, state_win1_kv, state_win2_kv)
    for st, (w, d) in zip(states, DIL_PAIRS):
        assert st.shape[2] == w and w // d == CHUNK and d & (d - 1) == 0

    common = _common_proj_operands(norm_gain[0], w_in[0], gmlp_ln_gain[0], gmlp_ln_bias[0],
                                   attn_q_norm[0], attn_k_norm[0], mem_q_norm[0])
    tril = jnp.tril(jnp.ones((CHUNK, CHUNK), dtype=bool))
    ws = jnp.where(tril[None], gmlp_w_s[0], 0.0)
    b_s = gmlp_b_s[0]

    reps = CHUNK // t
    ws_s = jnp.stack([jnp.kron(jnp.eye(reps, dtype=F32), ws[g, :t, :t]) for g in range(N_GMLP_GROUPS)])
    bias_s = jnp.repeat(jnp.tile(b_s[:, :t], (1, reps)).T, HEAD_DIM, axis=1)
    cos_s, sin_s = _rope_tables(past_len + (jnp.arange(ROW_TILE, dtype=jnp.int32) % t))
    mixa_s, vln_s, q_s, k_s, v_s, qc_s, sgb_s, sgc_s = _sample_proj(
        x_sample.reshape(bd * t, D_MODEL), common, cos_s, sin_s, _pair_ws(ws_s), bias_s)
    feat_major = lambda a: jnp.transpose(a, (0, 2, 3, 4, 1)).reshape(a.shape[0], -1, a.shape[1])
    wins = [feat_major(st[0]) for st in states]
    mem_s = feat_major(cache_mem_kv[0])

    mem_kv_p, mem_k, mem_v = _mem_kv(mem_prompt, mem_norm[0], w_mem_kv[0], mem_k_norm[0])
    cos_p, sin_p = _rope_tables(jnp.arange(s, dtype=jnp.int32))
    bias_p = jnp.repeat(b_s.T, HEAD_DIM, axis=1)
    outs = _prompt_proj(x_prompt, common, cos_p, sin_p, _pair_ws(ws), bias_p, mem_k, mem_v,
                        wins, mem_s, (q_s, k_s, v_s, qc_s, sgb_s, sgc_s), t)
    mixa_p, mixc_p, sgb_p = outs[0:3]
    tails = outs[12:15]
    new_wins_t = outs[15:19]
    mixb_p = _prompt_attn(outs[3:12], sgb_p, bp, s)
    y_prompt = _out_proj(x_prompt.reshape(bp * s, D_MODEL), (mixa_p, mixb_p, mixc_p), w_out[0])

    mixbc_s = new_wins_t[3]
    heads = LANES // HEAD_DIM
    nw0, nw1, nw2 = [
        jnp.transpose(w.reshape(bd, 2, heads, HEAD_DIM, w.shape[2]), (0, 4, 1, 2, 3)) for w in new_wins_t[:3]]
    y_sample = _out_proj(x_sample.reshape(bd * t, D_MODEL), (mixa_s, mixbc_s), w_out[0])

    win_shape = lambda n, w: (1, n, w, 2, heads, HEAD_DIM)
    return (
        y_prompt.reshape(bp, s, D_MODEL),
        y_sample.reshape(bd, t, D_MODEL),
        tails[0].reshape(win_shape(bp, DIL_PAIRS[0][0])),
        tails[1].reshape(win_shape(bp, DIL_PAIRS[1][0])),
        tails[2].reshape(win_shape(bp, DIL_PAIRS[2][0])),
        mem_kv_p.reshape(1, bp, N_MEM, 2, MEM_HEADS, HEAD_DIM),
        nw0.reshape(win_shape(bd, DIL_PAIRS[0][0])),
        nw1.reshape(win_shape(bd, DIL_PAIRS[1][0])),
        nw2.reshape(win_shape(bd, DIL_PAIRS[2][0])),
        vln_s.reshape(1, bd, t, N_GMLP_GROUPS, HEAD_DIM),
    )
```

```python
import functools

import numpy as np
import jax
import jax.numpy as jnp
from jax import lax
from jax.experimental import pallas as pl
from jax.experimental.pallas import tpu as pltpu

F32 = jnp.float32
BF16 = jnp.bfloat16

D_MODEL = 1024
HEAD_DIM = 64
LANES = 128
N_GMLP_GROUPS = 6
GMLP_WIDTH = N_GMLP_GROUPS * HEAD_DIM
CHUNK = 128
DIL_PAIRS = ((128, 1), (512, 4), (2048, 16))
ATTN_WIDTH = len(DIL_PAIRS) * LANES
N_MEM = 256
PAST_LEN = 8192
MEM_HEADS = 4
MEM_WIDTH = MEM_HEADS * HEAD_DIM
MIX_WIDTH = GMLP_WIDTH + ATTN_WIDTH + MEM_WIDTH
N_IN = 3 * GMLP_WIDTH + 4 * ATTN_WIDTH + 2 * MEM_WIDTH
ROPE_THETA = 500000.0
ROT_DIM = HEAD_DIM // 4
ROT_HALF = ROT_DIM // 2
EPS = 1e-6
NEG = -1e30
SCALE = HEAD_DIM ** -0.5
assert np.frexp(SCALE)[0] == 0.5, "the prompt path folds SCALE into q, exact only for a power of two"

OFF_U = 0
OFF_V = OFF_U + GMLP_WIDTH
OFF_GA = OFF_V + GMLP_WIDTH
OFF_Q = OFF_GA + GMLP_WIDTH
OFF_K = OFF_Q + ATTN_WIDTH
OFF_VV = OFF_K + ATTN_WIDTH
OFF_GB = OFF_VV + ATTN_WIDTH
OFF_QC = OFF_GB + ATTN_WIDTH
OFF_GC = OFF_QC + MEM_WIDTH

ROW_TILE = 256
ATTN_TILE = 2048
SEQS_PER_STEP = 2
OUT_TILE = 1024
VMEM_LIMIT = 56 * 1024 * 1024


def _dot(a, b):
    return jnp.dot(a, b, preferred_element_type=F32)


def _dot_nt(a, b):
    return lax.dot_general(a, b, (((1,), (1,)), ((), ())), preferred_element_type=F32)


def _silu(x):
    return x * (1.0 / (1.0 + jnp.exp(-x)))


def _gelu(x):
    return 0.5 * x * (1.0 + lax.erf(x * np.float32(np.sqrt(0.5))))


def _lane_iota(shape):
    return lax.broadcasted_iota(jnp.int32, shape, len(shape) - 1)


def _head_sum(ss):
    r = (lax.broadcasted_iota(jnp.int32, (2 * LANES, LANES), 0) >> 6) & 1
    c = lax.broadcasted_iota(jnp.int32, (2 * LANES, LANES), 1) >> 6
    ones_blk = jnp.where(r == c, 1.0, 0.0).astype(BF16)
    hi = ss.astype(BF16)
    lo = (ss - hi.astype(F32)).astype(BF16)
    return _dot(jnp.concatenate([hi, lo], axis=1), ones_blk)


def _head_rms(x, gain):
    ms = _head_sum(x * x) * (1.0 / HEAD_DIM)
    return x * lax.rsqrt(ms + EPS) * gain


def _rope(x, cos, sin):
    lane = _lane_iota(x.shape) & (HEAD_DIM - 1)
    partner = jnp.where(lane < ROT_HALF, pltpu.roll(x, LANES - ROT_HALF, 1), pltpu.roll(x, ROT_HALF, 1))
    return x * cos + partner * sin


def _softmax_parts(s, axis=-1):
    mx = jnp.max(s, axis=axis, keepdims=True)
    e = jnp.exp(s - mx)
    return mx, e, jnp.sum(e, axis=axis, keepdims=True)


def _mem_kv_kernel(mem_ref, norm_ref, w_ref, kn_ref, kv_ref, k_ref, v_ref):
    x = mem_ref[0]
    h = x * lax.rsqrt(jnp.mean(x * x, axis=-1, keepdims=True) + EPS) * norm_ref[...]
    kv = _dot(h.astype(BF16), w_ref[...].astype(BF16))
    for c in range(MEM_WIDTH // LANES):
        sl = slice(c * LANES, (c + 1) * LANES)
        kc = _head_rms(kv[:, sl], kn_ref[...])
        kv_ref[0, :, sl] = kc
        k_ref[0, :, sl] = kc.astype(BF16)
    v = kv[:, MEM_WIDTH:]
    kv_ref[0, :, MEM_WIDTH:] = v
    v_ref[0] = v.astype(BF16)


def _mem_kv(mem, mem_norm, w_mem_kv, mem_k_norm):
    b = mem.shape[0]
    return pl.pallas_call(
        _mem_kv_kernel,
        grid=(b,),
        in_specs=[
            pl.BlockSpec((1, N_MEM, D_MODEL), lambda i: (i, 0, 0)),
            pl.BlockSpec((1, D_MODEL), lambda i: (0, 0)),
            pl.BlockSpec((D_MODEL, 2 * MEM_WIDTH), lambda i: (0, 0)),
            pl.BlockSpec((1, LANES), lambda i: (0, 0)),
        ],
        out_specs=[
            pl.BlockSpec((1, N_MEM, 2 * MEM_WIDTH), lambda i: (i, 0, 0)),
            pl.BlockSpec((1, N_MEM, MEM_WIDTH), lambda i: (i, 0, 0)),
            pl.BlockSpec((1, N_MEM, MEM_WIDTH), lambda i: (i, 0, 0)),
        ],
        out_shape=[
            jax.ShapeDtypeStruct((b, N_MEM, 2 * MEM_WIDTH), F32),
            jax.ShapeDtypeStruct((b, N_MEM, MEM_WIDTH), BF16),
            jax.ShapeDtypeStruct((b, N_MEM, MEM_WIDTH), BF16),
        ],
        name="mem_kv",
    )(mem, mem_norm.reshape(1, D_MODEL), w_mem_kv,
      jnp.tile(mem_k_norm, LANES // HEAD_DIM).reshape(1, LANES))


def _project(x_ref, gain_ref, w_ref, h_scr, z_scr):
    x = x_ref[...]
    h = x * lax.rsqrt(jnp.mean(x * x, axis=-1, keepdims=True) + EPS) * gain_ref[...]
    h_scr[...] = h.astype(BF16)
    for lo, width in ((OFF_U, GMLP_WIDTH), (OFF_V, GMLP_WIDTH), (OFF_GA, GMLP_WIDTH),
                      (OFF_Q, ATTN_WIDTH), (OFF_K, ATTN_WIDTH), (OFF_VV, ATTN_WIDTH),
                      (OFF_GB, ATTN_WIDTH), (OFF_QC, MEM_WIDTH), (OFF_GC, MEM_WIDTH)):
        z_scr[:, lo:lo + width] = _dot(h_scr[...], w_ref[:, lo:lo + width])


def _gmlp_chunk(z_scr, rows, lng_ref, lnb_ref, ws_ref, bias_ref):
    gu = _gelu(z_scr[rows, OFF_U:OFF_U + GMLP_WIDTH])
    gv = _gelu(z_scr[rows, OFF_V:OFF_V + GMLP_WIDTH])
    mu = jnp.mean(gv, axis=-1, keepdims=True)
    dv = gv - mu
    var = jnp.mean(dv * dv, axis=-1, keepdims=True)
    vln = dv * lax.rsqrt(var + EPS) * lng_ref[...] + lnb_ref[...]
    lane = _lane_iota((CHUNK, LANES))
    pieces = []
    for p in range(GMLP_WIDTH // LANES):
        sl = slice(p * LANES, (p + 1) * LANES)
        vp = vln[:, sl]
        rhs = jnp.concatenate([jnp.where(lane < HEAD_DIM, vp, 0.0),
                               jnp.where(lane < HEAD_DIM, 0.0, vp)], axis=0).astype(BF16)
        s = _dot(ws_ref[p], rhs) + bias_ref[:, sl]
        ga = z_scr[rows, OFF_GA + p * LANES:OFF_GA + (p + 1) * LANES]
        pieces.append(gu[:, sl] * s * _silu(ga))
    return pieces, vln


def _qk_chunk(z_scr, rows, off, g, norm_ref, cos, sin):
    x = z_scr[rows, off + g * LANES:off + (g + 1) * LANES]
    return _rope(_head_rms(x, norm_ref[...]), cos, sin)


def _stack_heads(x, n_heads):
    head = _lane_iota(x.shape) >> 6
    return jnp.concatenate([jnp.where(head == h, x, 0.0) for h in range(n_heads)], axis=0)


def _unstack_heads(y, n_heads):
    r = y.shape[0] // n_heads
    head = _lane_iota((r, y.shape[1])) >> 6
    out = y[0:r]
    for h in range(1, n_heads):
        out = jnp.where(head == h, y[h * r:(h + 1) * r], out)
    return out


def _mem_attend(qc, mk, mv):
    s = _dot_nt(_stack_heads(qc, MEM_HEADS).astype(BF16), mk) * SCALE
    _, e, l = _softmax_parts(s)
    pv = _dot(e.astype(BF16), mv) * (1.0 / l)
    return _unstack_heads(pv, MEM_HEADS)


def _prompt_proj_kernel(x_ref, gain_ref, w_ref, lng_ref, lnb_ref, qn_ref, kn_ref, mqn_ref,
                        cos_ref, sin_ref, ws_ref, bias_ref, mk_ref, mv_ref,
                        w0_ref, w1_ref, w2_ref, mem_ref, sq_ref, sk_ref, sv_ref, sqc_ref, ssgb_ref, ssgc_ref,
                        mixa_ref, mixc_ref, sgb_ref,
                        q0_ref, k0_ref, v0_ref, q1_ref, k1_ref, v1_ref, q2_ref, k2_ref, v2_ref,
                        t0_ref, t1_ref, t2_ref,
                        o0_ref, o1_ref, o2_ref, smix_ref,
                        h_scr, z_scr, perm_scr, kv16_scr, mem16_scr, *, t_new):
    win_refs, out_refs = (w0_ref, w1_ref, w2_ref), (o0_ref, o1_ref, o2_ref)
    kv16_offs, _ = _window_offsets(win_refs)
    seqs = range(SEQS_PER_STEP)
    groups = range(len(DIL_PAIRS))
    windows = {(bb, g): _SampleWindow(win_refs[g], out_refs[g], kv16_scr, kv16_offs[g],
                                      sq_ref, sk_ref, sv_ref, bb, g, t_new) for bb in seqs for g in groups}
    memories = [_SampleMemory(mem16_scr, sqc_ref, bb, t_new) for bb in seqs]
    sample_pieces = list(windows.values()) + memories

    def sample_finish():
        for bb in seqs:
            outs, lses = zip(*(windows[bb, g].values() for g in groups))
            _sample_finish(outs, lses, memories[bb].values(), ssgb_ref, ssgc_ref, smix_ref, bb, t_new)

    _stage_bf16(win_refs, mem_ref, kv16_scr, mem16_scr)

    x = x_ref[...]
    h = x * lax.rsqrt(jnp.mean(x * x, axis=-1, keepdims=True) + EPS) * gain_ref[...]
    h_scr[...] = h.astype(BF16)

    def project(lo, width):
        z_scr[:, lo:lo + width] = _dot(h_scr[...], w_ref[:, lo:lo + width])

    chunks = [slice(c * CHUNK, (c + 1) * CHUNK) for c in range(ROW_TILE // CHUNK)]
    nat_refs = (q0_ref, k0_ref, v0_ref)

    pm, gm, qk = {}, {}, {}

    def pm_queries():
        qs = []
        for rows in chunks:
            qc = jnp.concatenate(
                [_head_rms(z_scr[rows, OFF_QC + j * LANES:OFF_QC + (j + 1) * LANES], mqn_ref[...])
                 for j in range(MEM_WIDTH // LANES)], axis=1)
            qs.append(_stack_heads(qc, MEM_HEADS).astype(BF16))
        pm["qs"] = jnp.concatenate(qs, axis=0)

    def pm_scores():
        pm["s"] = _dot_nt(pm["qs"], mk_ref[0]) * SCALE

    def pm_softmax():
        _, e, l = _softmax_parts(pm["s"])
        pm["e"], pm["inv_l"] = e.astype(BF16), 1.0 / l

    def pm_values():
        pv = _dot(pm["e"], mv_ref[0]) * pm["inv_l"]
        per = MEM_HEADS * CHUNK
        for c, rows in enumerate(chunks):
            cm = _unstack_heads(pv[c * per:(c + 1) * per], MEM_HEADS)
            mixc_ref[rows, :] = (cm * _silu(z_scr[rows, OFF_GC:OFF_GC + MEM_WIDTH])).astype(BF16)

    def gm_prepare():
        lane = _lane_iota((CHUNK, LANES))
        for c, rows in enumerate(chunks):
            z_scr[rows, OFF_U:OFF_U + GMLP_WIDTH] = _gelu(z_scr[rows, OFF_U:OFF_U + GMLP_WIDTH])
            gv = _gelu(z_scr[rows, OFF_V:OFF_V + GMLP_WIDTH])
            mu = jnp.mean(gv, axis=-1, keepdims=True)
            dv = gv - mu
            var = jnp.mean(dv * dv, axis=-1, keepdims=True)
            vln = dv * lax.rsqrt(var + EPS) * lng_ref[...] + lnb_ref[...]
            for p in range(GMLP_WIDTH // LANES):
                vp = vln[:, p * LANES:(p + 1) * LANES]
                gm[c, p] = jnp.concatenate([jnp.where(lane < HEAD_DIM, vp, 0.0),
                                            jnp.where(lane < HEAD_DIM, 0.0, vp)], axis=0).astype(BF16)

    def gm_spatial():
        for key in list(gm):
            gm[key] = _dot(ws_ref[key[1]], gm[key])

    def gm_gate():
        for c, rows in enumerate(chunks):
            for p in range(GMLP_WIDTH // LANES):
                sl = slice(p * LANES, (p + 1) * LANES)
                s = gm[c, p] + bias_ref[:, sl]
                ga = z_scr[rows, OFF_GA + p * LANES:OFF_GA + (p + 1) * LANES]
                gu = z_scr[rows, OFF_U + p * LANES:OFF_U + (p + 1) * LANES]
                mixa_ref[rows, sl] = (gu * s * _silu(ga)).astype(BF16)

    def qk_sums():
        for c, rows in enumerate(chunks):
            for g in groups:
                for off in (OFF_Q, OFF_K):
                    x = z_scr[rows, off + g * LANES:off + (g + 1) * LANES]
                    qk[c, g, off] = _head_sum(x * x)

    def qk_normed(c, rows, g, off, norm_ref, cos, sin):
        x = z_scr[rows, off + g * LANES:off + (g + 1) * LANES]
        return _rope(x * lax.rsqrt(qk[c, g, off] * (1.0 / HEAD_DIM) + EPS) * norm_ref[...], cos, sin)

    project(OFF_QC, MEM_WIDTH)
    project(OFF_GC, MEM_WIDTH)
    for piece in sample_pieces:
        piece.scores()
    project(OFF_U, GMLP_WIDTH)
    for piece in sample_pieces:
        piece.softmax()
    pm_queries()
    project(OFF_V, GMLP_WIDTH)
    pm_scores()
    project(OFF_GA, GMLP_WIDTH)
    pm_softmax()
    gm_prepare()
    for g in groups:
        windows[0, g].shift()
    project(OFF_Q, ATTN_WIDTH)
    sample_finish()
    pm_values()
    gm_spatial()
    project(OFF_K, ATTN_WIDTH)
    gm_gate()
    project(OFF_VV, ATTN_WIDTH)
    qk_sums()
    project(OFF_GB, ATTN_WIDTH)
    for bb in seqs[1:]:
        for g in groups:
            windows[bb, g].shift()
    for c, rows in enumerate(chunks):
        cos = cos_ref[rows, :]
        sin = sin_ref[rows, :]
        for g in range(len(DIL_PAIRS)):
            q = qk_normed(c, rows, g, OFF_Q, qn_ref, cos, sin) * SCALE
            k = qk_normed(c, rows, g, OFF_K, kn_ref, cos, sin)
            v = z_scr[rows, OFF_VV + g * LANES:OFF_VV + (g + 1) * LANES]
            if g == 0:
                for ref, val in zip(nat_refs, (q, k, v)):
                    ref[0, rows, :] = val.astype(BF16)
                if rows is chunks[-1]:
                    t0_ref[0, :, 0:LANES] = k
                    t0_ref[0, :, LANES:2 * LANES] = v
            else:
                for j, val in enumerate((q, k, v)):
                    perm_scr[3 * (g - 1) + j, rows, :] = val
                tail = (t1_ref, t2_ref)[g - 1]
                tail[0, rows, 0:LANES] = k
                tail[0, rows, LANES:2 * LANES] = v

    for rows in chunks:
        sgb_ref[rows, :] = _silu(z_scr[rows, OFF_GB:OFF_GB + ATTN_WIDTH]).astype(sgb_ref.dtype)

    d1 = DIL_PAIRS[1][1]
    d2 = DIL_PAIRS[2][1]
    for j, ref in enumerate((q1_ref, k1_ref, v1_ref)):
        for r in range(d1):
            ref[0, 0, r, 0] = perm_scr[j, pl.ds(r, ROW_TILE // d1, stride=d1), :].astype(BF16)
    for j, ref in enumerate((q2_ref, k2_ref, v2_ref)):
        for r in range(d2):
            ref[0, 0, r, 0] = perm_scr[3 + j, pl.ds(r, ROW_TILE // d2, stride=d2), :].astype(BF16)


def _sample_proj_kernel(x_ref, gain_ref, w_ref, lng_ref, lnb_ref, qn_ref, kn_ref, mqn_ref,
                        cos_ref, sin_ref, ws_ref, bias_ref,
                        mixa_ref, vln_ref, q_ref, k_ref, v_ref, qc_ref, sgb_ref, sgc_ref,
                        h_scr, z_scr):
    _project(x_ref, gain_ref, w_ref, h_scr, z_scr)

    def chunk(c, carry):
        r0 = pl.multiple_of(c * CHUNK, CHUNK)
        rows = pl.ds(r0, CHUNK)
        pieces, vln = _gmlp_chunk(z_scr, rows, lng_ref, lnb_ref, ws_ref, bias_ref)
        for p, a in enumerate(pieces):
            mixa_ref[rows, p * LANES:(p + 1) * LANES] = a.astype(BF16)
        vln_ref[rows, :] = vln
        cos = cos_ref[rows, :]
        sin = sin_ref[rows, :]
        for g in range(len(DIL_PAIRS)):
            sl = slice(g * LANES, (g + 1) * LANES)
            q_ref[rows, sl] = _qk_chunk(z_scr, rows, OFF_Q, g, qn_ref, cos, sin)
            k_ref[rows, sl] = _qk_chunk(z_scr, rows, OFF_K, g, kn_ref, cos, sin)
            v_ref[rows, sl] = z_scr[rows, OFF_VV + g * LANES:OFF_VV + (g + 1) * LANES]
        sgb_ref[rows, :] = _silu(z_scr[rows, OFF_GB:OFF_GB + ATTN_WIDTH])
        for j in range(MEM_WIDTH // LANES):
            sl = slice(j * LANES, (j + 1) * LANES)
            qc_ref[rows, sl] = _head_rms(z_scr[rows, OFF_QC + j * LANES:OFF_QC + (j + 1) * LANES],
                                         mqn_ref[...])
        sgc_ref[rows, :] = _silu(z_scr[rows, OFF_GC:OFF_GC + MEM_WIDTH])
        return carry

    lax.fori_loop(0, ROW_TILE // CHUNK, chunk, 0)


def _rope_tables(pos):
    inv = ROPE_THETA ** (-jnp.arange(ROT_HALF, dtype=F32) * 2.0 / ROT_DIM)
    lane = np.arange(LANES) % HEAD_DIM
    rot = lane < ROT_DIM
    inv_l = jnp.where(rot, jnp.tile(inv, LANES // ROT_HALF), 0.0)
    sign_l = jnp.asarray(np.where(lane < ROT_HALF, -1.0, 1.0), F32)
    ang = pos.astype(F32)[:, None] * inv_l[None, :]
    return jnp.cos(ang), jnp.sin(ang) * sign_l[None, :]


def _common_proj_operands(norm_gain, w_in, ln_g, ln_b, q_norm, k_norm, mem_q_norm):
    tile2 = lambda g: jnp.tile(g, LANES // HEAD_DIM).reshape(1, LANES)
    return (norm_gain.reshape(1, D_MODEL), w_in.astype(BF16), ln_g.reshape(1, GMLP_WIDTH),
            ln_b.reshape(1, GMLP_WIDTH), tile2(q_norm), tile2(k_norm), tile2(mem_q_norm))


def _common_proj_specs(const):
    return [
        const((1, D_MODEL)), const((D_MODEL, N_IN)), const((1, GMLP_WIDTH)), const((1, GMLP_WIDTH)),
        const((1, LANES)), const((1, LANES)), const((1, LANES)),
    ]


def _pair_ws(ws):
    return jnp.concatenate([ws[0::2], ws[1::2]], axis=2).astype(BF16)


def _prompt_proj(x, common, cos, sin, ws_pair, bias, mem_k, mem_v, wins, mem_s, sample_feats, t_new):
    b, s, _ = x.shape
    n_tiles = s // ROW_TILE
    bd = mem_s.shape[0]
    assert b * n_tiles * SEQS_PER_STEP == bd
    w0, w1, w2 = (w for w, _ in DIL_PAIRS)
    d1, d2 = DIL_PAIRS[1][1], DIL_PAIRS[2][1]
    per1, per2 = w1 // ROW_TILE, w2 // ROW_TILE
    x2 = x.reshape(b * s, D_MODEL)
    const = lambda shape: pl.BlockSpec(shape, lambda i, j: (0,) * len(shape))
    rows = lambda width: pl.BlockSpec((ROW_TILE, width), lambda i, j: (i * n_tiles + j, 0))
    seq_blk = lambda a: pl.BlockSpec((SEQS_PER_STEP,) + a.shape[1:], lambda i, j: (i * n_tiles + j, 0, 0))
    seq_rows = lambda width: pl.BlockSpec((SEQS_PER_STEP * t_new, width), lambda i, j: (i * n_tiles + j, 0))
    in_specs = [rows(D_MODEL)] + _common_proj_specs(const) + [
        pl.BlockSpec((ROW_TILE, LANES), lambda i, j: (j, 0)),
        pl.BlockSpec((ROW_TILE, LANES), lambda i, j: (j, 0)),
        const((GMLP_WIDTH // LANES, CHUNK, 2 * CHUNK)), const((CHUNK, GMLP_WIDTH)),
        pl.BlockSpec((1, N_MEM, MEM_WIDTH), lambda i, j: (i, 0, 0)),
        pl.BlockSpec((1, N_MEM, MEM_WIDTH), lambda i, j: (i, 0, 0)),
    ] + [seq_blk(w) for w in wins] + [seq_blk(mem_s)] + [seq_rows(f.shape[1]) for f in sample_feats]
    nat = pl.BlockSpec((1, ROW_TILE, LANES), lambda i, j: (i, j, 0))
    g1 = pl.BlockSpec((1, 1, d1, 1, ROW_TILE // d1, LANES), lambda i, j: (i, j // per1, 0, j % per1, 0, 0))
    g2 = pl.BlockSpec((1, 1, d2, 1, ROW_TILE // d2, LANES), lambda i, j: (i, j // per2, 0, j % per2, 0, 0))
    last = lambda n_blk: (lambda i, j: (i, jnp.maximum(j - (n_tiles - n_blk), 0), 0))
    t0 = pl.BlockSpec((1, w0, 2 * LANES), last(1))
    t1 = pl.BlockSpec((1, ROW_TILE, 2 * LANES), last(per1))
    t2 = pl.BlockSpec((1, ROW_TILE, 2 * LANES), last(per2))
    out_specs = [rows(GMLP_WIDTH), rows(MEM_WIDTH), rows(ATTN_WIDTH),
                 nat, nat, nat, g1, g1, g1, g2, g2, g2, t0, t1, t2] \
        + [seq_blk(w) for w in wins] + [seq_rows(ATTN_WIDTH + MEM_WIDTH)]
    nat_s = jax.ShapeDtypeStruct((b, s, LANES), BF16)
    g1_s = jax.ShapeDtypeStruct((b, s // w1, d1, per1, ROW_TILE // d1, LANES), BF16)
    g2_s = jax.ShapeDtypeStruct((b, s // w2, d2, per2, ROW_TILE // d2, LANES), BF16)
    out_shape = [
        jax.ShapeDtypeStruct((b * s, GMLP_WIDTH), BF16),
        jax.ShapeDtypeStruct((b * s, MEM_WIDTH), BF16),
        jax.ShapeDtypeStruct((b * s, ATTN_WIDTH), BF16),
        nat_s, nat_s, nat_s, g1_s, g1_s, g1_s, g2_s, g2_s, g2_s,
        jax.ShapeDtypeStruct((b, w0, 2 * LANES), F32),
        jax.ShapeDtypeStruct((b, w1, 2 * LANES), F32),
        jax.ShapeDtypeStruct((b, w2, 2 * LANES), F32),
    ] + [jax.ShapeDtypeStruct(w.shape, F32) for w in wins] \
      + [jax.ShapeDtypeStruct((bd * t_new, ATTN_WIDTH + MEM_WIDTH), F32)]
    return pl.pallas_call(
        functools.partial(_prompt_proj_kernel, t_new=t_new),
        grid=(b, n_tiles),
        in_specs=in_specs,
        out_specs=out_specs,
        out_shape=out_shape,
        scratch_shapes=[
            pltpu.VMEM((ROW_TILE, D_MODEL), BF16),
            pltpu.VMEM((ROW_TILE, N_IN), F32),
            pltpu.VMEM((6, ROW_TILE, LANES), F32),
            pltpu.VMEM((SEQS_PER_STEP, 2 * LANES, sum(w.shape[2] for w in wins)), BF16),
            pltpu.VMEM((SEQS_PER_STEP,) + mem_s.shape[1:], BF16),
        ],
        compiler_params=pltpu.CompilerParams(
            dimension_semantics=("arbitrary", "arbitrary"), vmem_limit_bytes=VMEM_LIMIT),
        name="prompt_proj",
    )(x2, *common, cos, sin, ws_pair, bias, mem_k, mem_v, *wins, mem_s, *sample_feats)


def _sample_proj(x, common, cos, sin, ws_pair, bias):
    n = x.shape[0]
    const = lambda shape: pl.BlockSpec(shape, lambda i: (0,) * len(shape))
    rows = lambda width: pl.BlockSpec((ROW_TILE, width), lambda i: (i, 0))
    in_specs = [rows(D_MODEL)] + _common_proj_specs(const) + [
        const((ROW_TILE, LANES)), const((ROW_TILE, LANES)),
        const((GMLP_WIDTH // LANES, CHUNK, 2 * CHUNK)), const((CHUNK, GMLP_WIDTH)),
    ]
    widths = (GMLP_WIDTH, GMLP_WIDTH, ATTN_WIDTH, ATTN_WIDTH, ATTN_WIDTH, MEM_WIDTH, ATTN_WIDTH, MEM_WIDTH)
    dtypes = (BF16,) + (F32,) * 7
    return pl.pallas_call(
        _sample_proj_kernel,
        grid=(n // ROW_TILE,),
        in_specs=in_specs,
        out_specs=[rows(w) for w in widths],
        out_shape=[jax.ShapeDtypeStruct((n, w), dt) for w, dt in zip(widths, dtypes)],
        scratch_shapes=[
            pltpu.VMEM((ROW_TILE, D_MODEL), BF16),
            pltpu.VMEM((ROW_TILE, N_IN), F32),
        ],
        compiler_params=pltpu.CompilerParams(
            dimension_semantics=("arbitrary",), vmem_limit_bytes=VMEM_LIMIT),
        name="sample_proj",
    )(x, *common, cos, sin, ws_pair, bias)


def _window_unit(q, kk, vv, bias):
    n = q.shape[0]
    lane_lo = _lane_iota(q.shape) < HEAD_DIM
    zero = jnp.zeros_like(q)
    qs = jnp.concatenate([jnp.where(lane_lo, q, zero), jnp.where(lane_lo, zero, q)], axis=0)
    d = _dot_nt(qs, kk)
    parts = [_softmax_parts(d[h * n:(h + 1) * n] + bias) for h in range(2)]
    e = jnp.concatenate([p[1].astype(BF16) for p in parts], axis=0)
    pv = _dot(e, vv)
    o = jnp.where(lane_lo, pv[:n] * (1.0 / parts[0][2]), pv[n:] * (1.0 / parts[1][2]))
    lse = [p[0] + jnp.log(p[2]) for p in parts]
    lse_b = jnp.where(lane_lo, jnp.broadcast_to(lse[0], q.shape), jnp.broadcast_to(lse[1], q.shape))
    return o, lse_b


def _prompt_attn_kernel(q0_ref, k0_ref, v0_ref, k0p_ref, v0p_ref,
                        q1_ref, k1_ref, v1_ref, k1p_ref, v1p_ref,
                        q2_ref, k2_ref, v2_ref, k2p_ref, v2p_ref,
                        sgb_ref, mix_ref, o_scr, lse_scr, bias_scr):
    m = pl.program_id(1)
    blk = CHUNK
    qo = lax.broadcasted_iota(jnp.int32, (blk, 2 * blk), 0)
    ko = lax.broadcasted_iota(jnp.int32, (blk, 2 * blk), 1)
    band = (ko >= qo) & (ko <= qo + blk)
    bias_scr[0] = jnp.where(band, 0.0, NEG)
    bias_scr[1] = jnp.where(band & (ko >= blk * (m == 0).astype(jnp.int32)), 0.0, NEG)
    band, first = 0, 1

    d1, d2 = DIL_PAIRS[1][1], DIL_PAIRS[2][1]
    w1 = DIL_PAIRS[1][0]
    for b0 in range(ATTN_TILE // blk):
        q = q0_ref[0, b0 * blk:(b0 + 1) * blk, :]
        if b0 == 0:
            kk = jnp.concatenate([k0p_ref[0], k0_ref[0, 0:blk, :]], axis=0)
            vv = jnp.concatenate([v0p_ref[0], v0_ref[0, 0:blk, :]], axis=0)
            valid = first
        else:
            kk = k0_ref[0, (b0 - 1) * blk:(b0 + 1) * blk, :]
            vv = v0_ref[0, (b0 - 1) * blk:(b0 + 1) * blk, :]
            valid = band
        o, lse = _window_unit(q, kk, vv, bias_scr[valid])
        o_scr[0, b0 * blk:(b0 + 1) * blk, :] = o
        lse_scr[0, b0 * blk:(b0 + 1) * blk, :] = lse
    for nn in range(ATTN_TILE // w1):
        for r in range(d1):
            blk_of = lambda ref, n: ref[0, n, r].reshape(blk, LANES)
            q = blk_of(q1_ref, nn)
            if nn == 0:
                kk = jnp.concatenate([blk_of(k1p_ref, 0), blk_of(k1_ref, 0)], axis=0)
                vv = jnp.concatenate([blk_of(v1p_ref, 0), blk_of(v1_ref, 0)], axis=0)
                valid = first
            else:
                kk = jnp.concatenate([blk_of(k1_ref, nn - 1), blk_of(k1_ref, nn)], axis=0)
                vv = jnp.concatenate([blk_of(v1_ref, nn - 1), blk_of(v1_ref, nn)], axis=0)
                valid = band
            o, lse = _window_unit(q, kk, vv, bias_scr[valid])
            dst = pl.ds(nn * w1 + r, blk, stride=d1)
            o_scr[1, dst, :] = o
            lse_scr[1, dst, :] = lse
    for r in range(d2):
        q = q2_ref[0, 0, r].reshape(blk, LANES)
        kk = jnp.concatenate([k2p_ref[0, 0, r].reshape(blk, LANES), k2_ref[0, 0, r].reshape(blk, LANES)], axis=0)
        vv = jnp.concatenate([v2p_ref[0, 0, r].reshape(blk, LANES), v2_ref[0, 0, r].reshape(blk, LANES)], axis=0)
        o, lse = _window_unit(q, kk, vv, bias_scr[first])
        dst = pl.ds(r, blk, stride=d2)
        o_scr[2, dst, :] = o
        lse_scr[2, dst, :] = lse

    def combine(c, carry):
        rows = pl.ds(pl.multiple_of(c * blk, blk), blk)
        ls = [lse_scr[g, rows, :] for g in range(3)]
        mx = jnp.maximum(jnp.maximum(ls[0], ls[1]), ls[2])
        es = [jnp.exp(x - mx) for x in ls]
        inv = 1.0 / (es[0] + es[1] + es[2])
        for g in range(3):
            sl = slice(g * LANES, (g + 1) * LANES)
            mix_ref[rows, sl] = (o_scr[g, rows, :] * (es[g] * inv) * sgb_ref[rows, sl]).astype(BF16)
        return carry

    lax.fori_loop(0, ATTN_TILE // blk, combine, 0)


def _prompt_attn(feats, sgb, b, s):
    q0, k0, v0, q1, k1, v1, q2, k2, v2 = feats
    w1, d1 = DIL_PAIRS[1]
    w2, d2 = DIL_PAIRS[2]
    n_tiles = s // ATTN_TILE
    per1 = ATTN_TILE // w1
    prev = lambda m, k: jnp.maximum(m * k - 1, 0)
    nat = pl.BlockSpec((1, ATTN_TILE, LANES), lambda i, m: (i, m, 0))
    nat_p = pl.BlockSpec((1, CHUNK, LANES), lambda i, m: (i, prev(m, ATTN_TILE // CHUNK), 0))
    g1 = pl.BlockSpec((1, per1) + q1.shape[2:], lambda i, m: (i, m, 0, 0, 0, 0))
    g1_p = pl.BlockSpec((1, 1) + q1.shape[2:], lambda i, m: (i, prev(m, per1), 0, 0, 0, 0))
    g2 = pl.BlockSpec((1, 1) + q2.shape[2:], lambda i, m: (i, m, 0, 0, 0, 0))
    g2_p = pl.BlockSpec((1, 1) + q2.shape[2:], lambda i, m: (i, prev(m, 1), 0, 0, 0, 0))
    rows = pl.BlockSpec((ATTN_TILE, ATTN_WIDTH), lambda i, m: (i * n_tiles + m, 0))
    return pl.pallas_call(
        _prompt_attn_kernel,
        grid=(b, n_tiles),
        in_specs=[nat, nat, nat, nat_p, nat_p, g1, g1, g1, g1_p, g1_p, g2, g2, g2, g2_p, g2_p, rows],
        out_specs=rows,
        out_shape=jax.ShapeDtypeStruct((b * s, ATTN_WIDTH), BF16),
        scratch_shapes=[
            pltpu.VMEM((3, ATTN_TILE, LANES), F32),
            pltpu.VMEM((3, ATTN_TILE, LANES), F32),
            pltpu.VMEM((2, CHUNK, 2 * CHUNK), F32),
        ],
        compiler_params=pltpu.CompilerParams(
            dimension_semantics=("arbitrary", "arbitrary"), vmem_limit_bytes=VMEM_LIMIT),
        name="prompt_attn",
    )(q0, k0, v0, k0, v0, q1, k1, v1, k1, v1, q2, k2, v2, k2, v2, sgb)


def _window_offsets(win_refs):
    offs, total = [], 0
    for r in win_refs:
        offs.append(total)
        total += r.shape[2]
    return offs, total


def _stage_bf16(win_refs, mem_ref, kv16_ref, mem16_ref):
    offs, _ = _window_offsets(win_refs)
    for bb in range(mem_ref.shape[0]):
        for wref, off in zip(win_refs, offs):
            kv16_ref[bb, :, off:off + wref.shape[2]] = wref[bb].astype(BF16)
        mem16_ref[bb] = mem_ref[bb].astype(BF16)


class _SampleWindow:
    def __init__(self, wref, oref, kv16_ref, kv16_off, q_ref, k_ref, v_ref, bb, g, t_new):
        self.wref, self.oref, self.bb, self.t_new = wref, oref, bb, t_new
        self.window, self.dil = DIL_PAIRS[g]
        self.length = wref.shape[2]
        rows = slice(bb * t_new, (bb + 1) * t_new)
        sl = slice(g * LANES, (g + 1) * LANES)
        self.load_new = lambda: (q_ref[rows, sl], k_ref[rows, sl], v_ref[rows, sl])
        self.kv16 = lambda half: kv16_ref[bb, half * LANES:(half + 1) * LANES, kv16_off:kv16_off + self.length]

    def scores(self):
        t_new = self.t_new
        qg, kg, vg = self.load_new()
        self.lane_lo = _lane_iota(qg.shape) < HEAD_DIM
        qs = jnp.concatenate([jnp.where(self.lane_lo, qg, 0.0), jnp.where(self.lane_lo, 0.0, qg)],
                             axis=0).astype(BF16)
        zpad = jnp.zeros((CHUNK - t_new, LANES), F32)
        self.k_new = jnp.concatenate([kg, zpad], axis=0)
        self.v_new = jnp.concatenate([vg, zpad], axis=0)
        self.s_old = _dot(qs, self.kv16(0)) * SCALE
        self.s_new = _dot_nt(qs, self.k_new.astype(BF16)) * SCALE

    def softmax(self):
        t_new, dil, window = self.t_new, self.dil, self.window
        s_old, s_new = self.s_old, self.s_new
        tq = lax.broadcasted_iota(jnp.int32, s_old.shape, 0) & (t_new - 1)
        dist = self.length + tq - lax.broadcasted_iota(jnp.int32, s_old.shape, 1)
        ok_old = ((dist & (dil - 1)) == 0) & (dist <= window)
        tqn = lax.broadcasted_iota(jnp.int32, s_new.shape, 0) & (t_new - 1)
        dn = tqn - lax.broadcasted_iota(jnp.int32, s_new.shape, 1)
        ok_new = (dn >= 0) & ((dn & (dil - 1)) == 0) & (dn <= window)
        s_old = jnp.where(ok_old, s_old, NEG)
        s_new = jnp.where(ok_new, s_new, NEG)
        mx = jnp.maximum(jnp.max(s_old, axis=-1, keepdims=True), jnp.max(s_new, axis=-1, keepdims=True))
        e_old = jnp.exp(s_old - mx)
        e_new = jnp.exp(s_new - mx)
        l = jnp.sum(e_old, axis=-1, keepdims=True) + jnp.sum(e_new, axis=-1, keepdims=True)
        self.e_old, self.e_new = e_old.astype(BF16), e_new.astype(BF16)
        self.inv_l = 1.0 / l
        self.lse = mx + jnp.log(l)

    def values(self):
        t_new, lane_lo = self.t_new, self.lane_lo
        pv = (_dot_nt(self.e_old, self.kv16(1)) + _dot(self.e_new, self.v_new.astype(BF16))) * self.inv_l
        out = jnp.where(lane_lo, pv[:t_new], pv[t_new:])
        shape = (t_new, LANES)
        lse_b = jnp.where(lane_lo, jnp.broadcast_to(self.lse[:t_new], shape),
                          jnp.broadcast_to(self.lse[t_new:], shape))
        return out, lse_b

    def shift(self):
        t_new, wref, oref, bb = self.t_new, self.wref, self.oref, self.bb
        _, kg, vg = self.load_new()
        zpad = jnp.zeros((CHUNK - t_new, LANES), F32)
        new_t = jnp.concatenate([jnp.concatenate([kg, zpad], axis=0).T,
                                 jnp.concatenate([vg, zpad], axis=0).T], axis=0)
        keep = _lane_iota((2 * LANES, LANES)) < LANES - t_new
        nxt = pltpu.roll(wref[bb, :, 0:LANES], LANES - t_new, 1)
        n_blk = self.length // LANES
        for c in range(n_blk):
            cur = nxt
            if c + 1 < n_blk:
                nxt = pltpu.roll(wref[bb, :, (c + 1) * LANES:(c + 2) * LANES], LANES - t_new, 1)
            else:
                nxt = pltpu.roll(new_t, LANES - t_new, 1)
            oref[bb, :, c * LANES:(c + 1) * LANES] = jnp.where(keep, cur, nxt)


class _SampleMemory:
    def __init__(self, mem16_ref, qc_ref, bb, t_new):
        self.rows = slice(bb * t_new, (bb + 1) * t_new)
        self.qc_ref = qc_ref
        self.half = lambda h: mem16_ref[bb, h * MEM_WIDTH:(h + 1) * MEM_WIDTH, :]

    def scores(self):
        self.s = _dot(_stack_heads(self.qc_ref[self.rows, :], MEM_HEADS).astype(BF16), self.half(0)) * SCALE

    def softmax(self):
        _, e, l = _softmax_parts(self.s)
        self.e, self.inv_l = e.astype(BF16), 1.0 / l

    def values(self):
        return _unstack_heads(_dot_nt(self.e, self.half(1)) * self.inv_l, MEM_HEADS)


def _sample_finish(outs, lses, cm, sgb_ref, sgc_ref, mix_ref, bb, t_new):
    rows = slice(bb * t_new, (bb + 1) * t_new)
    mx = jnp.maximum(jnp.maximum(lses[0], lses[1]), lses[2])
    es = [jnp.exp(x - mx) for x in lses]
    inv = 1.0 / (es[0] + es[1] + es[2])
    for g in range(len(DIL_PAIRS)):
        sl = slice(g * LANES, (g + 1) * LANES)
        mix_ref[rows, sl] = outs[g] * (es[g] * inv) * sgb_ref[rows, sl]
    mix_ref[rows, ATTN_WIDTH:ATTN_WIDTH + MEM_WIDTH] = cm * sgc_ref[rows, :]


def _out_proj_kernel(x_ref, *refs):
    *mix_refs, w_ref, y_ref = refs
    mix = jnp.concatenate([r[...].astype(BF16) for r in mix_refs], axis=1)
    y_ref[...] = x_ref[...] + _dot(mix, w_ref[...].astype(BF16))


def _out_proj(x, mixes, w_out):
    n = x.shape[0]
    rows = lambda width: pl.BlockSpec((OUT_TILE, width), lambda i: (i, 0))
    return pl.pallas_call(
        _out_proj_kernel,
        grid=(n // OUT_TILE,),
        in_specs=[rows(D_MODEL)] + [rows(mx.shape[1]) for mx in mixes]
        + [pl.BlockSpec((MIX_WIDTH, D_MODEL), lambda i: (0, 0))],
        out_specs=rows(D_MODEL),
        out_shape=jax.ShapeDtypeStruct((n, D_MODEL), F32),
        compiler_params=pltpu.CompilerParams(
            dimension_semantics=("arbitrary",), vmem_limit_bytes=VMEM_LIMIT),
        name="out_proj",
    )(x, *mixes, w_out)


def kernel(x_prompt, x_sample, state_win0_kv, state_win1_kv, state_win2_kv, cache_mem_kv, mem_prompt, norm_gain, w_in, gmlp_ln_gain, gmlp_ln_bias, gmlp_w_s, gmlp_b_s, attn_q_norm, attn_k_norm, mem_norm, w_mem_kv, mem_q_norm, mem_k_norm, w_out):
    depth = norm_gain.shape[0]
    assert depth == 1, "single-layer step only"
    bp, s, _ = x_prompt.shape
    bd, t, _ = x_sample.shape
    past_len = PAST_LEN
    assert s % ATTN_TILE == 0 and (bd * t) % OUT_TILE == 0 and (bp * s) % OUT_TILE == 0
    assert CHUNK % t == 0 and t & (t - 1) == 0 and DIL_PAIRS[1][0] % ROW_TILE == 0
    states = (state_win0_kv, state_win1_kv, state_win2_kv)
    for st, (w, d) in zip(states, DIL_PAIRS):
        assert st.shape[2] == w and w // d == CHUNK and d & (d - 1) == 0

    common = _common_proj_operands(norm_gain[0], w_in[0], gmlp_ln_gain[0], gmlp_ln_bias[0],
                                   attn_q_norm[0], attn_k_norm[0], mem_q_norm[0])
    tril = jnp.tril(jnp.ones((CHUNK, CHUNK), dtype=bool))
    ws = jnp.where(tril[None], gmlp_w_s[0], 0.0)
    b_s = gmlp_b_s[0]

    reps = CHUNK // t
    ws_s = jnp.stack([jnp.kron(jnp.eye(reps, dtype=F32), ws[g, :t, :t]) for g in range(N_GMLP_GROUPS)])
    bias_s = jnp.repeat(jnp.tile(b_s[:, :t], (1, reps)).T, HEAD_DIM, axis=1)
    cos_s, sin_s = _rope_tables(past_len + (jnp.arange(ROW_TILE, dtype=jnp.int32) % t))
    mixa_s, vln_s, q_s, k_s, v_s, qc_s, sgb_s, sgc_s = _sample_proj(
        x_sample.reshape(bd * t, D_MODEL), common, cos_s, sin_s, _pair_ws(ws_s), bias_s)
    feat_major = lambda a: jnp.transpose(a, (0, 2, 3, 4, 1)).reshape(a.shape[0], -1, a.shape[1])
    wins = [feat_major(st[0]) for st in states]
    mem_s = feat_major(cache_mem_kv[0])

    mem_kv_p, mem_k, mem_v = _mem_kv(mem_prompt, mem_norm[0], w_mem_kv[0], mem_k_norm[0])
    cos_p, sin_p = _rope_tables(jnp.arange(s, dtype=jnp.int32))
    bias_p = jnp.repeat(b_s.T, HEAD_DIM, axis=1)
    outs = _prompt_proj(x_prompt, common, cos_p, sin_p, _pair_ws(ws), bias_p, mem_k, mem_v,
                        wins, mem_s, (q_s, k_s, v_s, qc_s, sgb_s, sgc_s), t)
    mixa_p, mixc_p, sgb_p = outs[0:3]
    tails = outs[12:15]
    new_wins_t = outs[15:19]
    mixb_p = _prompt_attn(outs[3:12], sgb_p, bp, s)
    y_prompt = _out_proj(x_prompt.reshape(bp * s, D_MODEL), (mixa_p, mixb_p, mixc_p), w_out[0])

    mixbc_s = new_wins_t[3]
    heads = LANES // HEAD_DIM
    nw0, nw1, nw2 = [
        jnp.transpose(w.reshape(bd, 2, heads, HEAD_DIM, w.shape[2]), (0, 4, 1, 2, 3)) for w in new_wins_t[:3]]
    y_sample = _out_proj(x_sample.reshape(bd * t, D_MODEL), (mixa_s, mixbc_s), w_out[0])

    win_shape = lambda n, w: (1, n, w, 2, heads, HEAD_DIM)
    return (
        y_prompt.reshape(bp, s, D_MODEL),
        y_sample.reshape(bd, t, D_MODEL),
        tails[0].reshape(win_shape(bp, DIL_PAIRS[0][0])),
        tails[1].reshape(win_shape(bp, DIL_PAIRS[1][0])),
        tails[2].reshape(win_shape(bp, DIL_PAIRS[2][0])),
        mem_kv_p.reshape(1, bp, N_MEM, 2, MEM_HEADS, HEAD_DIM),
        nw0.reshape(win_shape(bd, DIL_PAIRS[0][0])),
        nw1.reshape(win_shape(bd, DIL_PAIRS[1][0])),
        nw2.reshape(win_shape(bd, DIL_PAIRS[2][0])),
        vln_s.reshape(1, bd, t, N_GMLP_GROUPS, HEAD_DIM),
    )
```

```python
import functools

import numpy as np
import jax
import jax.numpy as jnp
from jax import lax
from jax.experimental import pallas as pl
from jax.experimental.pallas import tpu as pltpu

F32 = jnp.float32
BF16 = jnp.bfloat16

D_MODEL = 1024
HEAD_DIM = 64
LANES = 128
N_GMLP_GROUPS = 6
GMLP_WIDTH = N_GMLP_GROUPS * HEAD_DIM
CHUNK = 128
DIL_PAIRS = ((128, 1), (512, 4), (2048, 16))
ATTN_WIDTH = len(DIL_PAIRS) * LANES
N_MEM = 256
PAST_LEN = 8192
MEM_HEADS = 4
MEM_WIDTH = MEM_HEADS * HEAD_DIM
MIX_WIDTH = GMLP_WIDTH + ATTN_WIDTH + MEM_WIDTH
N_IN = 3 * GMLP_WIDTH + 4 * ATTN_WIDTH + 2 * MEM_WIDTH
ROPE_THETA = 500000.0
ROT_DIM = HEAD_DIM // 4
ROT_HALF = ROT_DIM // 2
EPS = 1e-6
NEG = -1e30
SCALE = HEAD_DIM ** -0.5
assert np.frexp(SCALE)[0] == 0.5, "the prompt path folds SCALE into q, exact only for a power of two"

OFF_U = 0
OFF_V = OFF_U + GMLP_WIDTH
OFF_GA = OFF_V + GMLP_WIDTH
OFF_Q = OFF_GA + GMLP_WIDTH
OFF_K = OFF_Q + ATTN_WIDTH
OFF_VV = OFF_K + ATTN_WIDTH
OFF_GB = OFF_VV + ATTN_WIDTH
OFF_QC = OFF_GB + ATTN_WIDTH
OFF_GC = OFF_QC + MEM_WIDTH

ROW_TILE = 256
ATTN_TILE = 2048
ATTN_SUB = 512
SEQS_PER_STEP = 2
OUT_TILE = 1024
VMEM_LIMIT = 56 * 1024 * 1024


def _dot(a, b):
    return jnp.dot(a, b, preferred_element_type=F32)


def _dot_nt(a, b):
    return lax.dot_general(a, b, (((1,), (1,)), ((), ())), preferred_element_type=F32)


def _silu(x):
    return x * (1.0 / (1.0 + jnp.exp(-x)))


def _gelu(x):
    return 0.5 * x * (1.0 + lax.erf(x * np.float32(np.sqrt(0.5))))


def _lane_iota(shape):
    return lax.broadcasted_iota(jnp.int32, shape, len(shape) - 1)


def _head_sum(ss):
    r = (lax.broadcasted_iota(jnp.int32, (2 * LANES, LANES), 0) >> 6) & 1
    c = lax.broadcasted_iota(jnp.int32, (2 * LANES, LANES), 1) >> 6
    ones_blk = jnp.where(r == c, 1.0, 0.0).astype(BF16)
    hi = ss.astype(BF16)
    lo = (ss - hi.astype(F32)).astype(BF16)
    return _dot(jnp.concatenate([hi, lo], axis=1), ones_blk)


def _head_rms(x, gain):
    ms = _head_sum(x * x) * (1.0 / HEAD_DIM)
    return x * lax.rsqrt(ms + EPS) * gain


def _rope(x, cos, sin):
    lane = _lane_iota(x.shape) & (HEAD_DIM - 1)
    partner = jnp.where(lane < ROT_HALF, pltpu.roll(x, LANES - ROT_HALF, 1), pltpu.roll(x, ROT_HALF, 1))
    return x * cos + partner * sin


def _softmax_parts(s, axis=-1):
    mx = jnp.max(s, axis=axis, keepdims=True)
    e = jnp.exp(s - mx)
    return mx, e, jnp.sum(e, axis=axis, keepdims=True)


def _mem_kv_kernel(mem_ref, norm_ref, w_ref, kn_ref, kv_ref, k_ref, v_ref):
    x = mem_ref[0]
    h = x * lax.rsqrt(jnp.mean(x * x, axis=-1, keepdims=True) + EPS) * norm_ref[...]
    kv = _dot(h.astype(BF16), w_ref[...].astype(BF16))
    for c in range(MEM_WIDTH // LANES):
        sl = slice(c * LANES, (c + 1) * LANES)
        kc = _head_rms(kv[:, sl], kn_ref[...])
        kv_ref[0, :, sl] = kc
        k_ref[0, :, sl] = kc.astype(BF16)
    v = kv[:, MEM_WIDTH:]
    kv_ref[0, :, MEM_WIDTH:] = v
    v_ref[0] = v.astype(BF16)


def _mem_kv(mem, mem_norm, w_mem_kv, mem_k_norm):
    b = mem.shape[0]
    return pl.pallas_call(
        _mem_kv_kernel,
        grid=(b,),
        in_specs=[
            pl.BlockSpec((1, N_MEM, D_MODEL), lambda i: (i, 0, 0)),
            pl.BlockSpec((1, D_MODEL), lambda i: (0, 0)),
            pl.BlockSpec((D_MODEL, 2 * MEM_WIDTH), lambda i: (0, 0)),
            pl.BlockSpec((1, LANES), lambda i: (0, 0)),
        ],
        out_specs=[
            pl.BlockSpec((1, N_MEM, 2 * MEM_WIDTH), lambda i: (i, 0, 0)),
            pl.BlockSpec((1, N_MEM, MEM_WIDTH), lambda i: (i, 0, 0)),
            pl.BlockSpec((1, N_MEM, MEM_WIDTH), lambda i: (i, 0, 0)),
        ],
        out_shape=[
            jax.ShapeDtypeStruct((b, N_MEM, 2 * MEM_WIDTH), F32),
            jax.ShapeDtypeStruct((b, N_MEM, MEM_WIDTH), BF16),
            jax.ShapeDtypeStruct((b, N_MEM, MEM_WIDTH), BF16),
        ],
        name="mem_kv",
    )(mem, mem_norm.reshape(1, D_MODEL), w_mem_kv,
      jnp.tile(mem_k_norm, LANES // HEAD_DIM).reshape(1, LANES))


def _project(x_ref, gain_ref, w_ref, h_scr, z_scr):
    x = x_ref[...]
    h = x * lax.rsqrt(jnp.mean(x * x, axis=-1, keepdims=True) + EPS) * gain_ref[...]
    h_scr[...] = h.astype(BF16)
    for lo, width in ((OFF_U, GMLP_WIDTH), (OFF_V, GMLP_WIDTH), (OFF_GA, GMLP_WIDTH),
                      (OFF_Q, ATTN_WIDTH), (OFF_K, ATTN_WIDTH), (OFF_VV, ATTN_WIDTH),
                      (OFF_GB, ATTN_WIDTH), (OFF_QC, MEM_WIDTH), (OFF_GC, MEM_WIDTH)):
        z_scr[:, lo:lo + width] = _dot(h_scr[...], w_ref[:, lo:lo + width])


def _gmlp_chunk(z_scr, rows, lng_ref, lnb_ref, ws_ref, bias_ref):
    gu = _gelu(z_scr[rows, OFF_U:OFF_U + GMLP_WIDTH])
    gv = _gelu(z_scr[rows, OFF_V:OFF_V + GMLP_WIDTH])
    mu = jnp.mean(gv, axis=-1, keepdims=True)
    dv = gv - mu
    var = jnp.mean(dv * dv, axis=-1, keepdims=True)
    vln = dv * lax.rsqrt(var + EPS) * lng_ref[...] + lnb_ref[...]
    lane = _lane_iota((CHUNK, LANES))
    pieces = []
    for p in range(GMLP_WIDTH // LANES):
        sl = slice(p * LANES, (p + 1) * LANES)
        vp = vln[:, sl]
        rhs = jnp.concatenate([jnp.where(lane < HEAD_DIM, vp, 0.0),
                               jnp.where(lane < HEAD_DIM, 0.0, vp)], axis=0).astype(BF16)
        s = _dot(ws_ref[p], rhs) + bias_ref[:, sl]
        ga = z_scr[rows, OFF_GA + p * LANES:OFF_GA + (p + 1) * LANES]
        pieces.append(gu[:, sl] * s * _silu(ga))
    return pieces, vln


def _qk_chunk(z_scr, rows, off, g, norm_ref, cos, sin):
    x = z_scr[rows, off + g * LANES:off + (g + 1) * LANES]
    return _rope(_head_rms(x, norm_ref[...]), cos, sin)


def _stack_heads(x, n_heads):
    head = _lane_iota(x.shape) >> 6
    return jnp.concatenate([jnp.where(head == h, x, 0.0) for h in range(n_heads)], axis=0)


def _unstack_heads(y, n_heads):
    r = y.shape[0] // n_heads
    head = _lane_iota((r, y.shape[1])) >> 6
    out = y[0:r]
    for h in range(1, n_heads):
        out = jnp.where(head == h, y[h * r:(h + 1) * r], out)
    return out


def _mem_attend(qc, mk, mv):
    s = _dot_nt(_stack_heads(qc, MEM_HEADS).astype(BF16), mk) * SCALE
    _, e, l = _softmax_parts(s)
    pv = _dot(e.astype(BF16), mv) * (1.0 / l)
    return _unstack_heads(pv, MEM_HEADS)


def _prompt_proj_kernel(x_ref, gain_ref, w_ref, lng_ref, lnb_ref, qn_ref, kn_ref, mqn_ref,
                        cos_ref, sin_ref, ws_ref, bias_ref, mk_ref, mv_ref,
                        w0_ref, w1_ref, w2_ref, mem_ref, sq_ref, sk_ref, sv_ref, sqc_ref, ssgb_ref, ssgc_ref,
                        mixa_ref, mixc_ref, sgb_ref,
                        q0_ref, k0_ref, v0_ref, q1_ref, k1_ref, v1_ref, q2_ref, k2_ref, v2_ref,
                        t0_ref, t1_ref, t2_ref,
                        o0_ref, o1_ref, o2_ref, smix_ref,
                        h_scr, z_scr, perm_scr, kv16_scr, mem16_scr, *, t_new):
    win_refs, out_refs = (w0_ref, w1_ref, w2_ref), (o0_ref, o1_ref, o2_ref)
    kv16_offs, _ = _window_offsets(win_refs)
    seqs = range(SEQS_PER_STEP)
    groups = range(len(DIL_PAIRS))
    windows = {(bb, g): _SampleWindow(win_refs[g], out_refs[g], kv16_scr, kv16_offs[g],
                                      sq_ref, sk_ref, sv_ref, bb, g, t_new) for bb in seqs for g in groups}
    memories = [_SampleMemory(mem16_scr, sqc_ref, bb, t_new) for bb in seqs]
    sample_pieces = list(windows.values()) + memories

    def sample_finish():
        for bb in seqs:
            outs, lses = zip(*(windows[bb, g].values() for g in groups))
            _sample_finish(outs, lses, memories[bb].values(), ssgb_ref, ssgc_ref, smix_ref, bb, t_new)

    _stage_bf16(win_refs, mem_ref, kv16_scr, mem16_scr)

    x = x_ref[...]
    h = x * lax.rsqrt(jnp.mean(x * x, axis=-1, keepdims=True) + EPS) * gain_ref[...]
    h_scr[...] = h.astype(BF16)

    def project(lo, width):
        z_scr[:, lo:lo + width] = _dot(h_scr[...], w_ref[:, lo:lo + width])

    chunks = [slice(c * CHUNK, (c + 1) * CHUNK) for c in range(ROW_TILE // CHUNK)]
    nat_refs = (q0_ref, k0_ref, v0_ref)

    pm, gm, qk = {}, {}, {}

    def pm_queries():
        qs = []
        for rows in chunks:
            qc = jnp.concatenate(
                [_head_rms(z_scr[rows, OFF_QC + j * LANES:OFF_QC + (j + 1) * LANES], mqn_ref[...])
                 for j in range(MEM_WIDTH // LANES)], axis=1)
            qs.append(_stack_heads(qc, MEM_HEADS).astype(BF16))
        pm["qs"] = jnp.concatenate(qs, axis=0)

    def pm_scores():
        pm["s"] = _dot_nt(pm["qs"], mk_ref[0]) * SCALE

    def pm_softmax():
        _, e, l = _softmax_parts(pm["s"])
        pm["e"], pm["inv_l"] = e.astype(BF16), 1.0 / l

    def pm_values():
        pv = _dot(pm["e"], mv_ref[0]) * pm["inv_l"]
        per = MEM_HEADS * CHUNK
        for c, rows in enumerate(chunks):
            cm = _unstack_heads(pv[c * per:(c + 1) * per], MEM_HEADS)
            mixc_ref[rows, :] = (cm * _silu(z_scr[rows, OFF_GC:OFF_GC + MEM_WIDTH])).astype(BF16)

    def gm_prepare():
        lane = _lane_iota((CHUNK, LANES))
        for c, rows in enumerate(chunks):
            z_scr[rows, OFF_U:OFF_U + GMLP_WIDTH] = _gelu(z_scr[rows, OFF_U:OFF_U + GMLP_WIDTH])
            gv = _gelu(z_scr[rows, OFF_V:OFF_V + GMLP_WIDTH])
            mu = jnp.mean(gv, axis=-1, keepdims=True)
            dv = gv - mu
            var = jnp.mean(dv * dv, axis=-1, keepdims=True)
            vln = dv * lax.rsqrt(var + EPS) * lng_ref[...] + lnb_ref[...]
            for p in range(GMLP_WIDTH // LANES):
                vp = vln[:, p * LANES:(p + 1) * LANES]
                gm[c, p] = jnp.concatenate([jnp.where(lane < HEAD_DIM, vp, 0.0),
                                            jnp.where(lane < HEAD_DIM, 0.0, vp)], axis=0).astype(BF16)

    def gm_spatial():
        for key in list(gm):
            gm[key] = _dot(ws_ref[key[1]], gm[key])

    def gm_gate():
        for c, rows in enumerate(chunks):
            for p in range(GMLP_WIDTH // LANES):
                sl = slice(p * LANES, (p + 1) * LANES)
                s = gm[c, p] + bias_ref[:, sl]
                ga = z_scr[rows, OFF_GA + p * LANES:OFF_GA + (p + 1) * LANES]
                gu = z_scr[rows, OFF_U + p * LANES:OFF_U + (p + 1) * LANES]
                mixa_ref[rows, sl] = (gu * s * _silu(ga)).astype(BF16)

    def qk_sums():
        for c, rows in enumerate(chunks):
            for g in groups:
                for off in (OFF_Q, OFF_K):
                    x = z_scr[rows, off + g * LANES:off + (g + 1) * LANES]
                    qk[c, g, off] = _head_sum(x * x)

    def qk_normed(c, rows, g, off, norm_ref, cos, sin):
        x = z_scr[rows, off + g * LANES:off + (g + 1) * LANES]
        return _rope(x * lax.rsqrt(qk[c, g, off] * (1.0 / HEAD_DIM) + EPS) * norm_ref[...], cos, sin)

    project(OFF_QC, MEM_WIDTH)
    project(OFF_GC, MEM_WIDTH)
    for piece in sample_pieces:
        piece.scores()
    project(OFF_U, GMLP_WIDTH)
    for piece in sample_pieces:
        piece.softmax()
    pm_queries()
    project(OFF_V, GMLP_WIDTH)
    pm_scores()
    project(OFF_GA, GMLP_WIDTH)
    pm_softmax()
    gm_prepare()
    for g in groups:
        windows[0, g].shift()
    project(OFF_Q, ATTN_WIDTH)
    sample_finish()
    pm_values()
    gm_spatial()
    project(OFF_K, ATTN_WIDTH)
    gm_gate()
    project(OFF_VV, ATTN_WIDTH)
    qk_sums()
    project(OFF_GB, ATTN_WIDTH)
    for bb in seqs[1:]:
        for g in groups:
            windows[bb, g].shift()
    for c, rows in enumerate(chunks):
        cos = cos_ref[rows, :]
        sin = sin_ref[rows, :]
        for g in range(len(DIL_PAIRS)):
            q = qk_normed(c, rows, g, OFF_Q, qn_ref, cos, sin) * SCALE
            k = qk_normed(c, rows, g, OFF_K, kn_ref, cos, sin)
            v = z_scr[rows, OFF_VV + g * LANES:OFF_VV + (g + 1) * LANES]
            if g == 0:
                for ref, val in zip(nat_refs, (q, k, v)):
                    ref[0, rows, :] = val.astype(BF16)
                if rows is chunks[-1]:
                    t0_ref[0, :, 0:LANES] = k
                    t0_ref[0, :, LANES:2 * LANES] = v
            else:
                for j, val in enumerate((q, k, v)):
                    perm_scr[3 * (g - 1) + j, rows, :] = val
                tail = (t1_ref, t2_ref)[g - 1]
                tail[0, rows, 0:LANES] = k
                tail[0, rows, LANES:2 * LANES] = v

    for rows in chunks:
        sgb_ref[rows, :] = _silu(z_scr[rows, OFF_GB:OFF_GB + ATTN_WIDTH]).astype(sgb_ref.dtype)

    d1 = DIL_PAIRS[1][1]
    d2 = DIL_PAIRS[2][1]
    for j, ref in enumerate((q1_ref, k1_ref, v1_ref)):
        for r in range(d1):
            ref[0, 0, r, 0] = perm_scr[j, pl.ds(r, ROW_TILE // d1, stride=d1), :].astype(BF16)
    for j, ref in enumerate((q2_ref, k2_ref, v2_ref)):
        for r in range(d2):
            ref[0, 0, r, 0] = perm_scr[3 + j, pl.ds(r, ROW_TILE // d2, stride=d2), :].astype(BF16)


def _sample_proj_kernel(x_ref, gain_ref, w_ref, lng_ref, lnb_ref, qn_ref, kn_ref, mqn_ref,
                        cos_ref, sin_ref, ws_ref, bias_ref,
                        mixa_ref, vln_ref, q_ref, k_ref, v_ref, qc_ref, sgb_ref, sgc_ref,
                        h_scr, z_scr):
    _project(x_ref, gain_ref, w_ref, h_scr, z_scr)

    def chunk(c, carry):
        r0 = pl.multiple_of(c * CHUNK, CHUNK)
        rows = pl.ds(r0, CHUNK)
        pieces, vln = _gmlp_chunk(z_scr, rows, lng_ref, lnb_ref, ws_ref, bias_ref)
        for p, a in enumerate(pieces):
            mixa_ref[rows, p * LANES:(p + 1) * LANES] = a.astype(BF16)
        vln_ref[rows, :] = vln
        cos = cos_ref[rows, :]
        sin = sin_ref[rows, :]
        for g in range(len(DIL_PAIRS)):
            sl = slice(g * LANES, (g + 1) * LANES)
            q_ref[rows, sl] = _qk_chunk(z_scr, rows, OFF_Q, g, qn_ref, cos, sin)
            k_ref[rows, sl] = _qk_chunk(z_scr, rows, OFF_K, g, kn_ref, cos, sin)
            v_ref[rows, sl] = z_scr[rows, OFF_VV + g * LANES:OFF_VV + (g + 1) * LANES]
        sgb_ref[rows, :] = _silu(z_scr[rows, OFF_GB:OFF_GB + ATTN_WIDTH])
        for j in range(MEM_WIDTH // LANES):
            sl = slice(j * LANES, (j + 1) * LANES)
            qc_ref[rows, sl] = _head_rms(z_scr[rows, OFF_QC + j * LANES:OFF_QC + (j + 1) * LANES],
                                         mqn_ref[...])
        sgc_ref[rows, :] = _silu(z_scr[rows, OFF_GC:OFF_GC + MEM_WIDTH])
        return carry

    lax.fori_loop(0, ROW_TILE // CHUNK, chunk, 0)


def _rope_tables(pos):
    inv = ROPE_THETA ** (-jnp.arange(ROT_HALF, dtype=F32) * 2.0 / ROT_DIM)
    ang = pos.astype(F32)[:, None] * inv[None, :]
    cos, sin = jnp.cos(ang), jnp.sin(ang)
    n = pos.shape[0]
    pad1 = jnp.ones((n, HEAD_DIM - ROT_DIM), F32)
    pad0 = jnp.zeros((n, HEAD_DIM - ROT_DIM), F32)
    cos_h = jnp.concatenate([cos, cos, pad1], axis=1)
    sin_h = jnp.concatenate([-sin, sin, pad0], axis=1)
    return jnp.tile(cos_h, (1, LANES // HEAD_DIM)), jnp.tile(sin_h, (1, LANES // HEAD_DIM))


def _common_proj_operands(norm_gain, w_in, ln_g, ln_b, q_norm, k_norm, mem_q_norm):
    tile2 = lambda g: jnp.tile(g, LANES // HEAD_DIM).reshape(1, LANES)
    return (norm_gain.reshape(1, D_MODEL), w_in.astype(BF16), ln_g.reshape(1, GMLP_WIDTH),
            ln_b.reshape(1, GMLP_WIDTH), tile2(q_norm), tile2(k_norm), tile2(mem_q_norm))


def _common_proj_specs(const):
    return [
        const((1, D_MODEL)), const((D_MODEL, N_IN)), const((1, GMLP_WIDTH)), const((1, GMLP_WIDTH)),
        const((1, LANES)), const((1, LANES)), const((1, LANES)),
    ]


def _pair_ws(ws):
    return jnp.concatenate([ws[0::2], ws[1::2]], axis=2).astype(BF16)


def _prompt_proj(x, common, cos, sin, ws_pair, bias, mem_k, mem_v, wins, mem_s, sample_feats, t_new):
    b, s, _ = x.shape
    n_tiles = s // ROW_TILE
    bd = mem_s.shape[0]
    assert b * n_tiles * SEQS_PER_STEP == bd
    w0, w1, w2 = (w for w, _ in DIL_PAIRS)
    d1, d2 = DIL_PAIRS[1][1], DIL_PAIRS[2][1]
    per1, per2 = w1 // ROW_TILE, w2 // ROW_TILE
    x2 = x.reshape(b * s, D_MODEL)
    const = lambda shape: pl.BlockSpec(shape, lambda i, j: (0,) * len(shape))
    rows = lambda width: pl.BlockSpec((ROW_TILE, width), lambda i, j: (i * n_tiles + j, 0))
    seq_blk = lambda a: pl.BlockSpec((SEQS_PER_STEP,) + a.shape[1:], lambda i, j: (i * n_tiles + j, 0, 0))
    seq_rows = lambda width: pl.BlockSpec((SEQS_PER_STEP * t_new, width), lambda i, j: (i * n_tiles + j, 0))
    in_specs = [rows(D_MODEL)] + _common_proj_specs(const) + [
        pl.BlockSpec((ROW_TILE, LANES), lambda i, j: (j, 0)),
        pl.BlockSpec((ROW_TILE, LANES), lambda i, j: (j, 0)),
        const((GMLP_WIDTH // LANES, CHUNK, 2 * CHUNK)), const((CHUNK, GMLP_WIDTH)),
        pl.BlockSpec((1, N_MEM, MEM_WIDTH), lambda i, j: (i, 0, 0)),
        pl.BlockSpec((1, N_MEM, MEM_WIDTH), lambda i, j: (i, 0, 0)),
    ] + [seq_blk(w) for w in wins] + [seq_blk(mem_s)] + [seq_rows(f.shape[1]) for f in sample_feats]
    nat = pl.BlockSpec((1, ROW_TILE, LANES), lambda i, j: (i, j, 0))
    g1 = pl.BlockSpec((1, 1, d1, 1, ROW_TILE // d1, LANES), lambda i, j: (i, j // per1, 0, j % per1, 0, 0))
    g2 = pl.BlockSpec((1, 1, d2, 1, ROW_TILE // d2, LANES), lambda i, j: (i, j // per2, 0, j % per2, 0, 0))
    last = lambda n_blk: (lambda i, j: (i, jnp.maximum(j - (n_tiles - n_blk), 0), 0))
    t0 = pl.BlockSpec((1, w0, 2 * LANES), last(1))
    t1 = pl.BlockSpec((1, ROW_TILE, 2 * LANES), last(per1))
    t2 = pl.BlockSpec((1, ROW_TILE, 2 * LANES), last(per2))
    out_specs = [rows(GMLP_WIDTH), rows(MEM_WIDTH), rows(ATTN_WIDTH),
                 nat, nat, nat, g1, g1, g1, g2, g2, g2, t0, t1, t2] \
        + [seq_blk(w) for w in wins] + [seq_rows(ATTN_WIDTH + MEM_WIDTH)]
    nat_s = jax.ShapeDtypeStruct((b, s, LANES), BF16)
    g1_s = jax.ShapeDtypeStruct((b, s // w1, d1, per1, ROW_TILE // d1, LANES), BF16)
    g2_s = jax.ShapeDtypeStruct((b, s // w2, d2, per2, ROW_TILE // d2, LANES), BF16)
    out_shape = [
        jax.ShapeDtypeStruct((b * s, GMLP_WIDTH), BF16),
        jax.ShapeDtypeStruct((b * s, MEM_WIDTH), BF16),
        jax.ShapeDtypeStruct((b * s, ATTN_WIDTH), BF16),
        nat_s, nat_s, nat_s, g1_s, g1_s, g1_s, g2_s, g2_s, g2_s,
        jax.ShapeDtypeStruct((b, w0, 2 * LANES), F32),
        jax.ShapeDtypeStruct((b, w1, 2 * LANES), F32),
        jax.ShapeDtypeStruct((b, w2, 2 * LANES), F32),
    ] + [jax.ShapeDtypeStruct(w.shape, F32) for w in wins] \
      + [jax.ShapeDtypeStruct((bd * t_new, ATTN_WIDTH + MEM_WIDTH), F32)]
    return pl.pallas_call(
        functools.partial(_prompt_proj_kernel, t_new=t_new),
        grid=(b, n_tiles),
        in_specs=in_specs,
        out_specs=out_specs,
        out_shape=out_shape,
        scratch_shapes=[
            pltpu.VMEM((ROW_TILE, D_MODEL), BF16),
            pltpu.VMEM((ROW_TILE, N_IN), F32),
            pltpu.VMEM((6, ROW_TILE, LANES), F32),
            pltpu.VMEM((SEQS_PER_STEP, 2 * LANES, sum(w.shape[2] for w in wins)), BF16),
            pltpu.VMEM((SEQS_PER_STEP,) + mem_s.shape[1:], BF16),
        ],
        compiler_params=pltpu.CompilerParams(
            dimension_semantics=("arbitrary", "arbitrary"), vmem_limit_bytes=VMEM_LIMIT),
        name="prompt_proj",
    )(x2, *common, cos, sin, ws_pair, bias, mem_k, mem_v, *wins, mem_s, *sample_feats)


def _sample_proj(x, common, cos, sin, ws_pair, bias):
    n = x.shape[0]
    const = lambda shape: pl.BlockSpec(shape, lambda i: (0,) * len(shape))
    rows = lambda width: pl.BlockSpec((ROW_TILE, width), lambda i: (i, 0))
    in_specs = [rows(D_MODEL)] + _common_proj_specs(const) + [
        const((ROW_TILE, LANES)), const((ROW_TILE, LANES)),
        const((GMLP_WIDTH // LANES, CHUNK, 2 * CHUNK)), const((CHUNK, GMLP_WIDTH)),
    ]
    widths = (GMLP_WIDTH, GMLP_WIDTH, ATTN_WIDTH, ATTN_WIDTH, ATTN_WIDTH, MEM_WIDTH, ATTN_WIDTH, MEM_WIDTH)
    dtypes = (BF16,) + (F32,) * 7
    return pl.pallas_call(
        _sample_proj_kernel,
        grid=(n // ROW_TILE,),
        in_specs=in_specs,
        out_specs=[rows(w) for w in widths],
        out_shape=[jax.ShapeDtypeStruct((n, w), dt) for w, dt in zip(widths, dtypes)],
        scratch_shapes=[
            pltpu.VMEM((ROW_TILE, D_MODEL), BF16),
            pltpu.VMEM((ROW_TILE, N_IN), F32),
        ],
        compiler_params=pltpu.CompilerParams(
            dimension_semantics=("arbitrary",), vmem_limit_bytes=VMEM_LIMIT),
        name="sample_proj",
    )(x, *common, cos, sin, ws_pair, bias)


def _window_unit(q, kk, vv, bias):
    n = q.shape[0]
    lane_lo = _lane_iota(q.shape) < HEAD_DIM
    zero = jnp.zeros_like(q)
    qs = jnp.concatenate([jnp.where(lane_lo, q, zero), jnp.where(lane_lo, zero, q)], axis=0)
    d = _dot_nt(qs, kk)
    parts = [_softmax_parts(d[h * n:(h + 1) * n] + bias) for h in range(2)]
    e = jnp.concatenate([p[1].astype(BF16) for p in parts], axis=0)
    pv = _dot(e, vv)
    o = jnp.where(lane_lo, pv[:n] * (1.0 / parts[0][2]), pv[n:] * (1.0 / parts[1][2]))
    lse = [p[0] + jnp.log(p[2]) for p in parts]
    lse_b = jnp.where(lane_lo, jnp.broadcast_to(lse[0], q.shape), jnp.broadcast_to(lse[1], q.shape))
    return o, lse_b


def _prompt_attn_kernel(*refs):
    (*attn_refs, x_ref, mixa_ref, mixc_ref, w_ref, y_ref,
     o_scr, lse_scr, bias_scr, mixb_scr, w16_scr) = refs
    m, k = pl.program_id(1), pl.program_id(2)

    @pl.when((pl.program_id(0) == 0) & (m == 0) & (k == 0))
    def _():
        w16_scr[...] = w_ref[...].astype(BF16)

    @pl.when(k == 0)
    def _():
        _window_attention(*attn_refs, mixb_scr, o_scr, lse_scr, bias_scr, m)

    sub = y_ref.shape[0]
    rows = pl.ds(pl.multiple_of(k * sub, sub), sub)
    mix = jnp.concatenate([mixa_ref[...], mixb_scr[rows, :], mixc_ref[...]], axis=1)
    y_ref[...] = x_ref[...] + _dot(mix, w16_scr[...])


def _window_attention(q0_ref, k0_ref, v0_ref, k0p_ref, v0p_ref,
                      q1_ref, k1_ref, v1_ref, k1p_ref, v1p_ref,
                      q2_ref, k2_ref, v2_ref, k2p_ref, v2p_ref,
                      sgb_ref, mix_ref, o_scr, lse_scr, bias_scr, m):
    blk = CHUNK
    qo = lax.broadcasted_iota(jnp.int32, (blk, 2 * blk), 0)
    ko = lax.broadcasted_iota(jnp.int32, (blk, 2 * blk), 1)
    band = (ko >= qo) & (ko <= qo + blk)
    bias_scr[0] = jnp.where(band, 0.0, NEG)
    bias_scr[1] = jnp.where(band & (ko >= blk * (m == 0).astype(jnp.int32)), 0.0, NEG)
    band, first = 0, 1

    d1, d2 = DIL_PAIRS[1][1], DIL_PAIRS[2][1]
    w1 = DIL_PAIRS[1][0]
    for b0 in range(ATTN_TILE // blk):
        q = q0_ref[0, b0 * blk:(b0 + 1) * blk, :]
        if b0 == 0:
            kk = jnp.concatenate([k0p_ref[0], k0_ref[0, 0:blk, :]], axis=0)
            vv = jnp.concatenate([v0p_ref[0], v0_ref[0, 0:blk, :]], axis=0)
            valid = first
        else:
            kk = k0_ref[0, (b0 - 1) * blk:(b0 + 1) * blk, :]
            vv = v0_ref[0, (b0 - 1) * blk:(b0 + 1) * blk, :]
            valid = band
        o, lse = _window_unit(q, kk, vv, bias_scr[valid])
        o_scr[0, b0 * blk:(b0 + 1) * blk, :] = o
        lse_scr[0, b0 * blk:(b0 + 1) * blk, :] = lse
    for nn in range(ATTN_TILE // w1):
        for r in range(d1):
            blk_of = lambda ref, n: ref[0, n, r].reshape(blk, LANES)
            q = blk_of(q1_ref, nn)
            if nn == 0:
                kk = jnp.concatenate([blk_of(k1p_ref, 0), blk_of(k1_ref, 0)], axis=0)
                vv = jnp.concatenate([blk_of(v1p_ref, 0), blk_of(v1_ref, 0)], axis=0)
                valid = first
            else:
                kk = jnp.concatenate([blk_of(k1_ref, nn - 1), blk_of(k1_ref, nn)], axis=0)
                vv = jnp.concatenate([blk_of(v1_ref, nn - 1), blk_of(v1_ref, nn)], axis=0)
                valid = band
            o, lse = _window_unit(q, kk, vv, bias_scr[valid])
            dst = pl.ds(nn * w1 + r, blk, stride=d1)
            o_scr[1, dst, :] = o
            lse_scr[1, dst, :] = lse
    for r in range(d2):
        q = q2_ref[0, 0, r].reshape(blk, LANES)
        kk = jnp.concatenate([k2p_ref[0, 0, r].reshape(blk, LANES), k2_ref[0, 0, r].reshape(blk, LANES)], axis=0)
        vv = jnp.concatenate([v2p_ref[0, 0, r].reshape(blk, LANES), v2_ref[0, 0, r].reshape(blk, LANES)], axis=0)
        o, lse = _window_unit(q, kk, vv, bias_scr[first])
        dst = pl.ds(r, blk, stride=d2)
        o_scr[2, dst, :] = o
        lse_scr[2, dst, :] = lse

    def combine(c, carry):
        rows = pl.ds(pl.multiple_of(c * blk, blk), blk)
        ls = [lse_scr[g, rows, :] for g in range(3)]
        mx = jnp.maximum(jnp.maximum(ls[0], ls[1]), ls[2])
        es = [jnp.exp(x - mx) for x in ls]
        inv = 1.0 / (es[0] + es[1] + es[2])
        for g in range(3):
            sl = slice(g * LANES, (g + 1) * LANES)
            mix_ref[rows, sl] = (o_scr[g, rows, :] * (es[g] * inv) * sgb_ref[rows, sl]).astype(BF16)
        return carry

    lax.fori_loop(0, ATTN_TILE // blk, combine, 0)


def _prompt_attn(feats, sgb, x, mixa, mixc, w_out, b, s):
    q0, k0, v0, q1, k1, v1, q2, k2, v2 = feats
    w1, d1 = DIL_PAIRS[1]
    w2, d2 = DIL_PAIRS[2]
    n_tiles = s // ATTN_TILE
    n_sub = ATTN_TILE // ATTN_SUB
    per1 = ATTN_TILE // w1
    prev = lambda m, k: jnp.maximum(m * k - 1, 0)
    nat = pl.BlockSpec((1, ATTN_TILE, LANES), lambda i, m, k: (i, m, 0))
    nat_p = pl.BlockSpec((1, CHUNK, LANES), lambda i, m, k: (i, prev(m, ATTN_TILE // CHUNK), 0))
    g1 = pl.BlockSpec((1, per1) + q1.shape[2:], lambda i, m, k: (i, m, 0, 0, 0, 0))
    g1_p = pl.BlockSpec((1, 1) + q1.shape[2:], lambda i, m, k: (i, prev(m, per1), 0, 0, 0, 0))
    g2 = pl.BlockSpec((1, 1) + q2.shape[2:], lambda i, m, k: (i, m, 0, 0, 0, 0))
    g2_p = pl.BlockSpec((1, 1) + q2.shape[2:], lambda i, m, k: (i, prev(m, 1), 0, 0, 0, 0))
    blk_rows = pl.BlockSpec((ATTN_TILE, ATTN_WIDTH), lambda i, m, k: (i * n_tiles + m, 0))
    sub_rows = lambda width: pl.BlockSpec(
        (ATTN_SUB, width), lambda i, m, k: ((i * n_tiles + m) * n_sub + k, 0))
    return pl.pallas_call(
        _prompt_attn_kernel,
        grid=(b, n_tiles, n_sub),
        in_specs=[nat, nat, nat, nat_p, nat_p, g1, g1, g1, g1_p, g1_p, g2, g2, g2, g2_p, g2_p, blk_rows,
                  sub_rows(D_MODEL), sub_rows(GMLP_WIDTH), sub_rows(MEM_WIDTH),
                  pl.BlockSpec((MIX_WIDTH, D_MODEL), lambda i, m, k: (0, 0))],
        out_specs=sub_rows(D_MODEL),
        out_shape=jax.ShapeDtypeStruct((b * s, D_MODEL), F32),
        scratch_shapes=[
            pltpu.VMEM((3, ATTN_TILE, LANES), F32),
            pltpu.VMEM((3, ATTN_TILE, LANES), F32),
            pltpu.VMEM((2, CHUNK, 2 * CHUNK), F32),
            pltpu.VMEM((ATTN_TILE, ATTN_WIDTH), BF16),
            pltpu.VMEM((MIX_WIDTH, D_MODEL), BF16),
        ],
        compiler_params=pltpu.CompilerParams(
            dimension_semantics=("arbitrary", "arbitrary", "arbitrary"), vmem_limit_bytes=VMEM_LIMIT),
        name="prompt_attn",
    )(q0, k0, v0, k0, v0, q1, k1, v1, k1, v1, q2, k2, v2, k2, v2, sgb, x, mixa, mixc, w_out)


def _window_offsets(win_refs):
    offs, total = [], 0
    for r in win_refs:
        offs.append(total)
        total += r.shape[2]
    return offs, total


def _stage_bf16(win_refs, mem_ref, kv16_ref, mem16_ref):
    offs, _ = _window_offsets(win_refs)
    for bb in range(mem_ref.shape[0]):
        for wref, off in zip(win_refs, offs):
            kv16_ref[bb, :, off:off + wref.shape[2]] = wref[bb].astype(BF16)
        mem16_ref[bb] = mem_ref[bb].astype(BF16)


class _SampleWindow:
    def __init__(self, wref, oref, kv16_ref, kv16_off, q_ref, k_ref, v_ref, bb, g, t_new):
        self.wref, self.oref, self.bb, self.t_new = wref, oref, bb, t_new
        self.window, self.dil = DIL_PAIRS[g]
        self.length = wref.shape[2]
        rows = slice(bb * t_new, (bb + 1) * t_new)
        sl = slice(g * LANES, (g + 1) * LANES)
        self.load_new = lambda: (q_ref[rows, sl], k_ref[rows, sl], v_ref[rows, sl])
        self.kv16 = lambda half: kv16_ref[bb, half * LANES:(half + 1) * LANES, kv16_off:kv16_off + self.length]

    def scores(self):
        t_new = self.t_new
        qg, kg, vg = self.load_new()
        self.lane_lo = _lane_iota(qg.shape) < HEAD_DIM
        qs = jnp.concatenate([jnp.where(self.lane_lo, qg, 0.0), jnp.where(self.lane_lo, 0.0, qg)],
                             axis=0).astype(BF16)
        zpad = jnp.zeros((CHUNK - t_new, LANES), F32)
        self.k_new = jnp.concatenate([kg, zpad], axis=0)
        self.v_new = jnp.concatenate([vg, zpad], axis=0)
        self.s_old = _dot(qs, self.kv16(0)) * SCALE
        self.s_new = _dot_nt(qs, self.k_new.astype(BF16)) * SCALE

    def softmax(self):
        t_new, dil, window = self.t_new, self.dil, self.window
        s_old, s_new = self.s_old, self.s_new
        tq = lax.broadcasted_iota(jnp.int32, s_old.shape, 0) & (t_new - 1)
        dist = self.length + tq - lax.broadcasted_iota(jnp.int32, s_old.shape, 1)
        ok_old = ((dist & (dil - 1)) == 0) & (dist <= window)
        tqn = lax.broadcasted_iota(jnp.int32, s_new.shape, 0) & (t_new - 1)
        dn = tqn - lax.broadcasted_iota(jnp.int32, s_new.shape, 1)
        ok_new = (dn >= 0) & ((dn & (dil - 1)) == 0) & (dn <= window)
        s_old = jnp.where(ok_old, s_old, NEG)
        s_new = jnp.where(ok_new, s_new, NEG)
        mx = jnp.maximum(jnp.max(s_old, axis=-1, keepdims=True), jnp.max(s_new, axis=-1, keepdims=True))
        e_old = jnp.exp(s_old - mx)
        e_new = jnp.exp(s_new - mx)
        l = jnp.sum(e_old, axis=-1, keepdims=True) + jnp.sum(e_new, axis=-1, keepdims=True)
        self.e_old, self.e_new = e_old.astype(BF16), e_new.astype(BF16)
        self.inv_l = 1.0 / l
        self.lse = mx + jnp.log(l)

    def values(self):
        t_new, lane_lo = self.t_new, self.lane_lo
        pv = (_dot_nt(self.e_old, self.kv16(1)) + _dot(self.e_new, self.v_new.astype(BF16))) * self.inv_l
        out = jnp.where(lane_lo, pv[:t_new], pv[t_new:])
        shape = (t_new, LANES)
        lse_b = jnp.where(lane_lo, jnp.broadcast_to(self.lse[:t_new], shape),
                          jnp.broadcast_to(self.lse[t_new:], shape))
        return out, lse_b

    def shift(self):
        t_new, wref, oref, bb = self.t_new, self.wref, self.oref, self.bb
        _, kg, vg = self.load_new()
        zpad = jnp.zeros((CHUNK - t_new, LANES), F32)
        new_t = jnp.concatenate([jnp.concatenate([kg, zpad], axis=0).T,
                                 jnp.concatenate([vg, zpad], axis=0).T], axis=0)
        keep = _lane_iota((2 * LANES, LANES)) < LANES - t_new
        nxt = pltpu.roll(wref[bb, :, 0:LANES], LANES - t_new, 1)
        n_blk = self.length // LANES
        for c in range(n_blk):
            cur = nxt
            if c + 1 < n_blk:
                nxt = pltpu.roll(wref[bb, :, (c + 1) * LANES:(c + 2) * LANES], LANES - t_new, 1)
            else:
                nxt = pltpu.roll(new_t, LANES - t_new, 1)
            oref[bb, :, c * LANES:(c + 1) * LANES] = jnp.where(keep, cur, nxt)


class _SampleMemory:
    def __init__(self, mem16_ref, qc_ref, bb, t_new):
        self.rows = slice(bb * t_new, (bb + 1) * t_new)
        self.qc_ref = qc_ref
        self.half = lambda h: mem16_ref[bb, h * MEM_WIDTH:(h + 1) * MEM_WIDTH, :]

    def scores(self):
        self.s = _dot(_stack_heads(self.qc_ref[self.rows, :], MEM_HEADS).astype(BF16), self.half(0)) * SCALE

    def softmax(self):
        _, e, l = _softmax_parts(self.s)
        self.e, self.inv_l = e.astype(BF16), 1.0 / l

    def values(self):
        return _unstack_heads(_dot_nt(self.e, self.half(1)) * self.inv_l, MEM_HEADS)


def _sample_finish(outs, lses, cm, sgb_ref, sgc_ref, mix_ref, bb, t_new):
    rows = slice(bb * t_new, (bb + 1) * t_new)
    mx = jnp.maximum(jnp.maximum(lses[0], lses[1]), lses[2])
    es = [jnp.exp(x - mx) for x in lses]
    inv = 1.0 / (es[0] + es[1] + es[2])
    for g in range(len(DIL_PAIRS)):
        sl = slice(g * LANES, (g + 1) * LANES)
        mix_ref[rows, sl] = outs[g] * (es[g] * inv) * sgb_ref[rows, sl]
    mix_ref[rows, ATTN_WIDTH:ATTN_WIDTH + MEM_WIDTH] = cm * sgc_ref[rows, :]


def _out_proj_kernel(x_ref, *refs):
    *mix_refs, w_ref, y_ref = refs
    mix = jnp.concatenate([r[...].astype(BF16) for r in mix_refs], axis=1)
    y_ref[...] = x_ref[...] + _dot(mix, w_ref[...].astype(BF16))


def _out_proj(x, mixes, w_out):
    n = x.shape[0]
    rows = lambda width: pl.BlockSpec((OUT_TILE, width), lambda i: (i, 0))
    return pl.pallas_call(
        _out_proj_kernel,
        grid=(n // OUT_TILE,),
        in_specs=[rows(D_MODEL)] + [rows(mx.shape[1]) for mx in mixes]
        + [pl.BlockSpec((MIX_WIDTH, D_MODEL), lambda i: (0, 0))],
        out_specs=rows(D_MODEL),
        out_shape=jax.ShapeDtypeStruct((n, D_MODEL), F32),
        compiler_params=pltpu.CompilerParams(
            dimension_semantics=("arbitrary",), vmem_limit_bytes=VMEM_LIMIT),
        name="out_proj",
    )(x, *mixes, w_out)


def kernel(x_prompt, x_sample, state_win0_kv, state_win1_kv, state_win2_kv, cache_mem_kv, mem_prompt, norm_gain, w_in, gmlp_ln_gain, gmlp_ln_bias, gmlp_w_s, gmlp_b_s, attn_q_norm, attn_k_norm, mem_norm, w_mem_kv, mem_q_norm, mem_k_norm, w_out):
    depth = norm_gain.shape[0]
    assert depth == 1, "single-layer step only"
    bp, s, _ = x_prompt.shape
    bd, t, _ = x_sample.shape
    past_len = PAST_LEN
    assert s % ATTN_TILE == 0 and (bd * t) % OUT_TILE == 0 and (bp * s) % OUT_TILE == 0
    assert CHUNK % t == 0 and t & (t - 1) == 0 and DIL_PAIRS[1][0] % ROW_TILE == 0
    states = (state_win0_kv, state_win1_kv, state_win2_kv)
    for st, (w, d) in zip(states, DIL_PAIRS):
        assert st.shape[2] == w and w // d == CHUNK and d & (d - 1) == 0

    common = _common_proj_operands(norm_gain[0], w_in[0], gmlp_ln_gain[0], gmlp_ln_bias[0],
                                   attn_q_norm[0], attn_k_norm[0], mem_q_norm[0])
    tril = jnp.tril(jnp.ones((CHUNK, CHUNK), dtype=bool))
    ws = jnp.where(tril[None], gmlp_w_s[0], 0.0)
    b_s = gmlp_b_s[0]

    reps = CHUNK // t
    ws_s = jnp.stack([jnp.kron(jnp.eye(reps, dtype=F32), ws[g, :t, :t]) for g in range(N_GMLP_GROUPS)])
    bias_s = jnp.repeat(jnp.tile(b_s[:, :t], (1, reps)).T, HEAD_DIM, axis=1)
    cos_s, sin_s = _rope_tables(past_len + (jnp.arange(ROW_TILE, dtype=jnp.int32) % t))
    mixa_s, vln_s, q_s, k_s, v_s, qc_s, sgb_s, sgc_s = _sample_proj(
        x_sample.reshape(bd * t, D_MODEL), common, cos_s, sin_s, _pair_ws(ws_s), bias_s)
    feat_major = lambda a: jnp.transpose(a, (0, 2, 3, 4, 1)).reshape(a.shape[0], -1, a.shape[1])
    wins = [feat_major(st[0]) for st in states]
    mem_s = feat_major(cache_mem_kv[0])

    mem_kv_p, mem_k, mem_v = _mem_kv(mem_prompt, mem_norm[0], w_mem_kv[0], mem_k_norm[0])
    cos_p, sin_p = _rope_tables(jnp.arange(s, dtype=jnp.int32))
    bias_p = jnp.repeat(b_s.T, HEAD_DIM, axis=1)
    outs = _prompt_proj(x_prompt, common, cos_p, sin_p, _pair_ws(ws), bias_p, mem_k, mem_v,
                        wins, mem_s, (q_s, k_s, v_s, qc_s, sgb_s, sgc_s), t)
    mixa_p, mixc_p, sgb_p = outs[0:3]
    tails = outs[12:15]
    new_wins_t = outs[15:19]
    y_prompt = _prompt_attn(outs[3:12], sgb_p, x_prompt.reshape(bp * s, D_MODEL), mixa_p, mixc_p,
                            w_out[0], bp, s)

    mixbc_s = new_wins_t[3]
    heads = LANES // HEAD_DIM
    nw0, nw1, nw2 = [
        jnp.transpose(w.reshape(bd, 2, heads, HEAD_DIM, w.shape[2]), (0, 4, 1, 2, 3)) for w in new_wins_t[:3]]
    y_sample = _out_proj(x_sample.reshape(bd * t, D_MODEL), (mixa_s, mixbc_s), w_out[0])

    win_shape = lambda n, w: (1, n, w, 2, heads, HEAD_DIM)
    return (
        y_prompt.reshape(bp, s, D_MODEL),
        y_sample.reshape(bd, t, D_MODEL),
        tails[0].reshape(win_shape(bp, DIL_PAIRS[0][0])),
        tails[1].reshape(win_shape(bp, DIL_PAIRS[1][0])),
        tails[2].reshape(win_shape(bp, DIL_PAIRS[2][0])),
        mem_kv_p.reshape(1, bp, N_MEM, 2, MEM_HEADS, HEAD_DIM),
        nw0.reshape(win_shape(bd, DIL_PAIRS[0][0])),
        nw1.reshape(win_shape(bd, DIL_PAIRS[1][0])),
        nw2.reshape(win_shape(bd, DIL_PAIRS[2][0])),
        vln_s.reshape(1, bd, t, N_GMLP_GROUPS, HEAD_DIM),
    )
```

```python
import functools

import numpy as np
import jax
import jax.numpy as jnp
from jax import lax
from jax.experimental import pallas as pl
from jax.experimental.pallas import tpu as pltpu

F32 = jnp.float32
BF16 = jnp.bfloat16

D_MODEL = 1024
HEAD_DIM = 64
LANES = 128
N_GMLP_GROUPS = 6
GMLP_WIDTH = N_GMLP_GROUPS * HEAD_DIM
CHUNK = 128
DIL_PAIRS = ((128, 1), (512, 4), (2048, 16))
ATTN_WIDTH = len(DIL_PAIRS) * LANES
N_MEM = 256
PAST_LEN = 8192
MEM_HEADS = 4
MEM_WIDTH = MEM_HEADS * HEAD_DIM
MIX_WIDTH = GMLP_WIDTH + ATTN_WIDTH + MEM_WIDTH
N_IN = 3 * GMLP_WIDTH + 4 * ATTN_WIDTH + 2 * MEM_WIDTH
ROPE_THETA = 500000.0
ROT_DIM = HEAD_DIM // 4
ROT_HALF = ROT_DIM // 2
EPS = 1e-6
NEG = -1e30
SCALE = HEAD_DIM ** -0.5
assert np.frexp(SCALE)[0] == 0.5, "the prompt path folds SCALE into q, exact only for a power of two"

OFF_U = 0
OFF_V = OFF_U + GMLP_WIDTH
OFF_GA = OFF_V + GMLP_WIDTH
OFF_Q = OFF_GA + GMLP_WIDTH
OFF_K = OFF_Q + ATTN_WIDTH
OFF_VV = OFF_K + ATTN_WIDTH
OFF_GB = OFF_VV + ATTN_WIDTH
OFF_QC = OFF_GB + ATTN_WIDTH
OFF_GC = OFF_QC + MEM_WIDTH

ROW_TILE = 256
ATTN_TILE = 2048
SEQS_PER_STEP = 2
OUT_TILE = 1024
VMEM_LIMIT = 56 * 1024 * 1024


def _dot(a, b):
    return jnp.dot(a, b, preferred_element_type=F32)


def _dot_nt(a, b):
    return lax.dot_general(a, b, (((1,), (1,)), ((), ())), preferred_element_type=F32)


def _silu(x):
    return x * (1.0 / (1.0 + jnp.exp(-x)))


def _gelu(x):
    return 0.5 * x * (1.0 + lax.erf(x * np.float32(np.sqrt(0.5))))


def _lane_iota(shape):
    return lax.broadcasted_iota(jnp.int32, shape, len(shape) - 1)


def _head_sum(ss):
    r = (lax.broadcasted_iota(jnp.int32, (2 * LANES, LANES), 0) >> 6) & 1
    c = lax.broadcasted_iota(jnp.int32, (2 * LANES, LANES), 1) >> 6
    ones_blk = jnp.where(r == c, 1.0, 0.0).astype(BF16)
    hi = ss.astype(BF16)
    lo = (ss - hi.astype(F32)).astype(BF16)
    return _dot(jnp.concatenate([hi, lo], axis=1), ones_blk)


def _head_rms(x, gain):
    ms = _head_sum(x * x) * (1.0 / HEAD_DIM)
    return x * lax.rsqrt(ms + EPS) * gain


def _rope(x, cos, sin):
    lane = _lane_iota(x.shape) & (HEAD_DIM - 1)
    partner = jnp.where(lane < ROT_HALF, pltpu.roll(x, LANES - ROT_HALF, 1), pltpu.roll(x, ROT_HALF, 1))
    return x * cos + partner * sin


def _softmax_parts(s, axis=-1):
    mx = jnp.max(s, axis=axis, keepdims=True)
    e = jnp.exp(s - mx)
    return mx, e, jnp.sum(e, axis=axis, keepdims=True)


def _mem_kv_kernel(mem_ref, norm_ref, w_ref, kn_ref, kv_ref, k_ref, v_ref):
    x = mem_ref[0]
    h = x * lax.rsqrt(jnp.mean(x * x, axis=-1, keepdims=True) + EPS) * norm_ref[...]
    kv = _dot(h.astype(BF16), w_ref[...].astype(BF16))
    for c in range(MEM_WIDTH // LANES):
        sl = slice(c * LANES, (c + 1) * LANES)
        kc = _head_rms(kv[:, sl], kn_ref[...])
        kv_ref[0, :, sl] = kc
        k_ref[0, :, sl] = kc.astype(BF16)
    v = kv[:, MEM_WIDTH:]
    kv_ref[0, :, MEM_WIDTH:] = v
    v_ref[0] = v.astype(BF16)


def _mem_kv(mem, mem_norm, w_mem_kv, mem_k_norm):
    b = mem.shape[0]
    return pl.pallas_call(
        _mem_kv_kernel,
        grid=(b,),
        in_specs=[
            pl.BlockSpec((1, N_MEM, D_MODEL), lambda i: (i, 0, 0)),
            pl.BlockSpec((1, D_MODEL), lambda i: (0, 0)),
            pl.BlockSpec((D_MODEL, 2 * MEM_WIDTH), lambda i: (0, 0)),
            pl.BlockSpec((1, LANES), lambda i: (0, 0)),
        ],
        out_specs=[
            pl.BlockSpec((1, N_MEM, 2 * MEM_WIDTH), lambda i: (i, 0, 0)),
            pl.BlockSpec((1, N_MEM, MEM_WIDTH), lambda i: (i, 0, 0)),
            pl.BlockSpec((1, N_MEM, MEM_WIDTH), lambda i: (i, 0, 0)),
        ],
        out_shape=[
            jax.ShapeDtypeStruct((b, N_MEM, 2 * MEM_WIDTH), F32),
            jax.ShapeDtypeStruct((b, N_MEM, MEM_WIDTH), BF16),
            jax.ShapeDtypeStruct((b, N_MEM, MEM_WIDTH), BF16),
        ],
        name="mem_kv",
    )(mem, mem_norm.reshape(1, D_MODEL), w_mem_kv,
      jnp.tile(mem_k_norm, LANES // HEAD_DIM).reshape(1, LANES))


def _gmlp_chunk(z_scr, rows, lng_ref, lnb_ref, ws_ref, bias_ref):
    gu = _gelu(z_scr[rows, OFF_U:OFF_U + GMLP_WIDTH])
    gv = _gelu(z_scr[rows, OFF_V:OFF_V + GMLP_WIDTH])
    mu = jnp.mean(gv, axis=-1, keepdims=True)
    dv = gv - mu
    var = jnp.mean(dv * dv, axis=-1, keepdims=True)
    vln = dv * lax.rsqrt(var + EPS) * lng_ref[...] + lnb_ref[...]
    lane = _lane_iota((CHUNK, LANES))
    pieces = []
    for p in range(GMLP_WIDTH // LANES):
        sl = slice(p * LANES, (p + 1) * LANES)
        vp = vln[:, sl]
        rhs = jnp.concatenate([jnp.where(lane < HEAD_DIM, vp, 0.0),
                               jnp.where(lane < HEAD_DIM, 0.0, vp)], axis=0).astype(BF16)
        s = _dot(ws_ref[p], rhs) + bias_ref[:, sl]
        ga = z_scr[rows, OFF_GA + p * LANES:OFF_GA + (p + 1) * LANES]
        pieces.append(gu[:, sl] * s * _silu(ga))
    return pieces, vln


def _qk_chunk(z_scr, rows, off, g, norm_ref, cos, sin):
    x = z_scr[rows, off + g * LANES:off + (g + 1) * LANES]
    return _rope(_head_rms(x, norm_ref[...]), cos, sin)


def _stack_heads(x, n_heads):
    head = _lane_iota(x.shape) >> 6
    return jnp.concatenate([jnp.where(head == h, x, 0.0) for h in range(n_heads)], axis=0)


def _unstack_heads(y, n_heads):
    r = y.shape[0] // n_heads
    head = _lane_iota((r, y.shape[1])) >> 6
    out = y[0:r]
    for h in range(1, n_heads):
        out = jnp.where(head == h, y[h * r:(h + 1) * r], out)
    return out


def _mem_attend(qc, mk, mv):
    s = _dot_nt(_stack_heads(qc, MEM_HEADS).astype(BF16), mk) * SCALE
    _, e, l = _softmax_parts(s)
    pv = _dot(e.astype(BF16), mv) * (1.0 / l)
    return _unstack_heads(pv, MEM_HEADS)


def _prompt_proj_kernel(x_ref, gain_ref, w_ref, lng_ref, lnb_ref, qn_ref, kn_ref, mqn_ref,
                        rope_b_ref, rope_r_ref, ws_ref, bias_ref, mk_ref, mv_ref,
                        w0_ref, w1_ref, w2_ref, mem_ref, sq_ref, sk_ref, sv_ref, sqc_ref, ssgb_ref, ssgc_ref,
                        mixa_ref, mixc_ref, sgb_ref,
                        q0_ref, k0_ref, v0_ref, q1_ref, k1_ref, v1_ref, q2_ref, k2_ref, v2_ref,
                        t0_ref, t1_ref, t2_ref,
                        o0_ref, o1_ref, o2_ref, smix_ref,
                        h_scr, z_scr, perm_scr, kv16_scr, mem16_scr, *, t_new):
    win_refs, out_refs = (w0_ref, w1_ref, w2_ref), (o0_ref, o1_ref, o2_ref)
    kv16_offs, _ = _window_offsets(win_refs)
    seqs = range(SEQS_PER_STEP)
    groups = range(len(DIL_PAIRS))
    windows = {(bb, g): _SampleWindow(win_refs[g], out_refs[g], kv16_scr, kv16_offs[g],
                                      sq_ref, sk_ref, sv_ref, bb, g, t_new) for bb in seqs for g in groups}
    memories = [_SampleMemory(mem16_scr, sqc_ref, bb, t_new) for bb in seqs]
    sample_pieces = list(windows.values()) + memories

    def sample_finish():
        for bb in seqs:
            outs, lses = zip(*(windows[bb, g].values() for g in groups))
            _sample_finish(outs, lses, memories[bb].values(), ssgb_ref, ssgc_ref, smix_ref, bb, t_new)

    _stage_bf16(win_refs, mem_ref, kv16_scr, mem16_scr)

    x = x_ref[...]
    h = x * lax.rsqrt(jnp.mean(x * x, axis=-1, keepdims=True) + EPS) * gain_ref[...]
    h_scr[...] = h.astype(BF16)

    def project(lo, width):
        z_scr[:, lo:lo + width] = _dot(h_scr[...], w_ref[:, lo:lo + width])

    chunks = [slice(c * CHUNK, (c + 1) * CHUNK) for c in range(ROW_TILE // CHUNK)]
    nat_refs = (q0_ref, k0_ref, v0_ref)

    pm, gm, qk = {}, {}, {}

    def pm_queries():
        qs = []
        for rows in chunks:
            qc = jnp.concatenate(
                [_head_rms(z_scr[rows, OFF_QC + j * LANES:OFF_QC + (j + 1) * LANES], mqn_ref[...])
                 for j in range(MEM_WIDTH // LANES)], axis=1)
            qs.append(_stack_heads(qc, MEM_HEADS).astype(BF16))
        pm["qs"] = jnp.concatenate(qs, axis=0)

    def pm_scores():
        pm["s"] = _dot_nt(pm["qs"], mk_ref[0]) * SCALE

    def pm_softmax():
        _, e, l = _softmax_parts(pm["s"])
        pm["e"], pm["inv_l"] = e.astype(BF16), 1.0 / l

    def pm_values():
        pv = _dot(pm["e"], mv_ref[0]) * pm["inv_l"]
        per = MEM_HEADS * CHUNK
        for c, rows in enumerate(chunks):
            cm = _unstack_heads(pv[c * per:(c + 1) * per], MEM_HEADS)
            mixc_ref[rows, :] = (cm * _silu(z_scr[rows, OFF_GC:OFF_GC + MEM_WIDTH])).astype(BF16)

    def gm_prepare():
        lane = _lane_iota((CHUNK, LANES))
        for c, rows in enumerate(chunks):
            z_scr[rows, OFF_U:OFF_U + GMLP_WIDTH] = _gelu(z_scr[rows, OFF_U:OFF_U + GMLP_WIDTH])
            gv = _gelu(z_scr[rows, OFF_V:OFF_V + GMLP_WIDTH])
            mu = jnp.mean(gv, axis=-1, keepdims=True)
            dv = gv - mu
            var = jnp.mean(dv * dv, axis=-1, keepdims=True)
            vln = dv * lax.rsqrt(var + EPS) * lng_ref[...] + lnb_ref[...]
            for p in range(GMLP_WIDTH // LANES):
                vp = vln[:, p * LANES:(p + 1) * LANES]
                gm[c, p] = jnp.concatenate([jnp.where(lane < HEAD_DIM, vp, 0.0),
                                            jnp.where(lane < HEAD_DIM, 0.0, vp)], axis=0).astype(BF16)

    def gm_spatial():
        for key in list(gm):
            gm[key] = _dot(ws_ref[key[1]], gm[key])

    def gm_gate():
        for c, rows in enumerate(chunks):
            for p in range(GMLP_WIDTH // LANES):
                sl = slice(p * LANES, (p + 1) * LANES)
                s = gm[c, p] + bias_ref[:, sl]
                ga = z_scr[rows, OFF_GA + p * LANES:OFF_GA + (p + 1) * LANES]
                gu = z_scr[rows, OFF_U + p * LANES:OFF_U + (p + 1) * LANES]
                mixa_ref[rows, sl] = (gu * s * _silu(ga)).astype(BF16)

    def qk_sums():
        for c, rows in enumerate(chunks):
            for g in groups:
                for off in (OFF_Q, OFF_K):
                    x = z_scr[rows, off + g * LANES:off + (g + 1) * LANES]
                    qk[c, g, off] = _head_sum(x * x)

    def qk_normed(c, rows, g, off, norm_ref, cos, sin):
        x = z_scr[rows, off + g * LANES:off + (g + 1) * LANES]
        return _rope(x * lax.rsqrt(qk[c, g, off] * (1.0 / HEAD_DIM) + EPS) * norm_ref[...], cos, sin)

    project(OFF_QC, MEM_WIDTH)
    project(OFF_GC, MEM_WIDTH)
    for piece in sample_pieces:
        piece.scores()
    project(OFF_U, GMLP_WIDTH)
    for piece in sample_pieces:
        piece.softmax()
    pm_queries()
    project(OFF_V, GMLP_WIDTH)
    pm_scores()
    project(OFF_GA, GMLP_WIDTH)
    pm_softmax()
    gm_prepare()
    for g in groups:
        windows[0, g].shift()
    project(OFF_Q, ATTN_WIDTH)
    sample_finish()
    pm_values()
    gm_spatial()
    project(OFF_K, ATTN_WIDTH)
    gm_gate()
    project(OFF_VV, ATTN_WIDTH)
    qk_sums()
    project(OFF_GB, ATTN_WIDTH)
    for bb in seqs[1:]:
        for g in groups:
            windows[bb, g].shift()
    cos_b, sin_b = rope_b_ref[0, 0:1, :], rope_b_ref[0, 1:2, :]
    for c, rows in enumerate(chunks):
        cos_r, sin_r = rope_r_ref[0, rows, :], rope_r_ref[1, rows, :]
        cos = cos_b * cos_r - sin_b * sin_r
        sin = sin_b * cos_r + cos_b * sin_r
        for g in range(len(DIL_PAIRS)):
            q = qk_normed(c, rows, g, OFF_Q, qn_ref, cos, sin) * SCALE
            k = qk_normed(c, rows, g, OFF_K, kn_ref, cos, sin)
            v = z_scr[rows, OFF_VV + g * LANES:OFF_VV + (g + 1) * LANES]
            if g == 0:
                for ref, val in zip(nat_refs, (q, k, v)):
                    ref[0, rows, :] = val.astype(BF16)
                if rows is chunks[-1]:
                    t0_ref[0, :, 0:LANES] = k
                    t0_ref[0, :, LANES:2 * LANES] = v
            else:
                for j, val in enumerate((q, k, v)):
                    perm_scr[3 * (g - 1) + j, rows, :] = val
                tail = (t1_ref, t2_ref)[g - 1]
                tail[0, rows, 0:LANES] = k
                tail[0, rows, LANES:2 * LANES] = v

    for rows in chunks:
        sgb_ref[rows, :] = _silu(z_scr[rows, OFF_GB:OFF_GB + ATTN_WIDTH]).astype(sgb_ref.dtype)

    d1 = DIL_PAIRS[1][1]
    d2 = DIL_PAIRS[2][1]
    for j, ref in enumerate((q1_ref, k1_ref, v1_ref)):
        for r in range(d1):
            ref[0, 0, r, 0] = perm_scr[j, pl.ds(r, ROW_TILE // d1, stride=d1), :].astype(BF16)
    for j, ref in enumerate((q2_ref, k2_ref, v2_ref)):
        for r in range(d2):
            ref[0, 0, r, 0] = perm_scr[3 + j, pl.ds(r, ROW_TILE // d2, stride=d2), :].astype(BF16)


def _sample_proj_kernel(x_ref, gain_ref, w_ref, lng_ref, lnb_ref, qn_ref, kn_ref, mqn_ref,
                        cos_ref, sin_ref, ws_ref, bias_ref,
                        mixa_ref, vln_ref, q_ref, k_ref, v_ref, qc_ref, sgb_ref, sgc_ref,
                        h_scr, z_scr):
    x = x_ref[...]
    h = x * lax.rsqrt(jnp.mean(x * x, axis=-1, keepdims=True) + EPS) * gain_ref[...]
    h_scr[...] = h.astype(BF16)

    def project(lo, width):
        z_scr[:, lo:lo + width] = _dot(h_scr[...], w_ref[:, lo:lo + width])

    chunks = [slice(c * CHUNK, (c + 1) * CHUNK) for c in range(ROW_TILE // CHUNK)]
    for lo in (OFF_U, OFF_V, OFF_GA):
        project(lo, GMLP_WIDTH)
    project(OFF_Q, ATTN_WIDTH)
    for rows in chunks:
        pieces, vln = _gmlp_chunk(z_scr, rows, lng_ref, lnb_ref, ws_ref, bias_ref)
        for p, a in enumerate(pieces):
            mixa_ref[rows, p * LANES:(p + 1) * LANES] = a.astype(BF16)
        vln_ref[rows, :] = vln
    project(OFF_K, ATTN_WIDTH)
    project(OFF_VV, ATTN_WIDTH)
    for rows in chunks:
        cos, sin = cos_ref[rows, :], sin_ref[rows, :]
        for g in range(len(DIL_PAIRS)):
            sl = slice(g * LANES, (g + 1) * LANES)
            q_ref[rows, sl] = _qk_chunk(z_scr, rows, OFF_Q, g, qn_ref, cos, sin)
    project(OFF_GB, ATTN_WIDTH)
    for rows in chunks:
        cos, sin = cos_ref[rows, :], sin_ref[rows, :]
        for g in range(len(DIL_PAIRS)):
            sl = slice(g * LANES, (g + 1) * LANES)
            k_ref[rows, sl] = _qk_chunk(z_scr, rows, OFF_K, g, kn_ref, cos, sin)
            v_ref[rows, sl] = z_scr[rows, OFF_VV + g * LANES:OFF_VV + (g + 1) * LANES]
    project(OFF_QC, MEM_WIDTH)
    project(OFF_GC, MEM_WIDTH)
    for rows in chunks:
        sgb_ref[rows, :] = _silu(z_scr[rows, OFF_GB:OFF_GB + ATTN_WIDTH])
        for j in range(MEM_WIDTH // LANES):
            sl = slice(j * LANES, (j + 1) * LANES)
            qc_ref[rows, sl] = _head_rms(z_scr[rows, OFF_QC + j * LANES:OFF_QC + (j + 1) * LANES],
                                         mqn_ref[...])
        sgc_ref[rows, :] = _silu(z_scr[rows, OFF_GC:OFF_GC + MEM_WIDTH])


def _rope_tables(pos):
    inv = ROPE_THETA ** (-jnp.arange(ROT_HALF, dtype=F32) * 2.0 / ROT_DIM)
    ang = pos.astype(F32)[:, None] * inv[None, :]
    cos, sin = jnp.cos(ang), jnp.sin(ang)
    n = pos.shape[0]
    pad1 = jnp.ones((n, HEAD_DIM - ROT_DIM), F32)
    pad0 = jnp.zeros((n, HEAD_DIM - ROT_DIM), F32)
    cos_h = jnp.concatenate([cos, cos, pad1], axis=1)
    sin_h = jnp.concatenate([-sin, sin, pad0], axis=1)
    return jnp.tile(cos_h, (1, LANES // HEAD_DIM)), jnp.tile(sin_h, (1, LANES // HEAD_DIM))


def _common_proj_operands(norm_gain, w_in, ln_g, ln_b, q_norm, k_norm, mem_q_norm):
    tile2 = lambda g: jnp.tile(g, LANES // HEAD_DIM).reshape(1, LANES)
    return (norm_gain.reshape(1, D_MODEL), w_in.astype(BF16), ln_g.reshape(1, GMLP_WIDTH),
            ln_b.reshape(1, GMLP_WIDTH), tile2(q_norm), tile2(k_norm), tile2(mem_q_norm))


def _common_proj_specs(const):
    return [
        const((1, D_MODEL)), const((D_MODEL, N_IN)), const((1, GMLP_WIDTH)), const((1, GMLP_WIDTH)),
        const((1, LANES)), const((1, LANES)), const((1, LANES)),
    ]


def _pair_ws(ws):
    return jnp.concatenate([ws[0::2], ws[1::2]], axis=2).astype(BF16)


def _prompt_proj(x, common, rope_base, rope_row, ws_pair, bias, mem_k, mem_v, wins, mem_s, sample_feats, t_new):
    b, s, _ = x.shape
    n_tiles = s // ROW_TILE
    bd = mem_s.shape[0]
    assert b * n_tiles * SEQS_PER_STEP == bd
    w0, w1, w2 = (w for w, _ in DIL_PAIRS)
    d1, d2 = DIL_PAIRS[1][1], DIL_PAIRS[2][1]
    per1, per2 = w1 // ROW_TILE, w2 // ROW_TILE
    x2 = x.reshape(b * s, D_MODEL)
    const = lambda shape: pl.BlockSpec(shape, lambda i, j: (0,) * len(shape))
    rows = lambda width: pl.BlockSpec((ROW_TILE, width), lambda i, j: (i * n_tiles + j, 0))
    seq_blk = lambda a: pl.BlockSpec((SEQS_PER_STEP,) + a.shape[1:], lambda i, j: (i * n_tiles + j, 0, 0))
    seq_rows = lambda width: pl.BlockSpec((SEQS_PER_STEP * t_new, width), lambda i, j: (i * n_tiles + j, 0))
    in_specs = [rows(D_MODEL)] + _common_proj_specs(const) + [
        pl.BlockSpec((1, 2, LANES), lambda i, j: (j, 0, 0)),
        const((2, ROW_TILE, LANES)),
        const((GMLP_WIDTH // LANES, CHUNK, 2 * CHUNK)), const((CHUNK, GMLP_WIDTH)),
        pl.BlockSpec((1, N_MEM, MEM_WIDTH), lambda i, j: (i, 0, 0)),
        pl.BlockSpec((1, N_MEM, MEM_WIDTH), lambda i, j: (i, 0, 0)),
    ] + [seq_blk(w) for w in wins] + [seq_blk(mem_s)] + [seq_rows(f.shape[1]) for f in sample_feats]
    nat = pl.BlockSpec((1, ROW_TILE, LANES), lambda i, j: (i, j, 0))
    g1 = pl.BlockSpec((1, 1, d1, 1, ROW_TILE // d1, LANES), lambda i, j: (i, j // per1, 0, j % per1, 0, 0))
    g2 = pl.BlockSpec((1, 1, d2, 1, ROW_TILE // d2, LANES), lambda i, j: (i, j // per2, 0, j % per2, 0, 0))
    last = lambda n_blk: (lambda i, j: (i, jnp.maximum(j - (n_tiles - n_blk), 0), 0))
    t0 = pl.BlockSpec((1, w0, 2 * LANES), last(1))
    t1 = pl.BlockSpec((1, ROW_TILE, 2 * LANES), last(per1))
    t2 = pl.BlockSpec((1, ROW_TILE, 2 * LANES), last(per2))
    out_specs = [rows(GMLP_WIDTH), rows(MEM_WIDTH), rows(ATTN_WIDTH),
                 nat, nat, nat, g1, g1, g1, g2, g2, g2, t0, t1, t2] \
        + [seq_blk(w) for w in wins] + [seq_rows(ATTN_WIDTH + MEM_WIDTH)]
    nat_s = jax.ShapeDtypeStruct((b, s, LANES), BF16)
    g1_s = jax.ShapeDtypeStruct((b, s // w1, d1, per1, ROW_TILE // d1, LANES), BF16)
    g2_s = jax.ShapeDtypeStruct((b, s // w2, d2, per2, ROW_TILE // d2, LANES), BF16)
    out_shape = [
        jax.ShapeDtypeStruct((b * s, GMLP_WIDTH), BF16),
        jax.ShapeDtypeStruct((b * s, MEM_WIDTH), BF16),
        jax.ShapeDtypeStruct((b * s, ATTN_WIDTH), BF16),
        nat_s, nat_s, nat_s, g1_s, g1_s, g1_s, g2_s, g2_s, g2_s,
        jax.ShapeDtypeStruct((b, w0, 2 * LANES), F32),
        jax.ShapeDtypeStruct((b, w1, 2 * LANES), F32),
        jax.ShapeDtypeStruct((b, w2, 2 * LANES), F32),
    ] + [jax.ShapeDtypeStruct(w.shape, F32) for w in wins] \
      + [jax.ShapeDtypeStruct((bd * t_new, ATTN_WIDTH + MEM_WIDTH), F32)]
    return pl.pallas_call(
        functools.partial(_prompt_proj_kernel, t_new=t_new),
        grid=(b, n_tiles),
        in_specs=in_specs,
        out_specs=out_specs,
        out_shape=out_shape,
        scratch_shapes=[
            pltpu.VMEM((ROW_TILE, D_MODEL), BF16),
            pltpu.VMEM((ROW_TILE, N_IN), F32),
            pltpu.VMEM((6, ROW_TILE, LANES), F32),
            pltpu.VMEM((SEQS_PER_STEP, 2 * LANES, sum(w.shape[2] for w in wins)), BF16),
            pltpu.VMEM((SEQS_PER_STEP,) + mem_s.shape[1:], BF16),
        ],
        compiler_params=pltpu.CompilerParams(
            dimension_semantics=("arbitrary", "arbitrary"), vmem_limit_bytes=VMEM_LIMIT),
        name="prompt_proj",
    )(x2, *common, rope_base, rope_row, ws_pair, bias, mem_k, mem_v, *wins, mem_s, *sample_feats)


def _sample_proj(x, common, cos, sin, ws_pair, bias):
    n = x.shape[0]
    const = lambda shape: pl.BlockSpec(shape, lambda i: (0,) * len(shape))
    rows = lambda width: pl.BlockSpec((ROW_TILE, width), lambda i: (i, 0))
    in_specs = [rows(D_MODEL)] + _common_proj_specs(const) + [
        const((ROW_TILE, LANES)), const((ROW_TILE, LANES)),
        const((GMLP_WIDTH // LANES, CHUNK, 2 * CHUNK)), const((CHUNK, GMLP_WIDTH)),
    ]
    widths = (GMLP_WIDTH, GMLP_WIDTH, ATTN_WIDTH, ATTN_WIDTH, ATTN_WIDTH, MEM_WIDTH, ATTN_WIDTH, MEM_WIDTH)
    dtypes = (BF16,) + (F32,) * 7
    return pl.pallas_call(
        _sample_proj_kernel,
        grid=(n // ROW_TILE,),
        in_specs=in_specs,
        out_specs=[rows(w) for w in widths],
        out_shape=[jax.ShapeDtypeStruct((n, w), dt) for w, dt in zip(widths, dtypes)],
        scratch_shapes=[
            pltpu.VMEM((ROW_TILE, D_MODEL), BF16),
            pltpu.VMEM((ROW_TILE, N_IN), F32),
        ],
        compiler_params=pltpu.CompilerParams(
            dimension_semantics=("arbitrary",), vmem_limit_bytes=VMEM_LIMIT),
        name="sample_proj",
    )(x, *common, cos, sin, ws_pair, bias)


def _window_unit(q, kk, vv, bias):
    n = q.shape[0]
    lane_lo = _lane_iota(q.shape) < HEAD_DIM
    zero = jnp.zeros_like(q)
    qs = jnp.concatenate([jnp.where(lane_lo, q, zero), jnp.where(lane_lo, zero, q)], axis=0)
    d = _dot_nt(qs, kk)
    parts = [_softmax_parts(d[h * n:(h + 1) * n] + bias) for h in range(2)]
    e = jnp.concatenate([p[1].astype(BF16) for p in parts], axis=0)
    pv = _dot(e, vv)
    o = jnp.where(lane_lo, pv[:n] * (1.0 / parts[0][2]), pv[n:] * (1.0 / parts[1][2]))
    lse = [p[0] + jnp.log(p[2]) for p in parts]
    lse_b = jnp.where(lane_lo, jnp.broadcast_to(lse[0], q.shape), jnp.broadcast_to(lse[1], q.shape))
    return o, lse_b


def _prompt_attn_kernel(q0_ref, k0_ref, v0_ref, k0p_ref, v0p_ref,
                        q1_ref, k1_ref, v1_ref, k1p_ref, v1p_ref,
                        q2_ref, k2_ref, v2_ref, k2p_ref, v2p_ref,
                        sgb_ref, mix_ref, o_scr, lse_scr, bias_scr):
    m = pl.program_id(1)
    blk = CHUNK
    qo = lax.broadcasted_iota(jnp.int32, (blk, 2 * blk), 0)
    ko = lax.broadcasted_iota(jnp.int32, (blk, 2 * blk), 1)
    band = (ko >= qo) & (ko <= qo + blk)
    bias_scr[0] = jnp.where(band, 0.0, NEG)
    bias_scr[1] = jnp.where(band & (ko >= blk * (m == 0).astype(jnp.int32)), 0.0, NEG)
    band, first = 0, 1

    d1, d2 = DIL_PAIRS[1][1], DIL_PAIRS[2][1]
    w1 = DIL_PAIRS[1][0]
    for b0 in range(ATTN_TILE // blk):
        q = q0_ref[0, b0 * blk:(b0 + 1) * blk, :]
        if b0 == 0:
            kk = jnp.concatenate([k0p_ref[0], k0_ref[0, 0:blk, :]], axis=0)
            vv = jnp.concatenate([v0p_ref[0], v0_ref[0, 0:blk, :]], axis=0)
            valid = first
        else:
            kk = k0_ref[0, (b0 - 1) * blk:(b0 + 1) * blk, :]
            vv = v0_ref[0, (b0 - 1) * blk:(b0 + 1) * blk, :]
            valid = band
        o, lse = _window_unit(q, kk, vv, bias_scr[valid])
        o_scr[0, b0 * blk:(b0 + 1) * blk, :] = o
        lse_scr[0, b0 * blk:(b0 + 1) * blk, :] = lse
    for nn in range(ATTN_TILE // w1):
        for r in range(d1):
            blk_of = lambda ref, n: ref[0, n, r].reshape(blk, LANES)
            q = blk_of(q1_ref, nn)
            if nn == 0:
                kk = jnp.concatenate([blk_of(k1p_ref, 0), blk_of(k1_ref, 0)], axis=0)
                vv = jnp.concatenate([blk_of(v1p_ref, 0), blk_of(v1_ref, 0)], axis=0)
                valid = first
            else:
                kk = jnp.concatenate([blk_of(k1_ref, nn - 1), blk_of(k1_ref, nn)], axis=0)
                vv = jnp.concatenate([blk_of(v1_ref, nn - 1), blk_of(v1_ref, nn)], axis=0)
                valid = band
            o, lse = _window_unit(q, kk, vv, bias_scr[valid])
            dst = pl.ds(nn * w1 + r, blk, stride=d1)
            o_scr[1, dst, :] = o
            lse_scr[1, dst, :] = lse
    for r in range(d2):
        q = q2_ref[0, 0, r].reshape(blk, LANES)
        kk = jnp.concatenate([k2p_ref[0, 0, r].reshape(blk, LANES), k2_ref[0, 0, r].reshape(blk, LANES)], axis=0)
        vv = jnp.concatenate([v2p_ref[0, 0, r].reshape(blk, LANES), v2_ref[0, 0, r].reshape(blk, LANES)], axis=0)
        o, lse = _window_unit(q, kk, vv, bias_scr[first])
        dst = pl.ds(r, blk, stride=d2)
        o_scr[2, dst, :] = o
        lse_scr[2, dst, :] = lse

    def combine(c, carry):
        rows = pl.ds(pl.multiple_of(c * blk, blk), blk)
        ls = [lse_scr[g, rows, :] for g in range(3)]
        mx = jnp.maximum(jnp.maximum(ls[0], ls[1]), ls[2])
        es = [jnp.exp(x - mx) for x in ls]
        inv = 1.0 / (es[0] + es[1] + es[2])
        for g in range(3):
            sl = slice(g * LANES, (g + 1) * LANES)
            mix_ref[rows, sl] = (o_scr[g, rows, :] * (es[g] * inv) * sgb_ref[rows, sl]).astype(BF16)
        return carry

    lax.fori_loop(0, ATTN_TILE // blk, combine, 0)


def _prompt_attn(feats, sgb, b, s):
    q0, k0, v0, q1, k1, v1, q2, k2, v2 = feats
    w1, d1 = DIL_PAIRS[1]
    w2, d2 = DIL_PAIRS[2]
    n_tiles = s // ATTN_TILE
    per1 = ATTN_TILE // w1
    prev = lambda m, k: jnp.maximum(m * k - 1, 0)
    nat = pl.BlockSpec((1, ATTN_TILE, LANES), lambda i, m: (i, m, 0))
    nat_p = pl.BlockSpec((1, CHUNK, LANES), lambda i, m: (i, prev(m, ATTN_TILE // CHUNK), 0))
    g1 = pl.BlockSpec((1, per1) + q1.shape[2:], lambda i, m: (i, m, 0, 0, 0, 0))
    g1_p = pl.BlockSpec((1, 1) + q1.shape[2:], lambda i, m: (i, prev(m, per1), 0, 0, 0, 0))
    g2 = pl.BlockSpec((1, 1) + q2.shape[2:], lambda i, m: (i, m, 0, 0, 0, 0))
    g2_p = pl.BlockSpec((1, 1) + q2.shape[2:], lambda i, m: (i, prev(m, 1), 0, 0, 0, 0))
    rows = pl.BlockSpec((ATTN_TILE, ATTN_WIDTH), lambda i, m: (i * n_tiles + m, 0))
    return pl.pallas_call(
        _prompt_attn_kernel,
        grid=(b, n_tiles),
        in_specs=[nat, nat, nat, nat_p, nat_p, g1, g1, g1, g1_p, g1_p, g2, g2, g2, g2_p, g2_p, rows],
        out_specs=rows,
        out_shape=jax.ShapeDtypeStruct((b * s, ATTN_WIDTH), BF16),
        scratch_shapes=[
            pltpu.VMEM((3, ATTN_TILE, LANES), F32),
            pltpu.VMEM((3, ATTN_TILE, LANES), F32),
            pltpu.VMEM((2, CHUNK, 2 * CHUNK), F32),
        ],
        compiler_params=pltpu.CompilerParams(
            dimension_semantics=("arbitrary", "arbitrary"), vmem_limit_bytes=VMEM_LIMIT),
        name="prompt_attn",
    )(q0, k0, v0, k0, v0, q1, k1, v1, k1, v1, q2, k2, v2, k2, v2, sgb)


def _window_offsets(win_refs):
    offs, total = [], 0
    for r in win_refs:
        offs.append(total)
        total += r.shape[2]
    return offs, total


def _stage_bf16(win_refs, mem_ref, kv16_ref, mem16_ref):
    offs, _ = _window_offsets(win_refs)
    for bb in range(mem_ref.shape[0]):
        for wref, off in zip(win_refs, offs):
            kv16_ref[bb, :, off:off + wref.shape[2]] = wref[bb].astype(BF16)
        mem16_ref[bb] = mem_ref[bb].astype(BF16)


class _SampleWindow:
    def __init__(self, wref, oref, kv16_ref, kv16_off, q_ref, k_ref, v_ref, bb, g, t_new):
        self.wref, self.oref, self.bb, self.t_new = wref, oref, bb, t_new
        self.window, self.dil = DIL_PAIRS[g]
        self.length = wref.shape[2]
        rows = slice(bb * t_new, (bb + 1) * t_new)
        sl = slice(g * LANES, (g + 1) * LANES)
        self.load_new = lambda: (q_ref[rows, sl], k_ref[rows, sl], v_ref[rows, sl])
        self.kv16 = lambda half: kv16_ref[bb, half * LANES:(half + 1) * LANES, kv16_off:kv16_off + self.length]

    def scores(self):
        t_new = self.t_new
        qg, kg, vg = self.load_new()
        self.lane_lo = _lane_iota(qg.shape) < HEAD_DIM
        qs = jnp.concatenate([jnp.where(self.lane_lo, qg, 0.0), jnp.where(self.lane_lo, 0.0, qg)],
                             axis=0).astype(BF16)
        zpad = jnp.zeros((CHUNK - t_new, LANES), F32)
        self.k_new = jnp.concatenate([kg, zpad], axis=0)
        self.v_new = jnp.concatenate([vg, zpad], axis=0)
        self.s_old = _dot(qs, self.kv16(0)) * SCALE
        self.s_new = _dot_nt(qs, self.k_new.astype(BF16)) * SCALE

    def softmax(self):
        t_new, dil, window = self.t_new, self.dil, self.window
        s_old, s_new = self.s_old, self.s_new
        tq = lax.broadcasted_iota(jnp.int32, s_old.shape, 0) & (t_new - 1)
        dist = self.length + tq - lax.broadcasted_iota(jnp.int32, s_old.shape, 1)
        ok_old = ((dist & (dil - 1)) == 0) & (dist <= window)
        tqn = lax.broadcasted_iota(jnp.int32, s_new.shape, 0) & (t_new - 1)
        dn = tqn - lax.broadcasted_iota(jnp.int32, s_new.shape, 1)
        ok_new = (dn >= 0) & ((dn & (dil - 1)) == 0) & (dn <= window)
        s_old = jnp.where(ok_old, s_old, NEG)
        s_new = jnp.where(ok_new, s_new, NEG)
        mx = jnp.maximum(jnp.max(s_old, axis=-1, keepdims=True), jnp.max(s_new, axis=-1, keepdims=True))
        e_old = jnp.exp(s_old - mx)
        e_new = jnp.exp(s_new - mx)
        l = jnp.sum(e_old, axis=-1, keepdims=True) + jnp.sum(e_new, axis=-1, keepdims=True)
        self.e_old, self.e_new = e_old.astype(BF16), e_new.astype(BF16)
        self.inv_l = 1.0 / l
        self.lse = mx + jnp.log(l)

    def values(self):
        t_new, lane_lo = self.t_new, self.lane_lo
        pv = (_dot_nt(self.e_old, self.kv16(1)) + _dot(self.e_new, self.v_new.astype(BF16))) * self.inv_l
        out = jnp.where(lane_lo, pv[:t_new], pv[t_new:])
        shape = (t_new, LANES)
        lse_b = jnp.where(lane_lo, jnp.broadcast_to(self.lse[:t_new], shape),
                          jnp.broadcast_to(self.lse[t_new:], shape))
        return out, lse_b

    def shift(self):
        t_new, wref, oref, bb = self.t_new, self.wref, self.oref, self.bb
        _, kg, vg = self.load_new()
        zpad = jnp.zeros((CHUNK - t_new, LANES), F32)
        new_t = jnp.concatenate([jnp.concatenate([kg, zpad], axis=0).T,
                                 jnp.concatenate([vg, zpad], axis=0).T], axis=0)
        keep = _lane_iota((2 * LANES, LANES)) < LANES - t_new
        nxt = pltpu.roll(wref[bb, :, 0:LANES], LANES - t_new, 1)
        n_blk = self.length // LANES
        for c in range(n_blk):
            cur = nxt
            if c + 1 < n_blk:
                nxt = pltpu.roll(wref[bb, :, (c + 1) * LANES:(c + 2) * LANES], LANES - t_new, 1)
            else:
                nxt = pltpu.roll(new_t, LANES - t_new, 1)
            oref[bb, :, c * LANES:(c + 1) * LANES] = jnp.where(keep, cur, nxt)


class _SampleMemory:
    def __init__(self, mem16_ref, qc_ref, bb, t_new):
        self.rows = slice(bb * t_new, (bb + 1) * t_new)
        self.qc_ref = qc_ref
        self.half = lambda h: mem16_ref[bb, h * MEM_WIDTH:(h + 1) * MEM_WIDTH, :]

    def scores(self):
        self.s = _dot(_stack_heads(self.qc_ref[self.rows, :], MEM_HEADS).astype(BF16), self.half(0)) * SCALE

    def softmax(self):
        _, e, l = _softmax_parts(self.s)
        self.e, self.inv_l = e.astype(BF16), 1.0 / l

    def values(self):
        return _unstack_heads(_dot_nt(self.e, self.half(1)) * self.inv_l, MEM_HEADS)


def _sample_finish(outs, lses, cm, sgb_ref, sgc_ref, mix_ref, bb, t_new):
    rows = slice(bb * t_new, (bb + 1) * t_new)
    mx = jnp.maximum(jnp.maximum(lses[0], lses[1]), lses[2])
    es = [jnp.exp(x - mx) for x in lses]
    inv = 1.0 / (es[0] + es[1] + es[2])
    for g in range(len(DIL_PAIRS)):
        sl = slice(g * LANES, (g + 1) * LANES)
        mix_ref[rows, sl] = outs[g] * (es[g] * inv) * sgb_ref[rows, sl]
    mix_ref[rows, ATTN_WIDTH:ATTN_WIDTH + MEM_WIDTH] = cm * sgc_ref[rows, :]


def _out_proj_kernel(x_ref, *refs):
    *mix_refs, w_ref, y_ref = refs
    mix = jnp.concatenate([r[...].astype(BF16) for r in mix_refs], axis=1)
    y_ref[...] = x_ref[...] + _dot(mix, w_ref[...].astype(BF16))


def _out_proj(x, mixes, w_out):
    n = x.shape[0]
    rows = lambda width: pl.BlockSpec((OUT_TILE, width), lambda i: (i, 0))
    return pl.pallas_call(
        _out_proj_kernel,
        grid=(n // OUT_TILE,),
        in_specs=[rows(D_MODEL)] + [rows(mx.shape[1]) for mx in mixes]
        + [pl.BlockSpec((MIX_WIDTH, D_MODEL), lambda i: (0, 0))],
        out_specs=rows(D_MODEL),
        out_shape=jax.ShapeDtypeStruct((n, D_MODEL), F32),
        compiler_params=pltpu.CompilerParams(
            dimension_semantics=("arbitrary",), vmem_limit_bytes=VMEM_LIMIT),
        name="out_proj",
    )(x, *mixes, w_out)


def kernel(x_prompt, x_sample, state_win0_kv, state_win1_kv, state_win2_kv, cache_mem_kv, mem_prompt, norm_gain, w_in, gmlp_ln_gain, gmlp_ln_bias, gmlp_w_s, gmlp_b_s, attn_q_norm, attn_k_norm, mem_norm, w_mem_kv, mem_q_norm, mem_k_norm, w_out):
    depth = norm_gain.shape[0]
    assert depth == 1, "single-layer step only"
    bp, s, _ = x_prompt.shape
    bd, t, _ = x_sample.shape
    past_len = PAST_LEN
    assert s % ATTN_TILE == 0 and (bd * t) % OUT_TILE == 0 and (bp * s) % OUT_TILE == 0
    assert CHUNK % t == 0 and t & (t - 1) == 0 and DIL_PAIRS[1][0] % ROW_TILE == 0
    states = (state_win0_kv, state_win1_kv, state_win2_kv)
    for st, (w, d) in zip(states, DIL_PAIRS):
        assert st.shape[2] == w and w // d == CHUNK and d & (d - 1) == 0

    common = _common_proj_operands(norm_gain[0], w_in[0], gmlp_ln_gain[0], gmlp_ln_bias[0],
                                   attn_q_norm[0], attn_k_norm[0], mem_q_norm[0])
    tril = jnp.tril(jnp.ones((CHUNK, CHUNK), dtype=bool))
    ws = jnp.where(tril[None], gmlp_w_s[0], 0.0)
    b_s = gmlp_b_s[0]

    reps = CHUNK // t
    ws_s = jnp.stack([jnp.kron(jnp.eye(reps, dtype=F32), ws[g, :t, :t]) for g in range(N_GMLP_GROUPS)])
    bias_s = jnp.repeat(jnp.tile(b_s[:, :t], (1, reps)).T, HEAD_DIM, axis=1)
    cos_s, sin_s = _rope_tables(past_len + (jnp.arange(ROW_TILE, dtype=jnp.int32) % t))
    mixa_s, vln_s, q_s, k_s, v_s, qc_s, sgb_s, sgc_s = _sample_proj(
        x_sample.reshape(bd * t, D_MODEL), common, cos_s, sin_s, _pair_ws(ws_s), bias_s)
    feat_major = lambda a: jnp.transpose(a, (0, 2, 3, 4, 1)).reshape(a.shape[0], -1, a.shape[1])
    wins = [feat_major(st[0]) for st in states]
    mem_s = feat_major(cache_mem_kv[0])

    mem_kv_p, mem_k, mem_v = _mem_kv(mem_prompt, mem_norm[0], w_mem_kv[0], mem_k_norm[0])
    rope_base = jnp.stack(_rope_tables(jnp.arange(0, s, ROW_TILE, dtype=jnp.int32)), axis=1)
    rope_row = jnp.stack(_rope_tables(jnp.arange(ROW_TILE, dtype=jnp.int32)), axis=0)
    bias_p = jnp.repeat(b_s.T, HEAD_DIM, axis=1)
    outs = _prompt_proj(x_prompt, common, rope_base, rope_row, _pair_ws(ws), bias_p, mem_k, mem_v,
                        wins, mem_s, (q_s, k_s, v_s, qc_s, sgb_s, sgc_s), t)
    mixa_p, mixc_p, sgb_p = outs[0:3]
    tails = outs[12:15]
    new_wins_t = outs[15:19]
    mixb_p = _prompt_attn(outs[3:12], sgb_p, bp, s)
    y_prompt = _out_proj(x_prompt.reshape(bp * s, D_MODEL), (mixa_p, mixb_p, mixc_p), w_out[0])

    mixbc_s = new_wins_t[3]
    heads = LANES // HEAD_DIM
    nw0, nw1, nw2 = [
        jnp.transpose(w.reshape(bd, 2, heads, HEAD_DIM, w.shape[2]), (0, 4, 1, 2, 3)) for w in new_wins_t[:3]]
    y_sample = _out_proj(x_sample.reshape(bd * t, D_MODEL), (mixa_s, mixbc_s), w_out[0])

    win_shape = lambda n, w: (1, n, w, 2, heads, HEAD_DIM)
    return (
        y_prompt.reshape(bp, s, D_MODEL),
        y_sample.reshape(bd, t, D_MODEL),
        tails[0].reshape(win_shape(bp, DIL_PAIRS[0][0])),
        tails[1].reshape(win_shape(bp, DIL_PAIRS[1][0])),
        tails[2].reshape(win_shape(bp, DIL_PAIRS[2][0])),
        mem_kv_p.reshape(1, bp, N_MEM, 2, MEM_HEADS, HEAD_DIM),
        nw0.reshape(win_shape(bd, DIL_PAIRS[0][0])),
        nw1.reshape(win_shape(bd, DIL_PAIRS[1][0])),
        nw2.reshape(win_shape(bd, DIL_PAIRS[2][0])),
        vln_s.reshape(1, bd, t, N_GMLP_GROUPS, HEAD_DIM),
    )
```

```python
import functools

import numpy as np
import jax
import jax.numpy as jnp
from jax import lax
from jax.experimental import pallas as pl
from jax.experimental.pallas import tpu as pltpu

F32 = jnp.float32
BF16 = jnp.bfloat16

D_MODEL = 1024
HEAD_DIM = 64
LANES = 128
N_GMLP_GROUPS = 6
GMLP_WIDTH = N_GMLP_GROUPS * HEAD_DIM
CHUNK = 128
DIL_PAIRS = ((128, 1), (512, 4), (2048, 16))
ATTN_WIDTH = len(DIL_PAIRS) * LANES
N_MEM = 256
PAST_LEN = 8192
MEM_HEADS = 4
MEM_WIDTH = MEM_HEADS * HEAD_DIM
MIX_WIDTH = GMLP_WIDTH + ATTN_WIDTH + MEM_WIDTH
N_IN = 3 * GMLP_WIDTH + 4 * ATTN_WIDTH + 2 * MEM_WIDTH
ROPE_THETA = 500000.0
ROT_DIM = HEAD_DIM // 4
ROT_HALF = ROT_DIM // 2
EPS = 1e-6
NEG = -1e30
SCALE = HEAD_DIM ** -0.5
assert np.frexp(SCALE)[0] == 0.5, "the prompt path folds SCALE into q, exact only for a power of two"

OFF_U = 0
OFF_V = OFF_U + GMLP_WIDTH
OFF_GA = OFF_V + GMLP_WIDTH
OFF_Q = OFF_GA + GMLP_WIDTH
OFF_K = OFF_Q + ATTN_WIDTH
OFF_VV = OFF_K + ATTN_WIDTH
OFF_GB = OFF_VV + ATTN_WIDTH
OFF_QC = OFF_GB + ATTN_WIDTH
OFF_GC = OFF_QC + MEM_WIDTH

ROW_TILE = 256
SAMPLE_TILE = 512
ATTN_TILE = 2048
SEQS_PER_STEP = 2
OUT_TILE = 2048
VMEM_LIMIT = 56 * 1024 * 1024


def _dot(a, b):
    return jnp.dot(a, b, preferred_element_type=F32)


def _dot_nt(a, b):
    return lax.dot_general(a, b, (((1,), (1,)), ((), ())), preferred_element_type=F32)


def _silu(x):
    return x * (1.0 / (1.0 + jnp.exp(-x)))


def _gelu(x):
    return 0.5 * x * (1.0 + lax.erf(x * np.float32(np.sqrt(0.5))))


def _lane_iota(shape):
    return lax.broadcasted_iota(jnp.int32, shape, len(shape) - 1)


def _head_sum(ss):
    r = (lax.broadcasted_iota(jnp.int32, (2 * LANES, LANES), 0) >> 6) & 1
    c = lax.broadcasted_iota(jnp.int32, (2 * LANES, LANES), 1) >> 6
    ones_blk = jnp.where(r == c, 1.0, 0.0).astype(BF16)
    hi = ss.astype(BF16)
    lo = (ss - hi.astype(F32)).astype(BF16)
    return _dot(jnp.concatenate([hi, lo], axis=1), ones_blk)


def _head_rms(x, gain):
    ms = _head_sum(x * x) * (1.0 / HEAD_DIM)
    return x * lax.rsqrt(ms + EPS) * gain


def _rope(x, cos, sin):
    lane = _lane_iota(x.shape) & (HEAD_DIM - 1)
    partner = jnp.where(lane < ROT_HALF, pltpu.roll(x, LANES - ROT_HALF, 1), pltpu.roll(x, ROT_HALF, 1))
    return x * cos + partner * sin


def _softmax_parts(s, axis=-1):
    mx = jnp.max(s, axis=axis, keepdims=True)
    e = jnp.exp(s - mx)
    return mx, e, jnp.sum(e, axis=axis, keepdims=True)


def _mem_kv_kernel(mem_ref, norm_ref, w_ref, kn_ref, kv_ref, k_ref, v_ref):
    x = mem_ref[0]
    h = x * lax.rsqrt(jnp.mean(x * x, axis=-1, keepdims=True) + EPS) * norm_ref[...]
    kv = _dot(h.astype(BF16), w_ref[...].astype(BF16))
    for c in range(MEM_WIDTH // LANES):
        sl = slice(c * LANES, (c + 1) * LANES)
        kc = _head_rms(kv[:, sl], kn_ref[...])
        kv_ref[0, :, sl] = kc
        k_ref[0, :, sl] = kc.astype(BF16)
    v = kv[:, MEM_WIDTH:]
    kv_ref[0, :, MEM_WIDTH:] = v
    v_ref[0] = v.astype(BF16)


def _mem_kv(mem, mem_norm, w_mem_kv, mem_k_norm):
    b = mem.shape[0]
    return pl.pallas_call(
        _mem_kv_kernel,
        grid=(b,),
        in_specs=[
            pl.BlockSpec((1, N_MEM, D_MODEL), lambda i: (i, 0, 0)),
            pl.BlockSpec((1, D_MODEL), lambda i: (0, 0)),
            pl.BlockSpec((D_MODEL, 2 * MEM_WIDTH), lambda i: (0, 0)),
            pl.BlockSpec((1, LANES), lambda i: (0, 0)),
        ],
        out_specs=[
            pl.BlockSpec((1, N_MEM, 2 * MEM_WIDTH), lambda i: (i, 0, 0)),
            pl.BlockSpec((1, N_MEM, MEM_WIDTH), lambda i: (i, 0, 0)),
            pl.BlockSpec((1, N_MEM, MEM_WIDTH), lambda i: (i, 0, 0)),
        ],
        out_shape=[
            jax.ShapeDtypeStruct((b, N_MEM, 2 * MEM_WIDTH), F32),
            jax.ShapeDtypeStruct((b, N_MEM, MEM_WIDTH), BF16),
            jax.ShapeDtypeStruct((b, N_MEM, MEM_WIDTH), BF16),
        ],
        name="mem_kv",
    )(mem, mem_norm.reshape(1, D_MODEL), w_mem_kv,
      jnp.tile(mem_k_norm, LANES // HEAD_DIM).reshape(1, LANES))


def _gmlp_chunk(z_scr, rows, lng_ref, lnb_ref, ws_ref, bias_ref):
    gu = _gelu(z_scr[rows, OFF_U:OFF_U + GMLP_WIDTH])
    gv = _gelu(z_scr[rows, OFF_V:OFF_V + GMLP_WIDTH])
    mu = jnp.mean(gv, axis=-1, keepdims=True)
    dv = gv - mu
    var = jnp.mean(dv * dv, axis=-1, keepdims=True)
    vln = dv * lax.rsqrt(var + EPS) * lng_ref[...] + lnb_ref[...]
    lane = _lane_iota((CHUNK, LANES))
    pieces = []
    for p in range(GMLP_WIDTH // LANES):
        sl = slice(p * LANES, (p + 1) * LANES)
        vp = vln[:, sl]
        rhs = jnp.concatenate([jnp.where(lane < HEAD_DIM, vp, 0.0),
                               jnp.where(lane < HEAD_DIM, 0.0, vp)], axis=0).astype(BF16)
        s = _dot(ws_ref[p], rhs) + bias_ref[:, sl]
        ga = z_scr[rows, OFF_GA + p * LANES:OFF_GA + (p + 1) * LANES]
        pieces.append(gu[:, sl] * s * _silu(ga))
    return pieces, vln


def _qk_chunk(z_scr, rows, off, g, norm_ref, cos, sin):
    x = z_scr[rows, off + g * LANES:off + (g + 1) * LANES]
    return _rope(_head_rms(x, norm_ref[...]), cos, sin)


def _stack_heads(x, n_heads):
    head = _lane_iota(x.shape) >> 6
    return jnp.concatenate([jnp.where(head == h, x, 0.0) for h in range(n_heads)], axis=0)


def _unstack_heads(y, n_heads):
    r = y.shape[0] // n_heads
    head = _lane_iota((r, y.shape[1])) >> 6
    out = y[0:r]
    for h in range(1, n_heads):
        out = jnp.where(head == h, y[h * r:(h + 1) * r], out)
    return out


def _mem_attend(qc, mk, mv):
    s = _dot_nt(_stack_heads(qc, MEM_HEADS).astype(BF16), mk) * SCALE
    _, e, l = _softmax_parts(s)
    pv = _dot(e.astype(BF16), mv) * (1.0 / l)
    return _unstack_heads(pv, MEM_HEADS)


def _prompt_proj_kernel(x_ref, gain_ref, w_ref, lng_ref, lnb_ref, qn_ref, kn_ref, mqn_ref,
                        rope_b_ref, rope_r_ref, ws_ref, bias_ref, mk_ref, mv_ref,
                        w0_ref, w1_ref, w2_ref, mem_ref, sq_ref, sk_ref, sv_ref, sqc_ref, ssgb_ref, ssgc_ref,
                        mixa_ref, mixc_ref, sgb_ref,
                        q0_ref, k0_ref, v0_ref, q1_ref, k1_ref, v1_ref, q2_ref, k2_ref, v2_ref,
                        t0_ref, t1_ref, t2_ref,
                        o0_ref, o1_ref, o2_ref, smix_ref,
                        h_scr, z_scr, perm_scr, kv16_scr, mem16_scr, *, t_new):
    win_refs, out_refs = (w0_ref, w1_ref, w2_ref), (o0_ref, o1_ref, o2_ref)
    kv16_offs, _ = _window_offsets(win_refs)
    seqs = range(SEQS_PER_STEP)
    groups = range(len(DIL_PAIRS))
    windows = {(bb, g): _SampleWindow(win_refs[g], out_refs[g], kv16_scr, kv16_offs[g],
                                      sq_ref, sk_ref, sv_ref, bb, g, t_new) for bb in seqs for g in groups}
    memories = [_SampleMemory(mem16_scr, sqc_ref, bb, t_new) for bb in seqs]
    sample_pieces = list(windows.values()) + memories

    def sample_finish():
        for bb in seqs:
            outs, lses = zip(*(windows[bb, g].values() for g in groups))
            _sample_finish(outs, lses, memories[bb].values(), ssgb_ref, ssgc_ref, smix_ref, bb, t_new)

    _stage_bf16(win_refs, mem_ref, kv16_scr, mem16_scr)

    x = x_ref[...]
    h = x * lax.rsqrt(jnp.mean(x * x, axis=-1, keepdims=True) + EPS) * gain_ref[...]
    h_scr[...] = h.astype(BF16)

    def project(lo, width):
        z_scr[:, lo:lo + width] = _dot(h_scr[...], w_ref[:, lo:lo + width])

    chunks = [slice(c * CHUNK, (c + 1) * CHUNK) for c in range(ROW_TILE // CHUNK)]
    nat_refs = (q0_ref, k0_ref, v0_ref)

    pm, gm, qk = {}, {}, {}

    def pm_queries():
        qs = []
        for rows in chunks:
            qc = jnp.concatenate(
                [_head_rms(z_scr[rows, OFF_QC + j * LANES:OFF_QC + (j + 1) * LANES], mqn_ref[...])
                 for j in range(MEM_WIDTH // LANES)], axis=1)
            qs.append(_stack_heads(qc, MEM_HEADS).astype(BF16))
        pm["qs"] = jnp.concatenate(qs, axis=0)

    def pm_scores():
        pm["s"] = _dot_nt(pm["qs"], mk_ref[0]) * SCALE

    def pm_softmax():
        _, e, l = _softmax_parts(pm["s"])
        pm["e"], pm["inv_l"] = e.astype(BF16), 1.0 / l

    def pm_values():
        pv = _dot(pm["e"], mv_ref[0]) * pm["inv_l"]
        per = MEM_HEADS * CHUNK
        for c, rows in enumerate(chunks):
            cm = _unstack_heads(pv[c * per:(c + 1) * per], MEM_HEADS)
            mixc_ref[rows, :] = (cm * _silu(z_scr[rows, OFF_GC:OFF_GC + MEM_WIDTH])).astype(BF16)

    def gm_prepare():
        lane = _lane_iota((CHUNK, LANES))
        for c, rows in enumerate(chunks):
            z_scr[rows, OFF_U:OFF_U + GMLP_WIDTH] = _gelu(z_scr[rows, OFF_U:OFF_U + GMLP_WIDTH])
            gv = _gelu(z_scr[rows, OFF_V:OFF_V + GMLP_WIDTH])
            mu = jnp.mean(gv, axis=-1, keepdims=True)
            dv = gv - mu
            var = jnp.mean(dv * dv, axis=-1, keepdims=True)
            vln = dv * lax.rsqrt(var + EPS) * lng_ref[...] + lnb_ref[...]
            for p in range(GMLP_WIDTH // LANES):
                vp = vln[:, p * LANES:(p + 1) * LANES]
                gm[c, p] = jnp.concatenate([jnp.where(lane < HEAD_DIM, vp, 0.0),
                                            jnp.where(lane < HEAD_DIM, 0.0, vp)], axis=0).astype(BF16)

    def gm_spatial():
        for key in list(gm):
            gm[key] = _dot(ws_ref[key[1]], gm[key])

    def gm_gate():
        for c, rows in enumerate(chunks):
            for p in range(GMLP_WIDTH // LANES):
                sl = slice(p * LANES, (p + 1) * LANES)
                s = gm[c, p] + bias_ref[:, sl]
                ga = z_scr[rows, OFF_GA + p * LANES:OFF_GA + (p + 1) * LANES]
                gu = z_scr[rows, OFF_U + p * LANES:OFF_U + (p + 1) * LANES]
                mixa_ref[rows, sl] = (gu * s * _silu(ga)).astype(BF16)

    def qk_sums():
        for c, rows in enumerate(chunks):
            for g in groups:
                for off in (OFF_Q, OFF_K):
                    x = z_scr[rows, off + g * LANES:off + (g + 1) * LANES]
                    qk[c, g, off] = _head_sum(x * x)

    def qk_normed(c, rows, g, off, norm_ref, cos, sin):
        x = z_scr[rows, off + g * LANES:off + (g + 1) * LANES]
        return _rope(x * lax.rsqrt(qk[c, g, off] * (1.0 / HEAD_DIM) + EPS) * norm_ref[...], cos, sin)

    project(OFF_QC, MEM_WIDTH)
    project(OFF_GC, MEM_WIDTH)
    for piece in sample_pieces:
        piece.scores()
    project(OFF_U, GMLP_WIDTH)
    for piece in sample_pieces:
        piece.softmax()
    pm_queries()
    project(OFF_V, GMLP_WIDTH)
    pm_scores()
    project(OFF_GA, GMLP_WIDTH)
    pm_softmax()
    gm_prepare()
    for g in groups:
        windows[0, g].shift()
    project(OFF_Q, ATTN_WIDTH)
    sample_finish()
    pm_values()
    gm_spatial()
    project(OFF_K, ATTN_WIDTH)
    gm_gate()
    project(OFF_VV, ATTN_WIDTH)
    qk_sums()
    project(OFF_GB, ATTN_WIDTH)
    for bb in seqs[1:]:
        for g in groups:
            windows[bb, g].shift()
    cos_b, sin_b = rope_b_ref[0, 0:1, :], rope_b_ref[0, 1:2, :]
    for c, rows in enumerate(chunks):
        cos_r, sin_r = rope_r_ref[0, rows, :], rope_r_ref[1, rows, :]
        cos = cos_b * cos_r - sin_b * sin_r
        sin = sin_b * cos_r + cos_b * sin_r
        for g in range(len(DIL_PAIRS)):
            q = qk_normed(c, rows, g, OFF_Q, qn_ref, cos, sin) * SCALE
            k = qk_normed(c, rows, g, OFF_K, kn_ref, cos, sin)
            v = z_scr[rows, OFF_VV + g * LANES:OFF_VV + (g + 1) * LANES]
            if g == 0:
                for ref, val in zip(nat_refs, (q, k, v)):
                    ref[0, rows, :] = val.astype(BF16)
                if rows is chunks[-1]:
                    t0_ref[0, :, 0:LANES] = k
                    t0_ref[0, :, LANES:2 * LANES] = v
            else:
                for j, val in enumerate((q, k, v)):
                    perm_scr[3 * (g - 1) + j, rows, :] = val
                tail = (t1_ref, t2_ref)[g - 1]
                tail[0, rows, 0:LANES] = k
                tail[0, rows, LANES:2 * LANES] = v

    for rows in chunks:
        sgb_ref[rows, :] = _silu(z_scr[rows, OFF_GB:OFF_GB + ATTN_WIDTH]).astype(sgb_ref.dtype)

    d1 = DIL_PAIRS[1][1]
    d2 = DIL_PAIRS[2][1]
    for j, ref in enumerate((q1_ref, k1_ref, v1_ref)):
        for r in range(d1):
            ref[0, 0, r, 0] = perm_scr[j, pl.ds(r, ROW_TILE // d1, stride=d1), :].astype(BF16)
    for j, ref in enumerate((q2_ref, k2_ref, v2_ref)):
        for r in range(d2):
            ref[0, 0, r, 0] = perm_scr[3 + j, pl.ds(r, ROW_TILE // d2, stride=d2), :].astype(BF16)


def _sample_proj_kernel(x_ref, gain_ref, w_ref, lng_ref, lnb_ref, qn_ref, kn_ref, mqn_ref,
                        cos_ref, sin_ref, ws_ref, bias_ref,
                        mixa_ref, vln_ref, q_ref, k_ref, v_ref, qc_ref, sgb_ref, sgc_ref,
                        h_scr, z_scr):
    x = x_ref[...]
    h = x * lax.rsqrt(jnp.mean(x * x, axis=-1, keepdims=True) + EPS) * gain_ref[...]
    h_scr[...] = h.astype(BF16)

    def project(lo, width):
        z_scr[:, lo:lo + width] = _dot(h_scr[...], w_ref[:, lo:lo + width])

    chunks = [slice(c * CHUNK, (c + 1) * CHUNK) for c in range(x_ref.shape[0] // CHUNK)]
    for lo in (OFF_U, OFF_V, OFF_GA):
        project(lo, GMLP_WIDTH)
    project(OFF_Q, ATTN_WIDTH)
    for rows in chunks:
        pieces, vln = _gmlp_chunk(z_scr, rows, lng_ref, lnb_ref, ws_ref, bias_ref)
        for p, a in enumerate(pieces):
            mixa_ref[rows, p * LANES:(p + 1) * LANES] = a.astype(BF16)
        vln_ref[rows, :] = vln
    project(OFF_K, ATTN_WIDTH)
    project(OFF_VV, ATTN_WIDTH)
    for rows in chunks:
        cos, sin = cos_ref[rows, :], sin_ref[rows, :]
        for g in range(len(DIL_PAIRS)):
            sl = slice(g * LANES, (g + 1) * LANES)
            q_ref[rows, sl] = _qk_chunk(z_scr, rows, OFF_Q, g, qn_ref, cos, sin)
    project(OFF_GB, ATTN_WIDTH)
    for rows in chunks:
        cos, sin = cos_ref[rows, :], sin_ref[rows, :]
        for g in range(len(DIL_PAIRS)):
            sl = slice(g * LANES, (g + 1) * LANES)
            k_ref[rows, sl] = _qk_chunk(z_scr, rows, OFF_K, g, kn_ref, cos, sin)
            v_ref[rows, sl] = z_scr[rows, OFF_VV + g * LANES:OFF_VV + (g + 1) * LANES]
    project(OFF_QC, MEM_WIDTH)
    project(OFF_GC, MEM_WIDTH)
    for rows in chunks:
        sgb_ref[rows, :] = _silu(z_scr[rows, OFF_GB:OFF_GB + ATTN_WIDTH])
        for j in range(MEM_WIDTH // LANES):
            sl = slice(j * LANES, (j + 1) * LANES)
            qc_ref[rows, sl] = _head_rms(z_scr[rows, OFF_QC + j * LANES:OFF_QC + (j + 1) * LANES],
                                         mqn_ref[...])
        sgc_ref[rows, :] = _silu(z_scr[rows, OFF_GC:OFF_GC + MEM_WIDTH])


def _rope_tables(pos):
    inv = ROPE_THETA ** (-jnp.arange(ROT_HALF, dtype=F32) * 2.0 / ROT_DIM)
    ang = pos.astype(F32)[:, None] * inv[None, :]
    cos, sin = jnp.cos(ang), jnp.sin(ang)
    n = pos.shape[0]
    pad1 = jnp.ones((n, HEAD_DIM - ROT_DIM), F32)
    pad0 = jnp.zeros((n, HEAD_DIM - ROT_DIM), F32)
    cos_h = jnp.concatenate([cos, cos, pad1], axis=1)
    sin_h = jnp.concatenate([-sin, sin, pad0], axis=1)
    return jnp.tile(cos_h, (1, LANES // HEAD_DIM)), jnp.tile(sin_h, (1, LANES // HEAD_DIM))


def _common_proj_operands(norm_gain, w_in, ln_g, ln_b, q_norm, k_norm, mem_q_norm):
    tile2 = lambda g: jnp.tile(g, LANES // HEAD_DIM).reshape(1, LANES)
    return (norm_gain.reshape(1, D_MODEL), w_in.astype(BF16), ln_g.reshape(1, GMLP_WIDTH),
            ln_b.reshape(1, GMLP_WIDTH), tile2(q_norm), tile2(k_norm), tile2(mem_q_norm))


def _common_proj_specs(const):
    return [
        const((1, D_MODEL)), const((D_MODEL, N_IN)), const((1, GMLP_WIDTH)), const((1, GMLP_WIDTH)),
        const((1, LANES)), const((1, LANES)), const((1, LANES)),
    ]


def _pair_ws(ws):
    return jnp.concatenate([ws[0::2], ws[1::2]], axis=2).astype(BF16)


def _prompt_proj(x, common, rope_base, rope_row, ws_pair, bias, mem_k, mem_v, wins, mem_s, sample_feats, t_new):
    b, s, _ = x.shape
    n_tiles = s // ROW_TILE
    bd = mem_s.shape[0]
    assert b * n_tiles * SEQS_PER_STEP == bd
    w0, w1, w2 = (w for w, _ in DIL_PAIRS)
    d1, d2 = DIL_PAIRS[1][1], DIL_PAIRS[2][1]
    per1, per2 = w1 // ROW_TILE, w2 // ROW_TILE
    x2 = x.reshape(b * s, D_MODEL)
    const = lambda shape: pl.BlockSpec(shape, lambda i, j: (0,) * len(shape))
    rows = lambda width: pl.BlockSpec((ROW_TILE, width), lambda i, j: (i * n_tiles + j, 0))
    seq_blk = lambda a: pl.BlockSpec((SEQS_PER_STEP,) + a.shape[1:], lambda i, j: (i * n_tiles + j, 0, 0))
    seq_rows = lambda width: pl.BlockSpec((SEQS_PER_STEP * t_new, width), lambda i, j: (i * n_tiles + j, 0))
    in_specs = [rows(D_MODEL)] + _common_proj_specs(const) + [
        pl.BlockSpec((1, 2, LANES), lambda i, j: (j, 0, 0)),
        const((2, ROW_TILE, LANES)),
        const((GMLP_WIDTH // LANES, CHUNK, 2 * CHUNK)), const((CHUNK, GMLP_WIDTH)),
        pl.BlockSpec((1, N_MEM, MEM_WIDTH), lambda i, j: (i, 0, 0)),
        pl.BlockSpec((1, N_MEM, MEM_WIDTH), lambda i, j: (i, 0, 0)),
    ] + [seq_blk(w) for w in wins] + [seq_blk(mem_s)] + [seq_rows(f.shape[1]) for f in sample_feats]
    nat = pl.BlockSpec((1, ROW_TILE, LANES), lambda i, j: (i, j, 0))
    g1 = pl.BlockSpec((1, 1, d1, 1, ROW_TILE // d1, LANES), lambda i, j: (i, j // per1, 0, j % per1, 0, 0))
    g2 = pl.BlockSpec((1, 1, d2, 1, ROW_TILE // d2, LANES), lambda i, j: (i, j // per2, 0, j % per2, 0, 0))
    last = lambda n_blk: (lambda i, j: (i, jnp.maximum(j - (n_tiles - n_blk), 0), 0))
    t0 = pl.BlockSpec((1, w0, 2 * LANES), last(1))
    t1 = pl.BlockSpec((1, ROW_TILE, 2 * LANES), last(per1))
    t2 = pl.BlockSpec((1, ROW_TILE, 2 * LANES), last(per2))
    out_specs = [rows(GMLP_WIDTH), rows(MEM_WIDTH), rows(ATTN_WIDTH),
                 nat, nat, nat, g1, g1, g1, g2, g2, g2, t0, t1, t2] \
        + [seq_blk(w) for w in wins] + [seq_rows(ATTN_WIDTH + MEM_WIDTH)]
    nat_s = jax.ShapeDtypeStruct((b, s, LANES), BF16)
    g1_s = jax.ShapeDtypeStruct((b, s // w1, d1, per1, ROW_TILE // d1, LANES), BF16)
    g2_s = jax.ShapeDtypeStruct((b, s // w2, d2, per2, ROW_TILE // d2, LANES), BF16)
    out_shape = [
        jax.ShapeDtypeStruct((b * s, GMLP_WIDTH), BF16),
        jax.ShapeDtypeStruct((b * s, MEM_WIDTH), BF16),
        jax.ShapeDtypeStruct((b * s, ATTN_WIDTH), BF16),
        nat_s, nat_s, nat_s, g1_s, g1_s, g1_s, g2_s, g2_s, g2_s,
        jax.ShapeDtypeStruct((b, w0, 2 * LANES), F32),
        jax.ShapeDtypeStruct((b, w1, 2 * LANES), F32),
        jax.ShapeDtypeStruct((b, w2, 2 * LANES), F32),
    ] + [jax.ShapeDtypeStruct(w.shape, F32) for w in wins] \
      + [jax.ShapeDtypeStruct((bd * t_new, ATTN_WIDTH + MEM_WIDTH), F32)]
    return pl.pallas_call(
        functools.partial(_prompt_proj_kernel, t_new=t_new),
        grid=(b, n_tiles),
        in_specs=in_specs,
        out_specs=out_specs,
        out_shape=out_shape,
        scratch_shapes=[
            pltpu.VMEM((ROW_TILE, D_MODEL), BF16),
            pltpu.VMEM((ROW_TILE, N_IN), F32),
            pltpu.VMEM((6, ROW_TILE, LANES), F32),
            pltpu.VMEM((SEQS_PER_STEP, 2 * LANES, sum(w.shape[2] for w in wins)), BF16),
            pltpu.VMEM((SEQS_PER_STEP,) + mem_s.shape[1:], BF16),
        ],
        compiler_params=pltpu.CompilerParams(
            dimension_semantics=("arbitrary", "arbitrary"), vmem_limit_bytes=VMEM_LIMIT),
        name="prompt_proj",
    )(x2, *common, rope_base, rope_row, ws_pair, bias, mem_k, mem_v, *wins, mem_s, *sample_feats)


def _sample_proj(x, common, cos, sin, ws_pair, bias):
    n = x.shape[0]
    const = lambda shape: pl.BlockSpec(shape, lambda i: (0,) * len(shape))
    rows = lambda width: pl.BlockSpec((SAMPLE_TILE, width), lambda i: (i, 0))
    in_specs = [rows(D_MODEL)] + _common_proj_specs(const) + [
        const((SAMPLE_TILE, LANES)), const((SAMPLE_TILE, LANES)),
        const((GMLP_WIDTH // LANES, CHUNK, 2 * CHUNK)), const((CHUNK, GMLP_WIDTH)),
    ]
    widths = (GMLP_WIDTH, GMLP_WIDTH, ATTN_WIDTH, ATTN_WIDTH, ATTN_WIDTH, MEM_WIDTH, ATTN_WIDTH, MEM_WIDTH)
    dtypes = (BF16,) + (F32,) * 7
    return pl.pallas_call(
        _sample_proj_kernel,
        grid=(n // SAMPLE_TILE,),
        in_specs=in_specs,
        out_specs=[rows(w) for w in widths],
        out_shape=[jax.ShapeDtypeStruct((n, w), dt) for w, dt in zip(widths, dtypes)],
        scratch_shapes=[
            pltpu.VMEM((SAMPLE_TILE, D_MODEL), BF16),
            pltpu.VMEM((SAMPLE_TILE, N_IN), F32),
        ],
        compiler_params=pltpu.CompilerParams(
            dimension_semantics=("arbitrary",), vmem_limit_bytes=VMEM_LIMIT),
        name="sample_proj",
    )(x, *common, cos, sin, ws_pair, bias)


def _window_unit(q, kk, vv, bias):
    n = q.shape[0]
    lane_lo = _lane_iota(q.shape) < HEAD_DIM
    zero = jnp.zeros_like(q)
    qs = jnp.concatenate([jnp.where(lane_lo, q, zero), jnp.where(lane_lo, zero, q)], axis=0)
    d = _dot_nt(qs, kk)
    parts = [_softmax_parts(d[h * n:(h + 1) * n] + bias) for h in range(2)]
    e = jnp.concatenate([p[1].astype(BF16) for p in parts], axis=0)
    pv = _dot(e, vv)
    o = jnp.where(lane_lo, pv[:n] * (1.0 / parts[0][2]), pv[n:] * (1.0 / parts[1][2]))
    lse = [p[0] + jnp.log(p[2]) for p in parts]
    lse_b = jnp.where(lane_lo, jnp.broadcast_to(lse[0], q.shape), jnp.broadcast_to(lse[1], q.shape))
    return o, lse_b


def _window_jobs(q0_ref, k0_ref, v0_ref, k0p_ref, v0p_ref,
                 q1_ref, k1_ref, v1_ref, k1p_ref, v1p_ref,
                 q2_ref, k2_ref, v2_ref, k2p_ref, v2p_ref, o_scr, lse_scr, bias_scr):
    blk = CHUNK
    band, first = 0, 1
    d1, d2 = DIL_PAIRS[1][1], DIL_PAIRS[2][1]
    w1 = DIL_PAIRS[1][0]
    jobs = []

    def add(g, dst, which, load):
        def job():
            q, kk, vv = load()
            o, lse = _window_unit(q, kk, vv, bias_scr[which])
            o_scr[g, dst, :] = o
            lse_scr[g, dst, :] = lse
        jobs.append(job)

    for b0 in range(ATTN_TILE // blk):
        def load(b0=b0):
            q = q0_ref[0, b0 * blk:(b0 + 1) * blk, :]
            if b0 == 0:
                return (q, jnp.concatenate([k0p_ref[0], k0_ref[0, 0:blk, :]], axis=0),
                        jnp.concatenate([v0p_ref[0], v0_ref[0, 0:blk, :]], axis=0))
            return q, k0_ref[0, (b0 - 1) * blk:(b0 + 1) * blk, :], v0_ref[0, (b0 - 1) * blk:(b0 + 1) * blk, :]
        add(0, slice(b0 * blk, (b0 + 1) * blk), first if b0 == 0 else band, load)
    for nn in range(ATTN_TILE // w1):
        for r in range(d1):
            def load(nn=nn, r=r):
                blk_of = lambda ref, n: ref[0, n, r].reshape(blk, LANES)
                if nn == 0:
                    prev_k, prev_v = blk_of(k1p_ref, 0), blk_of(v1p_ref, 0)
                else:
                    prev_k, prev_v = blk_of(k1_ref, nn - 1), blk_of(v1_ref, nn - 1)
                return (blk_of(q1_ref, nn), jnp.concatenate([prev_k, blk_of(k1_ref, nn)], axis=0),
                        jnp.concatenate([prev_v, blk_of(v1_ref, nn)], axis=0))
            add(1, pl.ds(nn * w1 + r, blk, stride=d1), first if nn == 0 else band, load)
    for r in range(d2):
        def load(r=r):
            blk_of = lambda ref: ref[0, 0, r].reshape(blk, LANES)
            return (blk_of(q2_ref), jnp.concatenate([blk_of(k2p_ref), blk_of(k2_ref)], axis=0),
                    jnp.concatenate([blk_of(v2p_ref), blk_of(v2_ref)], axis=0))
        add(2, pl.ds(r, blk, stride=d2), first, load)
    return jobs


def _prompt_attn_kernel(*refs):
    (*attn_refs, sgb_ref, mix_ref, o_scr, lse_scr, bias_scr) = refs
    m = pl.program_id(1)
    blk = CHUNK
    qo = lax.broadcasted_iota(jnp.int32, (blk, 2 * blk), 0)
    ko = lax.broadcasted_iota(jnp.int32, (blk, 2 * blk), 1)
    band = (ko >= qo) & (ko <= qo + blk)
    bias_scr[0] = jnp.where(band, 0.0, NEG)
    bias_scr[1] = jnp.where(band & (ko >= blk * (m == 0).astype(jnp.int32)), 0.0, NEG)

    for job in _window_jobs(*attn_refs, o_scr, lse_scr, bias_scr):
        job()

    def combine(c, carry):
        rows = pl.ds(pl.multiple_of(c * blk, blk), blk)
        ls = [lse_scr[g, rows, :] for g in range(3)]
        mx = jnp.maximum(jnp.maximum(ls[0], ls[1]), ls[2])
        es = [jnp.exp(x - mx) for x in ls]
        inv = 1.0 / (es[0] + es[1] + es[2])
        for g in range(3):
            sl = slice(g * LANES, (g + 1) * LANES)
            mix_ref[rows, sl] = (o_scr[g, rows, :] * (es[g] * inv) * sgb_ref[rows, sl]).astype(BF16)
        return carry

    lax.fori_loop(0, ATTN_TILE // blk, combine, 0)


def _prompt_attn(feats, sgb, b, s):
    q0, k0, v0, q1, k1, v1, q2, k2, v2 = feats
    w1, d1 = DIL_PAIRS[1]
    w2, d2 = DIL_PAIRS[2]
    n_tiles = s // ATTN_TILE
    per1 = ATTN_TILE // w1
    prev = lambda m, k: jnp.maximum(m * k - 1, 0)
    nat = pl.BlockSpec((1, ATTN_TILE, LANES), lambda i, m: (i, m, 0))
    nat_p = pl.BlockSpec((1, CHUNK, LANES), lambda i, m: (i, prev(m, ATTN_TILE // CHUNK), 0))
    g1 = pl.BlockSpec((1, per1) + q1.shape[2:], lambda i, m: (i, m, 0, 0, 0, 0))
    g1_p = pl.BlockSpec((1, 1) + q1.shape[2:], lambda i, m: (i, prev(m, per1), 0, 0, 0, 0))
    g2 = pl.BlockSpec((1, 1) + q2.shape[2:], lambda i, m: (i, m, 0, 0, 0, 0))
    g2_p = pl.BlockSpec((1, 1) + q2.shape[2:], lambda i, m: (i, prev(m, 1), 0, 0, 0, 0))
    rows = pl.BlockSpec((ATTN_TILE, ATTN_WIDTH), lambda i, m: (i * n_tiles + m, 0))
    return pl.pallas_call(
        _prompt_attn_kernel,
        grid=(b, n_tiles),
        in_specs=[nat, nat, nat, nat_p, nat_p, g1, g1, g1, g1_p, g1_p, g2, g2, g2, g2_p, g2_p, rows],
        out_specs=rows,
        out_shape=jax.ShapeDtypeStruct((b * s, ATTN_WIDTH), BF16),
        scratch_shapes=[
            pltpu.VMEM((3, ATTN_TILE, LANES), F32),
            pltpu.VMEM((3, ATTN_TILE, LANES), F32),
            pltpu.VMEM((2, CHUNK, 2 * CHUNK), F32),
        ],
        compiler_params=pltpu.CompilerParams(
            dimension_semantics=("arbitrary", "arbitrary"), vmem_limit_bytes=VMEM_LIMIT),
        name="prompt_attn",
    )(q0, k0, v0, k0, v0, q1, k1, v1, k1, v1, q2, k2, v2, k2, v2, sgb)


def _window_offsets(win_refs):
    offs, total = [], 0
    for r in win_refs:
        offs.append(total)
        total += r.shape[2]
    return offs, total


def _stage_bf16(win_refs, mem_ref, kv16_ref, mem16_ref):
    offs, _ = _window_offsets(win_refs)
    for bb in range(mem_ref.shape[0]):
        for wref, off in zip(win_refs, offs):
            kv16_ref[bb, :, off:off + wref.shape[2]] = wref[bb].astype(BF16)
        mem16_ref[bb] = mem_ref[bb].astype(BF16)


class _SampleWindow:
    def __init__(self, wref, oref, kv16_ref, kv16_off, q_ref, k_ref, v_ref, bb, g, t_new):
        self.wref, self.oref, self.bb, self.t_new = wref, oref, bb, t_new
        self.window, self.dil = DIL_PAIRS[g]
        self.length = wref.shape[2]
        rows = slice(bb * t_new, (bb + 1) * t_new)
        sl = slice(g * LANES, (g + 1) * LANES)
        self.load_new = lambda: (q_ref[rows, sl], k_ref[rows, sl], v_ref[rows, sl])
        self.kv16 = lambda half: kv16_ref[bb, half * LANES:(half + 1) * LANES, kv16_off:kv16_off + self.length]

    def scores(self):
        t_new = self.t_new
        qg, kg, vg = self.load_new()
        self.lane_lo = _lane_iota(qg.shape) < HEAD_DIM
        qs = jnp.concatenate([jnp.where(self.lane_lo, qg, 0.0), jnp.where(self.lane_lo, 0.0, qg)],
                             axis=0).astype(BF16)
        zpad = jnp.zeros((CHUNK - t_new, LANES), F32)
        self.k_new = jnp.concatenate([kg, zpad], axis=0)
        self.v_new = jnp.concatenate([vg, zpad], axis=0)
        self.s_old = _dot(qs, self.kv16(0)) * SCALE
        self.s_new = _dot_nt(qs, self.k_new.astype(BF16)) * SCALE

    def softmax(self):
        t_new, dil, window = self.t_new, self.dil, self.window
        s_old, s_new = self.s_old, self.s_new
        tq = lax.broadcasted_iota(jnp.int32, s_old.shape, 0) & (t_new - 1)
        dist = self.length + tq - lax.broadcasted_iota(jnp.int32, s_old.shape, 1)
        ok_old = ((dist & (dil - 1)) == 0) & (dist <= window)
        tqn = lax.broadcasted_iota(jnp.int32, s_new.shape, 0) & (t_new - 1)
        dn = tqn - lax.broadcasted_iota(jnp.int32, s_new.shape, 1)
        ok_new = (dn >= 0) & ((dn & (dil - 1)) == 0) & (dn <= window)
        s_old = jnp.where(ok_old, s_old, NEG)
        s_new = jnp.where(ok_new, s_new, NEG)
        mx = jnp.maximum(jnp.max(s_old, axis=-1, keepdims=True), jnp.max(s_new, axis=-1, keepdims=True))
        e_old = jnp.exp(s_old - mx)
        e_new = jnp.exp(s_new - mx)
        l = jnp.sum(e_old, axis=-1, keepdims=True) + jnp.sum(e_new, axis=-1, keepdims=True)
        self.e_old, self.e_new = e_old.astype(BF16), e_new.astype(BF16)
        self.inv_l = 1.0 / l
        self.lse = mx + jnp.log(l)

    def values(self):
        t_new, lane_lo = self.t_new, self.lane_lo
        pv = (_dot_nt(self.e_old, self.kv16(1)) + _dot(self.e_new, self.v_new.astype(BF16))) * self.inv_l
        out = jnp.where(lane_lo, pv[:t_new], pv[t_new:])
        shape = (t_new, LANES)
        lse_b = jnp.where(lane_lo, jnp.broadcast_to(self.lse[:t_new], shape),
                          jnp.broadcast_to(self.lse[t_new:], shape))
        return out, lse_b

    def shift(self):
        t_new, wref, oref, bb = self.t_new, self.wref, self.oref, self.bb
        _, kg, vg = self.load_new()
        zpad = jnp.zeros((CHUNK - t_new, LANES), F32)
        new_t = jnp.concatenate([jnp.concatenate([kg, zpad], axis=0).T,
                                 jnp.concatenate([vg, zpad], axis=0).T], axis=0)
        keep = _lane_iota((2 * LANES, LANES)) < LANES - t_new
        nxt = pltpu.roll(wref[bb, :, 0:LANES], LANES - t_new, 1)
        n_blk = self.length // LANES
        for c in range(n_blk):
            cur = nxt
            if c + 1 < n_blk:
                nxt = pltpu.roll(wref[bb, :, (c + 1) * LANES:(c + 2) * LANES], LANES - t_new, 1)
            else:
                nxt = pltpu.roll(new_t, LANES - t_new, 1)
            oref[bb, :, c * LANES:(c + 1) * LANES] = jnp.where(keep, cur, nxt)


class _SampleMemory:
    def __init__(self, mem16_ref, qc_ref, bb, t_new):
        self.rows = slice(bb * t_new, (bb + 1) * t_new)
        self.qc_ref = qc_ref
        self.half = lambda h: mem16_ref[bb, h * MEM_WIDTH:(h + 1) * MEM_WIDTH, :]

    def scores(self):
        self.s = _dot(_stack_heads(self.qc_ref[self.rows, :], MEM_HEADS).astype(BF16), self.half(0)) * SCALE

    def softmax(self):
        _, e, l = _softmax_parts(self.s)
        self.e, self.inv_l = e.astype(BF16), 1.0 / l

    def values(self):
        return _unstack_heads(_dot_nt(self.e, self.half(1)) * self.inv_l, MEM_HEADS)


def _sample_finish(outs, lses, cm, sgb_ref, sgc_ref, mix_ref, bb, t_new):
    rows = slice(bb * t_new, (bb + 1) * t_new)
    mx = jnp.maximum(jnp.maximum(lses[0], lses[1]), lses[2])
    es = [jnp.exp(x - mx) for x in lses]
    inv = 1.0 / (es[0] + es[1] + es[2])
    for g in range(len(DIL_PAIRS)):
        sl = slice(g * LANES, (g + 1) * LANES)
        mix_ref[rows, sl] = outs[g] * (es[g] * inv) * sgb_ref[rows, sl]
    mix_ref[rows, ATTN_WIDTH:ATTN_WIDTH + MEM_WIDTH] = cm * sgc_ref[rows, :]


def _out_proj_kernel(x_ref, *refs):
    *mix_refs, w_ref, y_ref = refs
    mix = jnp.concatenate([r[...].astype(BF16) for r in mix_refs], axis=1)
    y_ref[...] = x_ref[...] + _dot(mix, w_ref[...].astype(BF16))


def _out_proj(x, mixes, w_out):
    n = x.shape[0]
    tile = min(OUT_TILE, n)
    assert n % tile == 0
    rows = lambda width: pl.BlockSpec((tile, width), lambda i: (i, 0))
    return pl.pallas_call(
        _out_proj_kernel,
        grid=(n // tile,),
        in_specs=[rows(D_MODEL)] + [rows(mx.shape[1]) for mx in mixes]
        + [pl.BlockSpec((MIX_WIDTH, D_MODEL), lambda i: (0, 0))],
        out_specs=rows(D_MODEL),
        out_shape=jax.ShapeDtypeStruct((n, D_MODEL), F32),
        compiler_params=pltpu.CompilerParams(
            dimension_semantics=("arbitrary",), vmem_limit_bytes=VMEM_LIMIT),
        name="out_proj",
    )(x, *mixes, w_out)


def kernel(x_prompt, x_sample, state_win0_kv, state_win1_kv, state_win2_kv, cache_mem_kv, mem_prompt, norm_gain, w_in, gmlp_ln_gain, gmlp_ln_bias, gmlp_w_s, gmlp_b_s, attn_q_norm, attn_k_norm, mem_norm, w_mem_kv, mem_q_norm, mem_k_norm, w_out):
    depth = norm_gain.shape[0]
    assert depth == 1, "single-layer step only"
    bp, s, _ = x_prompt.shape
    bd, t, _ = x_sample.shape
    past_len = PAST_LEN
    assert s % ATTN_TILE == 0 and (bd * t) % SAMPLE_TILE == 0
    assert CHUNK % t == 0 and t & (t - 1) == 0 and DIL_PAIRS[1][0] % ROW_TILE == 0
    states = (state_win0_kv, state_win1_kv, state_win2_kv)
    for st, (w, d) in zip(states, DIL_PAIRS):
        assert st.shape[2] == w and w // d == CHUNK and d & (d - 1) == 0

    common = _common_proj_operands(norm_gain[0], w_in[0], gmlp_ln_gain[0], gmlp_ln_bias[0],
                                   attn_q_norm[0], attn_k_norm[0], mem_q_norm[0])
    tril = jnp.tril(jnp.ones((CHUNK, CHUNK), dtype=bool))
    ws = jnp.where(tril[None], gmlp_w_s[0], 0.0)
    b_s = gmlp_b_s[0]

    reps = CHUNK // t
    ws_s = jnp.stack([jnp.kron(jnp.eye(reps, dtype=F32), ws[g, :t, :t]) for g in range(N_GMLP_GROUPS)])
    bias_s = jnp.repeat(jnp.tile(b_s[:, :t], (1, reps)).T, HEAD_DIM, axis=1)
    cos_s, sin_s = _rope_tables(past_len + (jnp.arange(SAMPLE_TILE, dtype=jnp.int32) % t))
    mixa_s, vln_s, q_s, k_s, v_s, qc_s, sgb_s, sgc_s = _sample_proj(
        x_sample.reshape(bd * t, D_MODEL), common, cos_s, sin_s, _pair_ws(ws_s), bias_s)
    feat_major = lambda a: jnp.transpose(a, (0, 2, 3, 4, 1)).reshape(a.shape[0], -1, a.shape[1])
    wins = [feat_major(st[0]) for st in states]
    mem_s = feat_major(cache_mem_kv[0])

    mem_kv_p, mem_k, mem_v = _mem_kv(mem_prompt, mem_norm[0], w_mem_kv[0], mem_k_norm[0])
    rope_base = jnp.stack(_rope_tables(jnp.arange(0, s, ROW_TILE, dtype=jnp.int32)), axis=1)
    rope_row = jnp.stack(_rope_tables(jnp.arange(ROW_TILE, dtype=jnp.int32)), axis=0)
    bias_p = jnp.repeat(b_s.T, HEAD_DIM, axis=1)
    outs = _prompt_proj(x_prompt, common, rope_base, rope_row, _pair_ws(ws), bias_p, mem_k, mem_v,
                        wins, mem_s, (q_s, k_s, v_s, qc_s, sgb_s, sgc_s), t)
    mixa_p, mixc_p, sgb_p = outs[0:3]
    tails = outs[12:15]
    new_wins_t = outs[15:19]
    mixb_p = _prompt_attn(outs[3:12], sgb_p, bp, s)
    y_prompt = _out_proj(x_prompt.reshape(bp * s, D_MODEL), (mixa_p, mixb_p, mixc_p), w_out[0])

    mixbc_s = new_wins_t[3]
    heads = LANES // HEAD_DIM
    nw0, nw1, nw2 = [
        jnp.transpose(w.reshape(bd, 2, heads, HEAD_DIM, w.shape[2]), (0, 4, 1, 2, 3)) for w in new_wins_t[:3]]
    y_sample = _out_proj(x_sample.reshape(bd * t, D_MODEL), (mixa_s, mixbc_s), w_out[0])

    win_shape = lambda n, w: (1, n, w, 2, heads, HEAD_DIM)
    return (
        y_prompt.reshape(bp, s, D_MODEL),
        y_sample.reshape(bd, t, D_MODEL),
        tails[0].reshape(win_shape(bp, DIL_PAIRS[0][0])),
        tails[1].reshape(win_shape(bp, DIL_PAIRS[1][0])),
        tails[2].reshape(win_shape(bp, DIL_PAIRS[2][0])),
        mem_kv_p.reshape(1, bp, N_MEM, 2, MEM_HEADS, HEAD_DIM),
        nw0.reshape(win_shape(bd, DIL_PAIRS[0][0])),
        nw1.reshape(win_shape(bd, DIL_PAIRS[1][0])),
        nw2.reshape(win_shape(bd, DIL_PAIRS[2][0])),
        vln_s.reshape(1, bd, t, N_GMLP_GROUPS, HEAD_DIM),
    )
```

```python
import functools

import numpy as np
import jax
import jax.numpy as jnp
from jax import lax
from jax.experimental import pallas as pl
from jax.experimental.pallas import tpu as pltpu

F32 = jnp.float32
BF16 = jnp.bfloat16

D_MODEL = 1024
HEAD_DIM = 64
LANES = 128
N_GMLP_GROUPS = 6
GMLP_WIDTH = N_GMLP_GROUPS * HEAD_DIM
CHUNK = 128
DIL_PAIRS = ((128, 1), (512, 4), (2048, 16))
ATTN_WIDTH = len(DIL_PAIRS) * LANES
N_MEM = 256
PAST_LEN = 8192
MEM_HEADS = 4
MEM_WIDTH = MEM_HEADS * HEAD_DIM
MIX_WIDTH = GMLP_WIDTH + ATTN_WIDTH + MEM_WIDTH
N_IN = 3 * GMLP_WIDTH + 4 * ATTN_WIDTH + 2 * MEM_WIDTH
ROPE_THETA = 500000.0
ROT_DIM = HEAD_DIM // 4
ROT_HALF = ROT_DIM // 2
EPS = 1e-6
NEG = -1e30
SCALE = HEAD_DIM ** -0.5
assert np.frexp(SCALE)[0] == 0.5, "the prompt path folds SCALE into q, exact only for a power of two"

OFF_U = 0
OFF_V = OFF_U + GMLP_WIDTH
OFF_GA = OFF_V + GMLP_WIDTH
OFF_Q = OFF_GA + GMLP_WIDTH
OFF_K = OFF_Q + ATTN_WIDTH
OFF_VV = OFF_K + ATTN_WIDTH
OFF_GB = OFF_VV + ATTN_WIDTH
OFF_QC = OFF_GB + ATTN_WIDTH
OFF_GC = OFF_QC + MEM_WIDTH

ROW_TILE = 256
SAMPLE_TILE = 512
ATTN_TILE = 2048
SEQS_PER_STEP = 2
OUT_TILE = 2048
VMEM_LIMIT = 56 * 1024 * 1024


def _dot(a, b):
    return jnp.dot(a, b, preferred_element_type=F32)


def _dot_nt(a, b):
    return lax.dot_general(a, b, (((1,), (1,)), ((), ())), preferred_element_type=F32)


def _silu(x):
    return x * (1.0 / (1.0 + jnp.exp(-x)))


def _gelu(x):
    return 0.5 * x * (1.0 + lax.erf(x * np.float32(np.sqrt(0.5))))


def _lane_iota(shape):
    return lax.broadcasted_iota(jnp.int32, shape, len(shape) - 1)


def _head_sum(ss):
    r = (lax.broadcasted_iota(jnp.int32, (2 * LANES, LANES), 0) >> 6) & 1
    c = lax.broadcasted_iota(jnp.int32, (2 * LANES, LANES), 1) >> 6
    ones_blk = jnp.where(r == c, 1.0, 0.0).astype(BF16)
    hi = ss.astype(BF16)
    lo = (ss - hi.astype(F32)).astype(BF16)
    return _dot(jnp.concatenate([hi, lo], axis=1), ones_blk)


def _head_rms(x, gain):
    ms = _head_sum(x * x) * (1.0 / HEAD_DIM)
    return x * lax.rsqrt(ms + EPS) * gain


def _rope(x, cos, sin):
    lane = _lane_iota(x.shape) & (HEAD_DIM - 1)
    partner = jnp.where(lane < ROT_HALF, pltpu.roll(x, LANES - ROT_HALF, 1), pltpu.roll(x, ROT_HALF, 1))
    return x * cos + partner * sin


def _softmax_parts(s, axis=-1):
    mx = jnp.max(s, axis=axis, keepdims=True)
    e = jnp.exp(s - mx)
    return mx, e, jnp.sum(e, axis=axis, keepdims=True)


def _mem_kv_kernel(mem_ref, norm_ref, w_ref, kn_ref, kv_ref, k_ref, v_ref):
    x = mem_ref[0]
    h = x * lax.rsqrt(jnp.mean(x * x, axis=-1, keepdims=True) + EPS) * norm_ref[...]
    kv = _dot(h.astype(BF16), w_ref[...].astype(BF16))
    for c in range(MEM_WIDTH // LANES):
        sl = slice(c * LANES, (c + 1) * LANES)
        kc = _head_rms(kv[:, sl], kn_ref[...])
        kv_ref[0, :, sl] = kc
        k_ref[0, :, sl] = kc.astype(BF16)
    v = kv[:, MEM_WIDTH:]
    kv_ref[0, :, MEM_WIDTH:] = v
    v_ref[0] = v.astype(BF16)


def _mem_kv(mem, mem_norm, w_mem_kv, mem_k_norm):
    b = mem.shape[0]
    return pl.pallas_call(
        _mem_kv_kernel,
        grid=(b,),
        in_specs=[
            pl.BlockSpec((1, N_MEM, D_MODEL), lambda i: (i, 0, 0)),
            pl.BlockSpec((1, D_MODEL), lambda i: (0, 0)),
            pl.BlockSpec((D_MODEL, 2 * MEM_WIDTH), lambda i: (0, 0)),
            pl.BlockSpec((1, LANES), lambda i: (0, 0)),
        ],
        out_specs=[
            pl.BlockSpec((1, N_MEM, 2 * MEM_WIDTH), lambda i: (i, 0, 0)),
            pl.BlockSpec((1, N_MEM, MEM_WIDTH), lambda i: (i, 0, 0)),
            pl.BlockSpec((1, N_MEM, MEM_WIDTH), lambda i: (i, 0, 0)),
        ],
        out_shape=[
            jax.ShapeDtypeStruct((b, N_MEM, 2 * MEM_WIDTH), F32),
            jax.ShapeDtypeStruct((b, N_MEM, MEM_WIDTH), BF16),
            jax.ShapeDtypeStruct((b, N_MEM, MEM_WIDTH), BF16),
        ],
        name="mem_kv",
    )(mem, mem_norm.reshape(1, D_MODEL), w_mem_kv,
      jnp.tile(mem_k_norm, LANES // HEAD_DIM).reshape(1, LANES))


def _gmlp_chunk(z_scr, rows, lng_ref, lnb_ref, ws_ref, bias_ref):
    gu = _gelu(z_scr[rows, OFF_U:OFF_U + GMLP_WIDTH])
    gv = _gelu(z_scr[rows, OFF_V:OFF_V + GMLP_WIDTH])
    mu = jnp.mean(gv, axis=-1, keepdims=True)
    dv = gv - mu
    var = jnp.mean(dv * dv, axis=-1, keepdims=True)
    vln = dv * lax.rsqrt(var + EPS) * lng_ref[...] + lnb_ref[...]
    lane = _lane_iota((CHUNK, LANES))
    pieces = []
    for p in range(GMLP_WIDTH // LANES):
        sl = slice(p * LANES, (p + 1) * LANES)
        vp = vln[:, sl]
        rhs = jnp.concatenate([jnp.where(lane < HEAD_DIM, vp, 0.0),
                               jnp.where(lane < HEAD_DIM, 0.0, vp)], axis=0).astype(BF16)
        s = _dot(ws_ref[p], rhs) + bias_ref[:, sl]
        ga = z_scr[rows, OFF_GA + p * LANES:OFF_GA + (p + 1) * LANES]
        pieces.append(gu[:, sl] * s * _silu(ga))
    return pieces, vln


def _qk_chunk(z_scr, rows, off, g, norm_ref, cos, sin):
    x = z_scr[rows, off + g * LANES:off + (g + 1) * LANES]
    return _rope(_head_rms(x, norm_ref[...]), cos, sin)


def _stack_heads(x, n_heads):
    head = _lane_iota(x.shape) >> 6
    return jnp.concatenate([jnp.where(head == h, x, 0.0) for h in range(n_heads)], axis=0)


def _unstack_heads(y, n_heads):
    r = y.shape[0] // n_heads
    head = _lane_iota((r, y.shape[1])) >> 6
    out = y[0:r]
    for h in range(1, n_heads):
        out = jnp.where(head == h, y[h * r:(h + 1) * r], out)
    return out


def _prompt_proj_kernel(x_ref, gain_ref, w_ref, lng_ref, lnb_ref, qn_ref, kn_ref, mqn_ref,
                        rope_b_ref, rope_r_ref, ws_ref, bias_ref, mk_ref, mv_ref,
                        w0_ref, w1_ref, w2_ref, mem_ref, sq_ref, sk_ref, sv_ref, sqc_ref, ssgb_ref, ssgc_ref,
                        mixa_ref, mixc_ref, sgb_ref,
                        q0_ref, k0_ref, v0_ref, q1_ref, k1_ref, v1_ref, q2_ref, k2_ref, v2_ref,
                        t0_ref, t1_ref, t2_ref,
                        o0_ref, o1_ref, o2_ref, smix_ref,
                        h_scr, z_scr, perm_scr, kv16_scr, mem16_scr, *, t_new):
    win_refs, out_refs = (w0_ref, w1_ref, w2_ref), (o0_ref, o1_ref, o2_ref)
    kv16_offs, _ = _window_offsets(win_refs)
    seqs = range(SEQS_PER_STEP)
    groups = range(len(DIL_PAIRS))
    windows = {(bb, g): _SampleWindow(win_refs[g], out_refs[g], kv16_scr, kv16_offs[g],
                                      sq_ref, sk_ref, sv_ref, bb, g, t_new) for bb in seqs for g in groups}
    memories = [_SampleMemory(mem16_scr, sqc_ref, bb, t_new) for bb in seqs]
    sample_pieces = list(windows.values()) + memories

    def sample_finish():
        for bb in seqs:
            outs, lses = zip(*(windows[bb, g].values() for g in groups))
            _sample_finish(outs, lses, memories[bb].values(), ssgb_ref, ssgc_ref, smix_ref, bb, t_new)

    _stage_bf16(win_refs, mem_ref, kv16_scr, mem16_scr)

    x = x_ref[...]
    h = x * lax.rsqrt(jnp.mean(x * x, axis=-1, keepdims=True) + EPS) * gain_ref[...]
    h_scr[...] = h.astype(BF16)

    def project(lo, width):
        z_scr[:, lo:lo + width] = _dot(h_scr[...], w_ref[:, lo:lo + width])

    chunks = [slice(c * CHUNK, (c + 1) * CHUNK) for c in range(ROW_TILE // CHUNK)]
    nat_refs = (q0_ref, k0_ref, v0_ref)

    pm, gm, qk = {}, {}, {}

    def pm_queries():
        qs = []
        for rows in chunks:
            qc = jnp.concatenate(
                [_head_rms(z_scr[rows, OFF_QC + j * LANES:OFF_QC + (j + 1) * LANES], mqn_ref[...])
                 for j in range(MEM_WIDTH // LANES)], axis=1)
            qs.append(_stack_heads(qc, MEM_HEADS).astype(BF16))
        pm["qs"] = jnp.concatenate(qs, axis=0)

    def pm_scores():
        pm["s"] = _dot_nt(pm["qs"], mk_ref[0]) * SCALE

    def pm_softmax():
        _, e, l = _softmax_parts(pm["s"])
        pm["e"], pm["inv_l"] = e.astype(BF16), 1.0 / l

    def pm_values():
        pv = _dot(pm["e"], mv_ref[0]) * pm["inv_l"]
        per = MEM_HEADS * CHUNK
        for c, rows in enumerate(chunks):
            cm = _unstack_heads(pv[c * per:(c + 1) * per], MEM_HEADS)
            mixc_ref[rows, :] = (cm * _silu(z_scr[rows, OFF_GC:OFF_GC + MEM_WIDTH])).astype(BF16)

    def gm_prepare():
        lane = _lane_iota((CHUNK, LANES))
        for c, rows in enumerate(chunks):
            z_scr[rows, OFF_U:OFF_U + GMLP_WIDTH] = _gelu(z_scr[rows, OFF_U:OFF_U + GMLP_WIDTH])
            gv = _gelu(z_scr[rows, OFF_V:OFF_V + GMLP_WIDTH])
            mu = jnp.mean(gv, axis=-1, keepdims=True)
            dv = gv - mu
            var = jnp.mean(dv * dv, axis=-1, keepdims=True)
            vln = dv * lax.rsqrt(var + EPS) * lng_ref[...] + lnb_ref[...]
            for p in range(GMLP_WIDTH // LANES):
                vp = vln[:, p * LANES:(p + 1) * LANES]
                gm[c, p] = jnp.concatenate([jnp.where(lane < HEAD_DIM, vp, 0.0),
                                            jnp.where(lane < HEAD_DIM, 0.0, vp)], axis=0).astype(BF16)

    def gm_spatial():
        for key in list(gm):
            gm[key] = _dot(ws_ref[key[1]], gm[key])

    def gm_gate():
        for c, rows in enumerate(chunks):
            for p in range(GMLP_WIDTH // LANES):
                sl = slice(p * LANES, (p + 1) * LANES)
                s = gm[c, p] + bias_ref[:, sl]
                ga = z_scr[rows, OFF_GA + p * LANES:OFF_GA + (p + 1) * LANES]
                gu = z_scr[rows, OFF_U + p * LANES:OFF_U + (p + 1) * LANES]
                mixa_ref[rows, sl] = (gu * s * _silu(ga)).astype(BF16)

    def qk_sums():
        for c, rows in enumerate(chunks):
            for g in groups:
                for off in (OFF_Q, OFF_K):
                    x = z_scr[rows, off + g * LANES:off + (g + 1) * LANES]
                    qk[c, g, off] = _head_sum(x * x)

    def qk_normed(c, rows, g, off, norm_ref, cos, sin):
        x = z_scr[rows, off + g * LANES:off + (g + 1) * LANES]
        return _rope(x * lax.rsqrt(qk[c, g, off] * (1.0 / HEAD_DIM) + EPS) * norm_ref[...], cos, sin)

    project(OFF_QC, MEM_WIDTH)
    project(OFF_GC, MEM_WIDTH)
    for piece in sample_pieces:
        piece.scores()
    project(OFF_U, GMLP_WIDTH)
    for piece in sample_pieces:
        piece.softmax()
    pm_queries()
    project(OFF_V, GMLP_WIDTH)
    pm_scores()
    project(OFF_GA, GMLP_WIDTH)
    pm_softmax()
    gm_prepare()
    for g in groups:
        windows[0, g].shift()
    project(OFF_Q, ATTN_WIDTH)
    sample_finish()
    pm_values()
    gm_spatial()
    project(OFF_K, ATTN_WIDTH)
    gm_gate()
    project(OFF_VV, ATTN_WIDTH)
    qk_sums()
    project(OFF_GB, ATTN_WIDTH)
    for bb in seqs[1:]:
        for g in groups:
            windows[bb, g].shift()
    cos_b, sin_b = rope_b_ref[0, 0:1, :], rope_b_ref[0, 1:2, :]
    for c, rows in enumerate(chunks):
        cos_r, sin_r = rope_r_ref[0, rows, :], rope_r_ref[1, rows, :]
        cos = cos_b * cos_r - sin_b * sin_r
        sin = sin_b * cos_r + cos_b * sin_r
        for g in range(len(DIL_PAIRS)):
            q = qk_normed(c, rows, g, OFF_Q, qn_ref, cos, sin) * SCALE
            k = qk_normed(c, rows, g, OFF_K, kn_ref, cos, sin)
            v = z_scr[rows, OFF_VV + g * LANES:OFF_VV + (g + 1) * LANES]
            if g == 0:
                for ref, val in zip(nat_refs, (q, k, v)):
                    ref[0, rows, :] = val.astype(BF16)
                if rows is chunks[-1]:
                    t0_ref[0, :, 0:LANES] = k
                    t0_ref[0, :, LANES:2 * LANES] = v
            else:
                for j, val in enumerate((q, k, v)):
                    perm_scr[3 * (g - 1) + j, rows, :] = val
                tail = (t1_ref, t2_ref)[g - 1]
                tail[0, rows, 0:LANES] = k
                tail[0, rows, LANES:2 * LANES] = v

    for rows in chunks:
        sgb_ref[rows, :] = _silu(z_scr[rows, OFF_GB:OFF_GB + ATTN_WIDTH]).astype(sgb_ref.dtype)

    d1 = DIL_PAIRS[1][1]
    d2 = DIL_PAIRS[2][1]
    for j, ref in enumerate((q1_ref, k1_ref, v1_ref)):
        for r in range(d1):
            ref[0, 0, r, 0] = perm_scr[j, pl.ds(r, ROW_TILE // d1, stride=d1), :].astype(BF16)
    for j, ref in enumerate((q2_ref, k2_ref, v2_ref)):
        for r in range(d2):
            ref[0, 0, r, 0] = perm_scr[3 + j, pl.ds(r, ROW_TILE // d2, stride=d2), :].astype(BF16)


def _sample_proj_kernel(x_ref, gain_ref, w_ref, lng_ref, lnb_ref, qn_ref, kn_ref, mqn_ref,
                        cos_ref, sin_ref, ws_ref, bias_ref,
                        mixa_ref, vln_ref, q_ref, k_ref, v_ref, qc_ref, sgb_ref, sgc_ref,
                        h_scr, z_scr):
    x = x_ref[...]
    h = x * lax.rsqrt(jnp.mean(x * x, axis=-1, keepdims=True) + EPS) * gain_ref[...]
    h_scr[...] = h.astype(BF16)

    def project(lo, width):
        z_scr[:, lo:lo + width] = _dot(h_scr[...], w_ref[:, lo:lo + width])

    chunks = [slice(c * CHUNK, (c + 1) * CHUNK) for c in range(x_ref.shape[0] // CHUNK)]
    for lo in (OFF_U, OFF_V, OFF_GA):
        project(lo, GMLP_WIDTH)
    project(OFF_Q, ATTN_WIDTH)
    for rows in chunks:
        pieces, vln = _gmlp_chunk(z_scr, rows, lng_ref, lnb_ref, ws_ref, bias_ref)
        for p, a in enumerate(pieces):
            mixa_ref[rows, p * LANES:(p + 1) * LANES] = a.astype(BF16)
        vln_ref[rows, :] = vln
    project(OFF_K, ATTN_WIDTH)
    project(OFF_VV, ATTN_WIDTH)
    for rows in chunks:
        cos, sin = cos_ref[rows, :], sin_ref[rows, :]
        for g in range(len(DIL_PAIRS)):
            sl = slice(g * LANES, (g + 1) * LANES)
            q_ref[rows, sl] = _qk_chunk(z_scr, rows, OFF_Q, g, qn_ref, cos, sin)
    project(OFF_GB, ATTN_WIDTH)
    for rows in chunks:
        cos, sin = cos_ref[rows, :], sin_ref[rows, :]
        for g in range(len(DIL_PAIRS)):
            sl = slice(g * LANES, (g + 1) * LANES)
            k_ref[rows, sl] = _qk_chunk(z_scr, rows, OFF_K, g, kn_ref, cos, sin)
            v_ref[rows, sl] = z_scr[rows, OFF_VV + g * LANES:OFF_VV + (g + 1) * LANES]
    project(OFF_QC, MEM_WIDTH)
    project(OFF_GC, MEM_WIDTH)
    for rows in chunks:
        sgb_ref[rows, :] = _silu(z_scr[rows, OFF_GB:OFF_GB + ATTN_WIDTH])
        for j in range(MEM_WIDTH // LANES):
            sl = slice(j * LANES, (j + 1) * LANES)
            qc_ref[rows, sl] = _head_rms(z_scr[rows, OFF_QC + j * LANES:OFF_QC + (j + 1) * LANES],
                                         mqn_ref[...])
        sgc_ref[rows, :] = _silu(z_scr[rows, OFF_GC:OFF_GC + MEM_WIDTH])


def _rope_tables(pos):
    inv = ROPE_THETA ** (-jnp.arange(ROT_HALF, dtype=F32) * 2.0 / ROT_DIM)
    ang = pos.astype(F32)[:, None] * inv[None, :]
    cos, sin = jnp.cos(ang), jnp.sin(ang)
    n = pos.shape[0]
    pad1 = jnp.ones((n, HEAD_DIM - ROT_DIM), F32)
    pad0 = jnp.zeros((n, HEAD_DIM - ROT_DIM), F32)
    cos_h = jnp.concatenate([cos, cos, pad1], axis=1)
    sin_h = jnp.concatenate([-sin, sin, pad0], axis=1)
    return jnp.tile(cos_h, (1, LANES // HEAD_DIM)), jnp.tile(sin_h, (1, LANES // HEAD_DIM))


def _common_proj_operands(norm_gain, w_in, ln_g, ln_b, q_norm, k_norm, mem_q_norm):
    tile2 = lambda g: jnp.tile(g, LANES // HEAD_DIM).reshape(1, LANES)
    return (norm_gain.reshape(1, D_MODEL), w_in.astype(BF16), ln_g.reshape(1, GMLP_WIDTH),
            ln_b.reshape(1, GMLP_WIDTH), tile2(q_norm), tile2(k_norm), tile2(mem_q_norm))


def _common_proj_specs(const):
    return [
        const((1, D_MODEL)), const((D_MODEL, N_IN)), const((1, GMLP_WIDTH)), const((1, GMLP_WIDTH)),
        const((1, LANES)), const((1, LANES)), const((1, LANES)),
    ]


def _pair_ws(ws):
    return jnp.concatenate([ws[0::2], ws[1::2]], axis=2).astype(BF16)


def _prompt_proj(x, common, rope_base, rope_row, ws_pair, bias, mem_k, mem_v, wins, mem_s, sample_feats, t_new):
    b, s, _ = x.shape
    n_tiles = s // ROW_TILE
    bd = mem_s.shape[0]
    assert b * n_tiles * SEQS_PER_STEP == bd
    w0, w1, w2 = (w for w, _ in DIL_PAIRS)
    d1, d2 = DIL_PAIRS[1][1], DIL_PAIRS[2][1]
    per1, per2 = w1 // ROW_TILE, w2 // ROW_TILE
    x2 = x.reshape(b * s, D_MODEL)
    const = lambda shape: pl.BlockSpec(shape, lambda i, j: (0,) * len(shape))
    rows = lambda width: pl.BlockSpec((ROW_TILE, width), lambda i, j: (i * n_tiles + j, 0))
    seq_blk = lambda a: pl.BlockSpec((SEQS_PER_STEP,) + a.shape[1:], lambda i, j: (i * n_tiles + j, 0, 0))
    seq_rows = lambda width: pl.BlockSpec((SEQS_PER_STEP * t_new, width), lambda i, j: (i * n_tiles + j, 0))
    in_specs = [rows(D_MODEL)] + _common_proj_specs(const) + [
        pl.BlockSpec((1, 2, LANES), lambda i, j: (j, 0, 0)),
        const((2, ROW_TILE, LANES)),
        const((GMLP_WIDTH // LANES, CHUNK, 2 * CHUNK)), const((CHUNK, GMLP_WIDTH)),
        pl.BlockSpec((1, N_MEM, MEM_WIDTH), lambda i, j: (i, 0, 0)),
        pl.BlockSpec((1, N_MEM, MEM_WIDTH), lambda i, j: (i, 0, 0)),
    ] + [seq_blk(w) for w in wins] + [seq_blk(mem_s)] + [seq_rows(f.shape[1]) for f in sample_feats]
    nat = pl.BlockSpec((1, ROW_TILE, LANES), lambda i, j: (i, j, 0))
    g1 = pl.BlockSpec((1, 1, d1, 1, ROW_TILE // d1, LANES), lambda i, j: (i, j // per1, 0, j % per1, 0, 0))
    g2 = pl.BlockSpec((1, 1, d2, 1, ROW_TILE // d2, LANES), lambda i, j: (i, j // per2, 0, j % per2, 0, 0))
    last = lambda n_blk: (lambda i, j: (i, jnp.maximum(j - (n_tiles - n_blk), 0), 0))
    t0 = pl.BlockSpec((1, w0, 2 * LANES), last(1))
    t1 = pl.BlockSpec((1, ROW_TILE, 2 * LANES), last(per1))
    t2 = pl.BlockSpec((1, ROW_TILE, 2 * LANES), last(per2))
    out_specs = [rows(GMLP_WIDTH), rows(MEM_WIDTH), rows(ATTN_WIDTH),
                 nat, nat, nat, g1, g1, g1, g2, g2, g2, t0, t1, t2] \
        + [seq_blk(w) for w in wins] + [seq_rows(ATTN_WIDTH + MEM_WIDTH)]
    nat_s = jax.ShapeDtypeStruct((b, s, LANES), BF16)
    g1_s = jax.ShapeDtypeStruct((b, s // w1, d1, per1, ROW_TILE // d1, LANES), BF16)
    g2_s = jax.ShapeDtypeStruct((b, s // w2, d2, per2, ROW_TILE // d2, LANES), BF16)
    out_shape = [
        jax.ShapeDtypeStruct((b * s, GMLP_WIDTH), BF16),
        jax.ShapeDtypeStruct((b * s, MEM_WIDTH), BF16),
        jax.ShapeDtypeStruct((b * s, ATTN_WIDTH), BF16),
        nat_s, nat_s, nat_s, g1_s, g1_s, g1_s, g2_s, g2_s, g2_s,
        jax.ShapeDtypeStruct((b, w0, 2 * LANES), F32),
        jax.ShapeDtypeStruct((b, w1, 2 * LANES), F32),
        jax.ShapeDtypeStruct((b, w2, 2 * LANES), F32),
    ] + [jax.ShapeDtypeStruct(w.shape, F32) for w in wins] \
      + [jax.ShapeDtypeStruct((bd * t_new, ATTN_WIDTH + MEM_WIDTH), F32)]
    return pl.pallas_call(
        functools.partial(_prompt_proj_kernel, t_new=t_new),
        grid=(b, n_tiles),
        in_specs=in_specs,
        out_specs=out_specs,
        out_shape=out_shape,
        scratch_shapes=[
            pltpu.VMEM((ROW_TILE, D_MODEL), BF16),
            pltpu.VMEM((ROW_TILE, N_IN), F32),
            pltpu.VMEM((6, ROW_TILE, LANES), F32),
            pltpu.VMEM((SEQS_PER_STEP, 2 * LANES, sum(w.shape[2] for w in wins)), BF16),
            pltpu.VMEM((SEQS_PER_STEP,) + mem_s.shape[1:], BF16),
        ],
        compiler_params=pltpu.CompilerParams(
            dimension_semantics=("arbitrary", "arbitrary"), vmem_limit_bytes=VMEM_LIMIT),
        name="prompt_proj",
    )(x2, *common, rope_base, rope_row, ws_pair, bias, mem_k, mem_v, *wins, mem_s, *sample_feats)


def _sample_proj(x, common, cos, sin, ws_pair, bias):
    n = x.shape[0]
    const = lambda shape: pl.BlockSpec(shape, lambda i: (0,) * len(shape))
    rows = lambda width: pl.BlockSpec((SAMPLE_TILE, width), lambda i: (i, 0))
    in_specs = [rows(D_MODEL)] + _common_proj_specs(const) + [
        const((SAMPLE_TILE, LANES)), const((SAMPLE_TILE, LANES)),
        const((GMLP_WIDTH // LANES, CHUNK, 2 * CHUNK)), const((CHUNK, GMLP_WIDTH)),
    ]
    widths = (GMLP_WIDTH, GMLP_WIDTH, ATTN_WIDTH, ATTN_WIDTH, ATTN_WIDTH, MEM_WIDTH, ATTN_WIDTH, MEM_WIDTH)
    dtypes = (BF16,) + (F32,) * 7
    return pl.pallas_call(
        _sample_proj_kernel,
        grid=(n // SAMPLE_TILE,),
        in_specs=in_specs,
        out_specs=[rows(w) for w in widths],
        out_shape=[jax.ShapeDtypeStruct((n, w), dt) for w, dt in zip(widths, dtypes)],
        scratch_shapes=[
            pltpu.VMEM((SAMPLE_TILE, D_MODEL), BF16),
            pltpu.VMEM((SAMPLE_TILE, N_IN), F32),
        ],
        compiler_params=pltpu.CompilerParams(
            dimension_semantics=("arbitrary",), vmem_limit_bytes=VMEM_LIMIT),
        name="sample_proj",
    )(x, *common, cos, sin, ws_pair, bias)


def _window_unit(q, kk, vv, bias):
    n = q.shape[0]
    lane_lo = _lane_iota(q.shape) < HEAD_DIM
    zero = jnp.zeros_like(q)
    qs = jnp.concatenate([jnp.where(lane_lo, q, zero), jnp.where(lane_lo, zero, q)], axis=0)
    d = _dot_nt(qs, kk)
    parts = [_softmax_parts(d[h * n:(h + 1) * n] + bias) for h in range(2)]
    e = jnp.concatenate([p[1].astype(BF16) for p in parts], axis=0)
    pv = _dot(e, vv)
    o = jnp.where(lane_lo, pv[:n] * (1.0 / parts[0][2]), pv[n:] * (1.0 / parts[1][2]))
    lse = [p[0] + jnp.log(p[2]) for p in parts]
    lse_b = jnp.where(lane_lo, jnp.broadcast_to(lse[0], q.shape), jnp.broadcast_to(lse[1], q.shape))
    return o, lse_b


def _window_jobs(q0_ref, k0_ref, v0_ref, k0p_ref, v0p_ref,
                 q1_ref, k1_ref, v1_ref, k1p_ref, v1p_ref,
                 q2_ref, k2_ref, v2_ref, k2p_ref, v2p_ref, o_scr, lse_scr, bias_scr):
    blk = CHUNK
    band, first = 0, 1
    d1, d2 = DIL_PAIRS[1][1], DIL_PAIRS[2][1]
    w1 = DIL_PAIRS[1][0]
    jobs = []

    def add(g, dst, which, load):
        def job():
            q, kk, vv = load()
            o, lse = _window_unit(q, kk, vv, bias_scr[which])
            o_scr[g, dst, :] = o
            lse_scr[g, dst, :] = lse
        jobs.append(job)

    for b0 in range(ATTN_TILE // blk):
        def load(b0=b0):
            q = q0_ref[0, b0 * blk:(b0 + 1) * blk, :]
            if b0 == 0:
                return (q, jnp.concatenate([k0p_ref[0], k0_ref[0, 0:blk, :]], axis=0),
                        jnp.concatenate([v0p_ref[0], v0_ref[0, 0:blk, :]], axis=0))
            return q, k0_ref[0, (b0 - 1) * blk:(b0 + 1) * blk, :], v0_ref[0, (b0 - 1) * blk:(b0 + 1) * blk, :]
        add(0, slice(b0 * blk, (b0 + 1) * blk), first if b0 == 0 else band, load)
    for nn in range(ATTN_TILE // w1):
        for r in range(d1):
            def load(nn=nn, r=r):
                blk_of = lambda ref, n: ref[0, n, r].reshape(blk, LANES)
                if nn == 0:
                    prev_k, prev_v = blk_of(k1p_ref, 0), blk_of(v1p_ref, 0)
                else:
                    prev_k, prev_v = blk_of(k1_ref, nn - 1), blk_of(v1_ref, nn - 1)
                return (blk_of(q1_ref, nn), jnp.concatenate([prev_k, blk_of(k1_ref, nn)], axis=0),
                        jnp.concatenate([prev_v, blk_of(v1_ref, nn)], axis=0))
            add(1, pl.ds(nn * w1 + r, blk, stride=d1), first if nn == 0 else band, load)
    for r in range(d2):
        def load(r=r):
            blk_of = lambda ref: ref[0, 0, r].reshape(blk, LANES)
            return (blk_of(q2_ref), jnp.concatenate([blk_of(k2p_ref), blk_of(k2_ref)], axis=0),
                    jnp.concatenate([blk_of(v2p_ref), blk_of(v2_ref)], axis=0))
        add(2, pl.ds(r, blk, stride=d2), first, load)
    return jobs


def _prompt_attn_kernel(*refs):
    (*attn_refs, sgb_ref, mix_ref, o_scr, lse_scr, bias_scr) = refs
    m = pl.program_id(1)
    blk = CHUNK
    qo = lax.broadcasted_iota(jnp.int32, (blk, 2 * blk), 0)
    ko = lax.broadcasted_iota(jnp.int32, (blk, 2 * blk), 1)
    band = (ko >= qo) & (ko <= qo + blk)
    bias_scr[0] = jnp.where(band, 0.0, NEG)
    bias_scr[1] = jnp.where(band & (ko >= blk * (m == 0).astype(jnp.int32)), 0.0, NEG)

    for job in _window_jobs(*attn_refs, o_scr, lse_scr, bias_scr):
        job()

    def combine(c, carry):
        rows = pl.ds(pl.multiple_of(c * blk, blk), blk)
        ls = [lse_scr[g, rows, :] for g in range(3)]
        mx = jnp.maximum(jnp.maximum(ls[0], ls[1]), ls[2])
        es = [jnp.exp(x - mx) for x in ls]
        inv = 1.0 / (es[0] + es[1] + es[2])
        for g in range(3):
            sl = slice(g * LANES, (g + 1) * LANES)
            mix_ref[rows, sl] = (o_scr[g, rows, :] * (es[g] * inv) * sgb_ref[rows, sl]).astype(BF16)
        return carry

    lax.fori_loop(0, ATTN_TILE // blk, combine, 0)


def _prompt_attn(feats, sgb, b, s):
    q0, k0, v0, q1, k1, v1, q2, k2, v2 = feats
    w1, d1 = DIL_PAIRS[1]
    w2, d2 = DIL_PAIRS[2]
    n_tiles = s // ATTN_TILE
    per1 = ATTN_TILE // w1
    prev = lambda m, k: jnp.maximum(m * k - 1, 0)
    nat = pl.BlockSpec((1, ATTN_TILE, LANES), lambda i, m: (i, m, 0))
    nat_p = pl.BlockSpec((1, CHUNK, LANES), lambda i, m: (i, prev(m, ATTN_TILE // CHUNK), 0))
    g1 = pl.BlockSpec((1, per1) + q1.shape[2:], lambda i, m: (i, m, 0, 0, 0, 0))
    g1_p = pl.BlockSpec((1, 1) + q1.shape[2:], lambda i, m: (i, prev(m, per1), 0, 0, 0, 0))
    g2 = pl.BlockSpec((1, 1) + q2.shape[2:], lambda i, m: (i, m, 0, 0, 0, 0))
    g2_p = pl.BlockSpec((1, 1) + q2.shape[2:], lambda i, m: (i, prev(m, 1), 0, 0, 0, 0))
    rows = pl.BlockSpec((ATTN_TILE, ATTN_WIDTH), lambda i, m: (i * n_tiles + m, 0))
    return pl.pallas_call(
        _prompt_attn_kernel,
        grid=(b, n_tiles),
        in_specs=[nat, nat, nat, nat_p, nat_p, g1, g1, g1, g1_p, g1_p, g2, g2, g2, g2_p, g2_p, rows],
        out_specs=rows,
        out_shape=jax.ShapeDtypeStruct((b * s, ATTN_WIDTH), BF16),
        scratch_shapes=[
            pltpu.VMEM((3, ATTN_TILE, LANES), F32),
            pltpu.VMEM((3, ATTN_TILE, LANES), F32),
            pltpu.VMEM((2, CHUNK, 2 * CHUNK), F32),
        ],
        compiler_params=pltpu.CompilerParams(
            dimension_semantics=("arbitrary", "arbitrary"), vmem_limit_bytes=VMEM_LIMIT),
        name="prompt_attn",
    )(q0, k0, v0, k0, v0, q1, k1, v1, k1, v1, q2, k2, v2, k2, v2, sgb)


def _window_offsets(win_refs):
    offs, total = [], 0
    for r in win_refs:
        offs.append(total)
        total += r.shape[2]
    return offs, total


def _stage_bf16(win_refs, mem_ref, kv16_ref, mem16_ref):
    offs, _ = _window_offsets(win_refs)
    for bb in range(mem_ref.shape[0]):
        for wref, off in zip(win_refs, offs):
            kv16_ref[bb, :, off:off + wref.shape[2]] = wref[bb].astype(BF16)
        mem16_ref[bb] = mem_ref[bb].astype(BF16)


class _SampleWindow:
    def __init__(self, wref, oref, kv16_ref, kv16_off, q_ref, k_ref, v_ref, bb, g, t_new):
        self.wref, self.oref, self.bb, self.t_new = wref, oref, bb, t_new
        self.window, self.dil = DIL_PAIRS[g]
        self.length = wref.shape[2]
        rows = slice(bb * t_new, (bb + 1) * t_new)
        sl = slice(g * LANES, (g + 1) * LANES)
        self.load_new = lambda: (q_ref[rows, sl], k_ref[rows, sl], v_ref[rows, sl])
        self.kv16 = lambda half: kv16_ref[bb, half * LANES:(half + 1) * LANES, kv16_off:kv16_off + self.length]

    def scores(self):
        t_new = self.t_new
        qg, kg, vg = self.load_new()
        self.lane_lo = _lane_iota(qg.shape) < HEAD_DIM
        qs = jnp.concatenate([jnp.where(self.lane_lo, qg, 0.0), jnp.where(self.lane_lo, 0.0, qg)],
                             axis=0).astype(BF16)
        zpad = jnp.zeros((CHUNK - t_new, LANES), F32)
        self.k_new = jnp.concatenate([kg, zpad], axis=0)
        self.v_new = jnp.concatenate([vg, zpad], axis=0)
        self.s_old = _dot(qs, self.kv16(0)) * SCALE
        self.s_new = _dot_nt(qs, self.k_new.astype(BF16)) * SCALE

    def softmax(self):
        t_new, dil, window = self.t_new, self.dil, self.window
        s_old, s_new = self.s_old, self.s_new
        tq = lax.broadcasted_iota(jnp.int32, s_old.shape, 0) & (t_new - 1)
        dist = self.length + tq - lax.broadcasted_iota(jnp.int32, s_old.shape, 1)
        ok_old = ((dist & (dil - 1)) == 0) & (dist <= window)
        tqn = lax.broadcasted_iota(jnp.int32, s_new.shape, 0) & (t_new - 1)
        dn = tqn - lax.broadcasted_iota(jnp.int32, s_new.shape, 1)
        ok_new = (dn >= 0) & ((dn & (dil - 1)) == 0) & (dn <= window)
        s_old = jnp.where(ok_old, s_old, NEG)
        s_new = jnp.where(ok_new, s_new, NEG)
        mx = jnp.maximum(jnp.max(s_old, axis=-1, keepdims=True), jnp.max(s_new, axis=-1, keepdims=True))
        e_old = jnp.exp(s_old - mx)
        e_new = jnp.exp(s_new - mx)
        l = jnp.sum(e_old, axis=-1, keepdims=True) + jnp.sum(e_new, axis=-1, keepdims=True)
        self.e_old, self.e_new = e_old.astype(BF16), e_new.astype(BF16)
        self.inv_l = 1.0 / l
        self.lse = mx + jnp.log(l)

    def values(self):
        t_new, lane_lo = self.t_new, self.lane_lo
        pv = (_dot_nt(self.e_old, self.kv16(1)) + _dot(self.e_new, self.v_new.astype(BF16))) * self.inv_l
        out = jnp.where(lane_lo, pv[:t_new], pv[t_new:])
        shape = (t_new, LANES)
        lse_b = jnp.where(lane_lo, jnp.broadcast_to(self.lse[:t_new], shape),
                          jnp.broadcast_to(self.lse[t_new:], shape))
        return out, lse_b

    def shift(self):
        t_new, wref, oref, bb = self.t_new, self.wref, self.oref, self.bb
        _, kg, vg = self.load_new()
        zpad = jnp.zeros((CHUNK - t_new, LANES), F32)
        new_t = jnp.concatenate([jnp.concatenate([kg, zpad], axis=0).T,
                                 jnp.concatenate([vg, zpad], axis=0).T], axis=0)
        keep = _lane_iota((2 * LANES, LANES)) < LANES - t_new
        nxt = pltpu.roll(wref[bb, :, 0:LANES], LANES - t_new, 1)
        n_blk = self.length // LANES
        for c in range(n_blk):
            cur = nxt
            if c + 1 < n_blk:
                nxt = pltpu.roll(wref[bb, :, (c + 1) * LANES:(c + 2) * LANES], LANES - t_new, 1)
            else:
                nxt = pltpu.roll(new_t, LANES - t_new, 1)
            oref[bb, :, c * LANES:(c + 1) * LANES] = jnp.where(keep, cur, nxt)


class _SampleMemory:
    def __init__(self, mem16_ref, qc_ref, bb, t_new):
        self.rows = slice(bb * t_new, (bb + 1) * t_new)
        self.qc_ref = qc_ref
        self.half = lambda h: mem16_ref[bb, h * MEM_WIDTH:(h + 1) * MEM_WIDTH, :]

    def scores(self):
        self.s = _dot(_stack_heads(self.qc_ref[self.rows, :], MEM_HEADS).astype(BF16), self.half(0)) * SCALE

    def softmax(self):
        _, e, l = _softmax_parts(self.s)
        self.e, self.inv_l = e.astype(BF16), 1.0 / l

    def values(self):
        return _unstack_heads(_dot_nt(self.e, self.half(1)) * self.inv_l, MEM_HEADS)


def _sample_finish(outs, lses, cm, sgb_ref, sgc_ref, mix_ref, bb, t_new):
    rows = slice(bb * t_new, (bb + 1) * t_new)
    mx = jnp.maximum(jnp.maximum(lses[0], lses[1]), lses[2])
    es = [jnp.exp(x - mx) for x in lses]
    inv = 1.0 / (es[0] + es[1] + es[2])
    for g in range(len(DIL_PAIRS)):
        sl = slice(g * LANES, (g + 1) * LANES)
        mix_ref[rows, sl] = outs[g] * (es[g] * inv) * sgb_ref[rows, sl]
    mix_ref[rows, ATTN_WIDTH:ATTN_WIDTH + MEM_WIDTH] = cm * sgc_ref[rows, :]


def _out_proj_kernel(x_ref, *refs):
    *mix_refs, w_ref, y_ref = refs
    mix = jnp.concatenate([r[...].astype(BF16) for r in mix_refs], axis=1)
    y_ref[...] = x_ref[...] + _dot(mix, w_ref[...].astype(BF16))


def _out_proj(x, mixes, w_out):
    n = x.shape[0]
    tile = min(OUT_TILE, n)
    assert n % tile == 0
    rows = lambda width: pl.BlockSpec((tile, width), lambda i: (i, 0))
    return pl.pallas_call(
        _out_proj_kernel,
        grid=(n // tile,),
        in_specs=[rows(D_MODEL)] + [rows(mx.shape[1]) for mx in mixes]
        + [pl.BlockSpec((MIX_WIDTH, D_MODEL), lambda i: (0, 0))],
        out_specs=rows(D_MODEL),
        out_shape=jax.ShapeDtypeStruct((n, D_MODEL), F32),
        compiler_params=pltpu.CompilerParams(
            dimension_semantics=("arbitrary",), vmem_limit_bytes=VMEM_LIMIT),
        name="out_proj",
    )(x, *mixes, w_out)


def kernel(x_prompt, x_sample, state_win0_kv, state_win1_kv, state_win2_kv, cache_mem_kv, mem_prompt, norm_gain, w_in, gmlp_ln_gain, gmlp_ln_bias, gmlp_w_s, gmlp_b_s, attn_q_norm, attn_k_norm, mem_norm, w_mem_kv, mem_q_norm, mem_k_norm, w_out):
    depth = norm_gain.shape[0]
    assert depth == 1, "single-layer step only"
    bp, s, _ = x_prompt.shape
    bd, t, _ = x_sample.shape
    past_len = PAST_LEN
    assert s % ATTN_TILE == 0 and (bd * t) % SAMPLE_TILE == 0
    assert CHUNK % t == 0 and t & (t - 1) == 0 and DIL_PAIRS[1][0] % ROW_TILE == 0
    states = (state_win0_kv, state_win1_kv, state_win2_kv)
    for st, (w, d) in zip(states, DIL_PAIRS):
        assert st.shape[2] == w and w // d == CHUNK and d & (d - 1) == 0

    common = _common_proj_operands(norm_gain[0], w_in[0], gmlp_ln_gain[0], gmlp_ln_bias[0],
                                   attn_q_norm[0], attn_k_norm[0], mem_q_norm[0])
    tril = jnp.tril(jnp.ones((CHUNK, CHUNK), dtype=bool))
    ws = jnp.where(tril[None], gmlp_w_s[0], 0.0)
    b_s = gmlp_b_s[0]

    reps = CHUNK // t
    ws_s = jnp.stack([jnp.kron(jnp.eye(reps, dtype=F32), ws[g, :t, :t]) for g in range(N_GMLP_GROUPS)])
    bias_s = jnp.repeat(jnp.tile(b_s[:, :t], (1, reps)).T, HEAD_DIM, axis=1)
    cos_s, sin_s = _rope_tables(past_len + (jnp.arange(SAMPLE_TILE, dtype=jnp.int32) % t))
    mixa_s, vln_s, q_s, k_s, v_s, qc_s, sgb_s, sgc_s = _sample_proj(
        x_sample.reshape(bd * t, D_MODEL), common, cos_s, sin_s, _pair_ws(ws_s), bias_s)
    feat_major = lambda a: jnp.transpose(a, (0, 2, 3, 4, 1)).reshape(a.shape[0], -1, a.shape[1])
    wins = [feat_major(st[0]) for st in states]
    mem_s = feat_major(cache_mem_kv[0])

    mem_kv_p, mem_k, mem_v = _mem_kv(mem_prompt, mem_norm[0], w_mem_kv[0], mem_k_norm[0])
    rope_base = jnp.stack(_rope_tables(jnp.arange(0, s, ROW_TILE, dtype=jnp.int32)), axis=1)
    rope_row = jnp.stack(_rope_tables(jnp.arange(ROW_TILE, dtype=jnp.int32)), axis=0)
    bias_p = jnp.repeat(b_s.T, HEAD_DIM, axis=1)
    outs = _prompt_proj(x_prompt, common, rope_base, rope_row, _pair_ws(ws), bias_p, mem_k, mem_v,
                        wins, mem_s, (q_s, k_s, v_s, qc_s, sgb_s, sgc_s), t)
    mixa_p, mixc_p, sgb_p = outs[0:3]
    tails = outs[12:15]
    new_wins_t = outs[15:19]
    mixb_p = _prompt_attn(outs[3:12], sgb_p, bp, s)
    y_prompt = _out_proj(x_prompt.reshape(bp * s, D_MODEL), (mixa_p, mixb_p, mixc_p), w_out[0])

    mixbc_s = new_wins_t[3]
    heads = LANES // HEAD_DIM
    nw0, nw1, nw2 = [
        jnp.transpose(w.reshape(bd, 2, heads, HEAD_DIM, w.shape[2]), (0, 4, 1, 2, 3)) for w in new_wins_t[:3]]
    y_sample = _out_proj(x_sample.reshape(bd * t, D_MODEL), (mixa_s, mixbc_s), w_out[0])

    win_shape = lambda n, w: (1, n, w, 2, heads, HEAD_DIM)
    return (
        y_prompt.reshape(bp, s, D_MODEL),
        y_sample.reshape(bd, t, D_MODEL),
        tails[0].reshape(win_shape(bp, DIL_PAIRS[0][0])),
        tails[1].reshape(win_shape(bp, DIL_PAIRS[1][0])),
        tails[2].reshape(win_shape(bp, DIL_PAIRS[2][0])),
        mem_kv_p.reshape(1, bp, N_MEM, 2, MEM_HEADS, HEAD_DIM),
        nw0.reshape(win_shape(bd, DIL_PAIRS[0][0])),
        nw1.reshape(win_shape(bd, DIL_PAIRS[1][0])),
        nw2.reshape(win_shape(bd, DIL_PAIRS[2][0])),
        vln_s.reshape(1, bd, t, N_GMLP_GROUPS, HEAD_DIM),
    )
```
